```python
import math
import jax
import jax.numpy as jnp
from jax import lax
import numpy as np

D_MODEL = 4096
BATCH = 2
SEQ = 8192
DEPTH = 4

GRID_W = 64
CTX_LEN = 256
N_MIXERS = 2
N_SSD_LAYERS = (DEPTH + N_MIXERS - 1) // N_MIXERS
N_GDN_LAYERS = DEPTH // N_MIXERS
ADA_RANK = 256
N_MOD = 6
D_FF = ((8 * D_MODEL + 3 * 256 - 1) // (3 * 256)) * 256
EPS = 1e-6
F32 = jnp.float32

SSD_D_INNER = D_MODEL
SSD_HEAD_DIM = 64
SSD_HEADS = SSD_D_INNER // SSD_HEAD_DIM
SSD_GROUPS = 8
SSD_STATE = 128
SSD_CONV = 5
SSD_CHUNK = 128
SSD_BC_DIM = SSD_GROUPS * SSD_STATE
SSD_CONV_DIM = SSD_D_INNER + 2 * SSD_BC_DIM
SSD_IN_DIM = SSD_D_INNER + SSD_CONV_DIM + 2 * SSD_HEADS

GDN_HEAD_K = 128
GDN_HEAD_V = 128
GDN_K_HEADS = D_MODEL // 256
GDN_V_HEADS = D_MODEL // 128
GDN_KEY_DIM = GDN_K_HEADS * GDN_HEAD_K
GDN_VALUE_DIM = GDN_V_HEADS * GDN_HEAD_V
GDN_CONV = 5
GDN_CHUNK = 64
GDN_CONV_DIM = 2 * GDN_KEY_DIM + GDN_VALUE_DIM
GDN_IN_DIM = GDN_CONV_DIM + GDN_VALUE_DIM + 4 * GDN_V_HEADS

kernel_name = "hybrid_bissd_gdn_diffusion_block"


def rms_norm(t, w, eps=EPS):
    tf = t.astype(F32)
    tf = tf * lax.rsqrt(jnp.mean(tf * tf, axis=-1, keepdims=True) + eps)
    return (tf * w.astype(F32)).astype(t.dtype)


def l2_normalize(t, eps=EPS):
    tf = t.astype(F32)
    return tf * lax.rsqrt(jnp.sum(tf * tf, axis=-1, keepdims=True) + eps)


def gated_group_rmsnorm(y, z, w, n_groups, eps=EPS):
    t = y.astype(F32) * jax.nn.silu(z.astype(F32))
    shp = t.shape
    t = t.reshape(*shp[:-1], n_groups, shp[-1] // n_groups)
    t = t * lax.rsqrt(jnp.mean(t * t, axis=-1, keepdims=True) + eps)
    return t.reshape(shp) * w.astype(F32)


def modulate(t, shift, scale):
    return t * (1.0 + scale) + shift


def ada_modulation(cvec, down, up, bias):
    m = (jax.nn.silu(cvec) @ down) @ up + bias
    return jnp.split(m, N_MOD, axis=-1)


def centred_depthwise_conv(t, w):
    k, ch = w.shape
    return lax.conv_general_dilated(t, w[:, None, :].astype(t.dtype), (1,), [(k // 2, k // 2)],
                                    dimension_numbers=("NWC", "WIO", "NWC"), feature_group_count=ch)


def flip_seq(t):
    return jnp.flip(t, axis=1)


def grid_transpose(t, rows, cols):
    b, s, d = t.shape
    return t.reshape(b, rows, cols, d).swapaxes(1, 2).reshape(b, s, d)


def tril_decay(cum):
    q = cum.shape[-1]
    mask = jnp.tril(jnp.ones((q, q), dtype=bool))
    return jnp.exp(jnp.where(mask, cum[..., :, None] - cum[..., None, :], -jnp.inf))


def ssd_chunk_scan(xs, dt, a_neg, bm, cm, init_state, with_output):
    b, s, h, p = xs.shape
    g, n = bm.shape[2], bm.shape[3]
    r = h // g
    q = SSD_CHUNK
    nc = s // q
    xdt = (xs.astype(F32) * dt[..., None]).reshape(b, nc, q, g, r, p)
    bm = bm.astype(F32).reshape(b, nc, q, g, n)
    cm = cm.astype(F32).reshape(b, nc, q, g, n)
    a_cum = jnp.cumsum((dt * a_neg).reshape(b, nc, q, g, r), axis=2)
    a_last = a_cum[:, :, -1]
    chunk_states = jnp.einsum("bclgn,bclgr,bclgrp->bcgrpn", bm, jnp.exp(a_last[:, :, None] - a_cum), xdt)
    if init_state is None:
        init_state = jnp.zeros((b, g, r, p, n), F32)

    def step(state, inp):
        st, dec = inp
        return state * dec[..., None, None] + st, state

    final, states_in = lax.scan(step, init_state,
                                (jnp.moveaxis(chunk_states, 1, 0), jnp.moveaxis(jnp.exp(a_last), 1, 0)))
    if not with_output:
        return None, final
    states_in = jnp.moveaxis(states_in, 0, 1)
    decay = tril_decay(jnp.moveaxis(a_cum, 2, -1))
    cb = jnp.einsum("bclgn,bcsgn->bcgls", cm, bm)
    y_diag = jnp.einsum("bcgrls,bcsgrp->bclgrp", cb[:, :, :, None] * decay, xdt)
    y_off = jnp.einsum("bclgn,bcgrpn->bclgrp", cm, states_in) * jnp.exp(a_cum)[..., None]
    return (y_diag + y_off).reshape(b, s, h, p), final


def gated_delta_chunk_scan(q, k, v, g, beta, init_state, with_output):
    b, s, h, dk = k.shape
    dv = v.shape[-1]
    c = GDN_CHUNK
    nc = s // c

    def chunks(t):
        return jnp.moveaxis(t.astype(F32).reshape(b, nc, c, h, *t.shape[3:]), 3, 1)

    qc = chunks(q) * (dk ** -0.5)
    kc = chunks(k)
    vc = chunks(v)
    bc = chunks(beta)
    g_cum = jnp.cumsum(chunks(g), axis=-1)
    decay = tril_decay(g_cum)
    kb = kc * bc[..., None]
    a_strict = jnp.tril(jnp.einsum("bhnid,bhnjd->bhnij", kb, kc) * decay, -1)
    eye = jnp.eye(c, dtype=F32)
    t_inv = lax.linalg.triangular_solve(a_strict + eye, jnp.broadcast_to(eye, a_strict.shape),
                                        left_side=True, lower=True, unit_diagonal=True)
    u = t_inv @ (vc * bc[..., None])
    w = t_inv @ (kb * jnp.exp(g_cum)[..., None])
    g_last = g_cum[..., -1]
    k_dec = kc * jnp.exp(g_last[..., None] - g_cum)[..., None]
    if init_state is None:
        init_state = jnp.zeros((b, h, dk, dv), F32)

    def to_scan(t):
        return jnp.moveaxis(t, 2, 0)

    def update(state, u_n, w_n, kd_n, dec_n):
        v_new = u_n - w_n @ state
        return state * dec_n[..., None, None] + jnp.swapaxes(kd_n, -1, -2) @ v_new, v_new

    if with_output:
        q_dec = qc * jnp.exp(g_cum)[..., None]
        intra = jnp.einsum("bhnid,bhnjd->bhnij", qc, kc) * decay

        def step_out(state, inp):
            u_n, w_n, kd_n, dec_n, qd_n, intra_n = inp
            new_state, v_new = update(state, u_n, w_n, kd_n, dec_n)
            return new_state, qd_n @ state + intra_n @ v_new

        final, out = lax.scan(step_out, init_state,
                              (to_scan(u), to_scan(w), to_scan(k_dec), to_scan(jnp.exp(g_last)),
                               to_scan(q_dec), to_scan(intra)))
        out = jnp.moveaxis(jnp.moveaxis(out, 0, 2), 1, 3).reshape(b, s, h, dv)
        return out, final

    def step_state(state, inp):
        u_n, w_n, kd_n, dec_n = inp
        new_state, _ = update(state, u_n, w_n, kd_n, dec_n)
        return new_state, None

    final, _ = lax.scan(step_state, init_state,
                        (to_scan(u), to_scan(w), to_scan(k_dec), to_scan(jnp.exp(g_last))))
    return None, final


def ssd_mixer(h_ctx, h_lat, w_in, conv_w, conv_b, a_log, dt_bias, d_skip, norm_w, w_out, ctx_out):
    a_neg = -jnp.exp(a_log.astype(F32))

    def project(h):
        bsz, s, _ = h.shape
        z, xbc, dt = jnp.split(h @ w_in, [SSD_D_INNER, SSD_D_INNER + SSD_CONV_DIM], axis=-1)
        xbc = jax.nn.silu(centred_depthwise_conv(xbc, conv_w) + conv_b)
        xs, bm, cm = jnp.split(xbc, [SSD_D_INNER, SSD_D_INNER + SSD_BC_DIM], axis=-1)
        dt = jax.nn.softplus(dt.astype(F32).reshape(bsz, s, 2, SSD_HEADS) + dt_bias.astype(F32))
        return (z, xs.reshape(bsz, s, SSD_HEADS, SSD_HEAD_DIM), bm.reshape(bsz, s, SSD_GROUPS, SSD_STATE),
                cm.reshape(bsz, s, SSD_GROUPS, SSD_STATE), dt)

    z_c, x_c, b_c, c_c, dt_c = project(h_ctx)
    z_l, x_l, b_l, c_l, dt_l = project(h_lat)
    y_cf, state_f = ssd_chunk_scan(x_c, dt_c[:, :, 0], a_neg[0], b_c, c_c, None, ctx_out)
    y_lf, _ = ssd_chunk_scan(x_l, dt_l[:, :, 0], a_neg[0], b_l, c_l, state_f, True)
    y_cb, state_b = ssd_chunk_scan(flip_seq(x_c), flip_seq(dt_c[:, :, 1]), a_neg[1],
                                   flip_seq(b_c), flip_seq(c_c), None, ctx_out)
    y_lb, _ = ssd_chunk_scan(flip_seq(x_l), flip_seq(dt_l[:, :, 1]), a_neg[1],
                             flip_seq(b_l), flip_seq(c_l), state_b, True)

    def finish(y_f, y_b, xs, z):
        y = y_f + flip_seq(y_b) + d_skip.astype(F32)[:, None] * xs.astype(F32)
        y = gated_group_rmsnorm(y.reshape(z.shape), z, norm_w, SSD_GROUPS)
        return y.astype(z.dtype) @ w_out

    y_lat = finish(y_lf, y_lb, x_l, z_l)
    y_ctx = finish(y_cf, y_cb, x_c, z_c) if ctx_out else None
    return y_ctx, y_lat


def gdn_mixer(h_ctx, h_lat, w_in, conv_w, a_log, dt_bias, norm_w, w_out, ctx_out):
    a_pos = jnp.exp(a_log.astype(F32))
    rep = GDN_V_HEADS // GDN_K_HEADS
    split_at = [GDN_CONV_DIM, GDN_CONV_DIM + GDN_VALUE_DIM, GDN_CONV_DIM + GDN_VALUE_DIM + 2 * GDN_V_HEADS]

    def project(h):
        bsz, s, _ = h.shape
        qkv, z, b_raw, a_raw = jnp.split(h @ w_in, split_at, axis=-1)
        qkv = jax.nn.silu(centred_depthwise_conv(qkv, conv_w))
        q, k, v = jnp.split(qkv, [GDN_KEY_DIM, 2 * GDN_KEY_DIM], axis=-1)
        q = jnp.repeat(l2_normalize(q.reshape(bsz, s, GDN_K_HEADS, GDN_HEAD_K)), rep, axis=2)
        k = jnp.repeat(l2_normalize(k.reshape(bsz, s, GDN_K_HEADS, GDN_HEAD_K)), rep, axis=2)
        v = v.reshape(bsz, s, GDN_V_HEADS, GDN_HEAD_V)
        beta = jax.nn.sigmoid(b_raw.astype(F32).reshape(bsz, s, 2, GDN_V_HEADS))
        g = -a_pos * jax.nn.softplus(a_raw.astype(F32).reshape(bsz, s, 2, GDN_V_HEADS) + dt_bias.astype(F32))
        return z.reshape(bsz, s, GDN_V_HEADS, GDN_HEAD_V), q, k, v, beta, g

    z_c, q_c, k_c, v_c, beta_c, g_c = project(h_ctx)
    z_l, q_l, k_l, v_l, beta_l, g_l = project(h_lat)
    o_cf, state_f = gated_delta_chunk_scan(q_c, k_c, v_c, g_c[:, :, 0], beta_c[:, :, 0], None, ctx_out)
    o_lf, _ = gated_delta_chunk_scan(q_l, k_l, v_l, g_l[:, :, 0], beta_l[:, :, 0], state_f, True)
    o_cb, state_b = gated_delta_chunk_scan(flip_seq(q_c), flip_seq(k_c), flip_seq(v_c), flip_seq(g_c[:, :, 1]),
                                           flip_seq(beta_c[:, :, 1]), None, ctx_out)
    o_lb, _ = gated_delta_chunk_scan(flip_seq(q_l), flip_seq(k_l), flip_seq(v_l), flip_seq(g_l[:, :, 1]),
                                     flip_seq(beta_l[:, :, 1]), state_b, True)

    def finish(o_f, o_b, z):
        o = rms_norm(o_f + flip_seq(o_b), norm_w) * jax.nn.silu(z.astype(F32))
        return o.reshape(z.shape[0], z.shape[1], GDN_VALUE_DIM).astype(z.dtype) @ w_out

    y_lat = finish(o_lf, o_lb, z_l)
    y_ctx = finish(o_cf, o_cb, z_c) if ctx_out else None
    return y_ctx, y_lat


def swiglu_ffn(h, w_in, w_out):
    gate, up = jnp.split(h @ w_in, 2, axis=-1)
    return (jax.nn.silu(gate) * up) @ w_out


def setup_inputs(seed: int = 0) -> dict:
    key = jax.random.key(seed)
    ks = jax.random.split(key, 26)

    def normal(k, shape, scale):
        return jax.random.normal(k, shape, F32) * scale

    def gains(k, shape):
        return 1.0 + 0.02 * jax.random.normal(k, shape, F32)

    def decay_log_rate(k, shape):
        return jnp.log(jax.random.uniform(k, shape, F32, 1.0, 16.0))

    def dt_bias_init(k, shape):
        dt = jnp.exp(jax.random.uniform(k, shape, F32, math.log(1e-3), math.log(1e-1)))
        return dt + jnp.log(-jnp.expm1(-dt))

    return {
        "x": normal(ks[0], (BATCH, SEQ, D_MODEL), 1.0),
        "c": normal(ks[1], (BATCH, D_MODEL), 1.0),
        "ctx": normal(ks[2], (BATCH, CTX_LEN, D_MODEL), 1.0),
        "c_ctx": normal(ks[3], (D_MODEL,), 1.0),
        "ada_down": normal(ks[4], (DEPTH, D_MODEL, ADA_RANK), D_MODEL ** -0.5),
        "ada_up": normal(ks[5], (DEPTH, ADA_RANK, N_MOD * D_MODEL), 0.25 * ADA_RANK ** -0.5),
        "ada_bias": normal(ks[6], (DEPTH, N_MOD * D_MODEL), 0.01),
        "norm_mix": gains(ks[7], (DEPTH, D_MODEL)),
        "norm_ffn": gains(ks[8], (DEPTH, D_MODEL)),
        "ffn_in": normal(ks[9], (DEPTH, D_MODEL, 2 * D_FF), D_MODEL ** -0.5),
        "ffn_out": normal(ks[10], (DEPTH, D_FF, D_MODEL), D_FF ** -0.5),
        "ssd_in": normal(ks[11], (N_SSD_LAYERS, D_MODEL, SSD_IN_DIM), D_MODEL ** -0.5),
        "ssd_conv_w": normal(ks[12], (N_SSD_LAYERS, SSD_CONV, SSD_CONV_DIM), SSD_CONV ** -0.5),
        "ssd_conv_b": normal(ks[13], (N_SSD_LAYERS, SSD_CONV_DIM), 0.01),
        "ssd_a_log": decay_log_rate(ks[14], (N_SSD_LAYERS, 2, SSD_HEADS)),
        "ssd_dt_bias": dt_bias_init(ks[15], (N_SSD_LAYERS, 2, SSD_HEADS)),
        "ssd_d": gains(ks[16], (N_SSD_LAYERS, SSD_HEADS)),
        "ssd_norm": gains(ks[17], (N_SSD_LAYERS, SSD_D_INNER)),
        "ssd_out": normal(ks[18], (N_SSD_LAYERS, SSD_D_INNER, D_MODEL), SSD_D_INNER ** -0.5),
        "gdn_in": normal(ks[19], (N_GDN_LAYERS, D_MODEL, GDN_IN_DIM), D_MODEL ** -0.5),
        "gdn_conv_w": normal(ks[20], (N_GDN_LAYERS, GDN_CONV, GDN_CONV_DIM), GDN_CONV ** -0.5),
        "gdn_a_log": decay_log_rate(ks[21], (N_GDN_LAYERS, 2, GDN_V_HEADS)),
        "gdn_dt_bias": dt_bias_init(ks[22], (N_GDN_LAYERS, 2, GDN_V_HEADS)),
        "gdn_norm": gains(ks[23], (N_GDN_LAYERS, GDN_HEAD_V)),
        "gdn_out": normal(ks[24], (N_GDN_LAYERS, GDN_VALUE_DIM, D_MODEL), GDN_VALUE_DIM ** -0.5),
        "final_norm": gains(ks[25], (D_MODEL,)),
    }


def reference(x, c, ctx, c_ctx, ada_down, ada_up, ada_bias, norm_mix, norm_ffn, ffn_in, ffn_out,
              ssd_in, ssd_conv_w, ssd_conv_b, ssd_a_log, ssd_dt_bias, ssd_d, ssd_norm, ssd_out,
              gdn_in, gdn_conv_w, gdn_a_log, gdn_dt_bias, gdn_norm, gdn_out, final_norm):
    rows = x.shape[1] // GRID_W
    xc = ctx
    for i in range(DEPTH):
        ctx_out = i < DEPTH - 1
        sh1, sc1, g1, sh2, sc2, g2 = ada_modulation(c, ada_down[i], ada_up[i], ada_bias[i])
        csh1, csc1, cg1, csh2, csc2, cg2 = ada_modulation(c_ctx, ada_down[i], ada_up[i], ada_bias[i])

        h_lat = modulate(rms_norm(x, norm_mix[i]), sh1[:, None], sc1[:, None])
        h_ctx = modulate(rms_norm(xc, norm_mix[i]), csh1, csc1)
        col_major = (i // N_MIXERS) % 2 == 1
        if col_major:
            h_lat = grid_transpose(h_lat, rows, GRID_W)
        j = i // N_MIXERS
        if i % N_MIXERS == 0:
            y_ctx, y_lat = ssd_mixer(h_ctx, h_lat, ssd_in[j], ssd_conv_w[j], ssd_conv_b[j], ssd_a_log[j],
                                     ssd_dt_bias[j], ssd_d[j], ssd_norm[j], ssd_out[j], ctx_out)
        else:
            y_ctx, y_lat = gdn_mixer(h_ctx, h_lat, gdn_in[j], gdn_conv_w[j], gdn_a_log[j], gdn_dt_bias[j],
                                     gdn_norm[j], gdn_out[j], ctx_out)
        if col_major:
            y_lat = grid_transpose(y_lat, GRID_W, rows)
        x = x + g1[:, None] * y_lat

        x = x + g2[:, None] * swiglu_ffn(modulate(rms_norm(x, norm_ffn[i]), sh2[:, None], sc2[:, None]),
                                         ffn_in[i], ffn_out[i])
        if ctx_out:
            xc = xc + cg1 * y_ctx
            xc = xc + cg2 * swiglu_ffn(modulate(rms_norm(xc, norm_ffn[i]), csh2, csc2), ffn_in[i], ffn_out[i])
    return rms_norm(x, final_norm)
```

```python
import functools

import jax
import jax.numpy as jnp
from jax import lax
from jax.experimental import pallas as pl
from jax.experimental.pallas import tpu as pltpu

F32 = jnp.float32
BF16 = jnp.bfloat16
EPS = 1e-6

GRID_W = 64
N_MIXERS = 2
N_MOD = 6
SSD_HEAD_DIM = 64
SSD_GROUPS = 8
SSD_STATE = 128
SSD_CHUNK = 128
GDN_HEAD = 128
GDN_CHUNK = 64
GDN_HEADS_PER_STEP = 8

ROW_TILE = 128
CONV_ROWS = 256
HALO = 8
V7X_VMEM_LIMIT = 56 * 1024 * 1024


def _cparams(sem, vmem=None):
    return pltpu.CompilerParams(dimension_semantics=sem, vmem_limit_bytes=vmem or V7X_VMEM_LIMIT)


def _sigmoid(x):
    return 1.0 / (1.0 + jnp.exp(-x))


def _silu(x):
    return x * _sigmoid(x)


def _softplus(x):
    return jnp.maximum(x, 0.0) + jnp.log1p(jnp.exp(-jnp.abs(x)))


def _pick(n, candidates):
    for c in candidates:
        if n % c == 0:
            return c
    raise ValueError(f"no tile for {n} among {candidates}")


def _split3(a):
    hi = a.astype(BF16).astype(F32)
    r1 = a - hi
    mid = r1.astype(BF16).astype(F32)
    lo = (r1 - mid).astype(BF16).astype(F32)
    return hi, mid, lo


def _dot(a, b):
    return jnp.dot(a, b, preferred_element_type=F32)


def _dot_exact_rhs(m, a):
    hi, mid, lo = _split3(a)
    mb = m.astype(BF16)
    return _dot(mb, hi.astype(BF16)) + _dot(mb, mid.astype(BF16)) + _dot(mb, lo.astype(BF16))


def _dot_exact_lhs(a, m):
    hi, mid, lo = _split3(a)
    mb = m.astype(BF16)
    return _dot(hi.astype(BF16), mb) + _dot(mid.astype(BF16), mb) + _dot(lo.astype(BF16), mb)


def _ada_kernel(cv_ref, down_ref, up_ref, bias_ref, o_ref):
    t = jnp.dot(_silu(cv_ref[...]), down_ref[...], preferred_element_type=F32,
                precision=lax.Precision.HIGHEST)
    o_ref[...] = jnp.dot(t, up_ref[...], preferred_element_type=F32,
                         precision=lax.Precision.HIGHEST) + bias_ref[...]


def _ada_modulation(cv, down, up, bias):
    depth, d, r = down.shape
    n = up.shape[-1]
    tn = _pick(n, (6144, 3072, 1536, 768, 512, 256, 128))
    rows = cv.shape[0]
    return pl.pallas_call(
        _ada_kernel,
        grid=(depth, n // tn),
        in_specs=[
            pl.BlockSpec((rows, d), lambda l, j: (0, 0)),
            pl.BlockSpec((None, d, r), lambda l, j: (l, 0, 0)),
            pl.BlockSpec((None, r, tn), lambda l, j: (l, 0, j)),
            pl.BlockSpec((None, 1, tn), lambda l, j: (l, 0, j)),
        ],
        out_specs=pl.BlockSpec((None, rows, tn), lambda l, j: (l, 0, j)),
        out_shape=jax.ShapeDtypeStruct((depth, rows, n), F32),
        compiler_params=_cparams(("arbitrary", "arbitrary")),
        name="ada_modulation",
    )(cv, down, up, bias.reshape(depth, 1, n))


def _norm_kernel(*refs, has_y, n_lat_tiles):
    if has_y:
        (xl_ref, xc_ref, yl_ref, yc_ref, g_ref, w_ref, sh_ref, sc_ref,
         xlo_ref, xco_ref, h_ref) = refs
    else:
        xl_ref, xc_ref, w_ref, sh_ref, sc_ref, h_ref = refs

    def run(x_ref, y_ref, xo_ref):
        x = x_ref[...]
        if has_y:
            x = x + g_ref[...] * y_ref[...]
            xo_ref[...] = x
        xn = x * lax.rsqrt(jnp.mean(x * x, axis=-1, keepdims=True) + EPS) * w_ref[...]
        h_ref[...] = (xn * (1.0 + sc_ref[...]) + sh_ref[...]).astype(h_ref.dtype)

    i = pl.program_id(0)

    @pl.when(i < n_lat_tiles)
    def _():
        run(xl_ref, yl_ref if has_y else None, xlo_ref if has_y else None)

    @pl.when(i >= n_lat_tiles)
    def _():
        run(xc_ref, yc_ref if has_y else None, xco_ref if has_y else None)


def _norm_modulate(cfg, x_lat, x_ctx, y, gate, w, shift, scale, *, to_col_major):
    d, tr = cfg["d"], ROW_TILE
    n_lat, n_ctx = cfg["t_lat"] // tr, cfg["t_ctx"] // tr
    per_batch = cfg["seq"] // tr
    cols = cfg["cols"]
    has_y = y is not None

    def lat_tile(i):
        return jnp.minimum(i, n_lat - 1)

    def ctx_tile(i):
        return jnp.maximum(i - n_lat, 0)

    def stream(i):
        return jnp.where(i < n_lat, i // per_batch, cfg["batch"])

    if to_col_major:
        def strided(i):
            il = lat_tile(i)
            return (il // cols, il % cols)
        xl_in = x_lat.reshape(cfg["t_lat"] // cols, cols * d)
        xl_spec = pl.BlockSpec((tr, d), strided)
    else:
        xl_in = x_lat
        xl_spec = pl.BlockSpec((tr, d), lambda i: (lat_tile(i), 0))
    xc_spec = pl.BlockSpec((tr, d), lambda i: (ctx_tile(i), 0))
    mod_spec = pl.BlockSpec((None, 1, d), lambda i: (stream(i), 0, 0))
    w_spec = pl.BlockSpec((1, d), lambda i: (0, 0))
    h_spec = pl.BlockSpec((tr, d), lambda i: (i, 0))
    h_shape = jax.ShapeDtypeStruct((cfg["t"], d), BF16)

    if has_y:
        if to_col_major:
            yl_in = y.reshape(cfg["t"] // cols, cols * d)
            yl_spec = pl.BlockSpec((tr, d), strided)
        else:
            yl_in = y
            yl_spec = pl.BlockSpec((tr, d), lambda i: (lat_tile(i), 0))
        yc_spec = pl.BlockSpec((tr, d), lambda i: (n_lat + ctx_tile(i), 0))
        ins = (xl_in, x_ctx, yl_in, y, gate, w, shift, scale)
        in_specs = [xl_spec, xc_spec, yl_spec, yc_spec, mod_spec, w_spec, mod_spec, mod_spec]
        out_specs = [pl.BlockSpec((tr, d), lambda i: (lat_tile(i), 0)), xc_spec, h_spec]
        out_shape = [jax.ShapeDtypeStruct((cfg["t_lat"], d), F32),
                     jax.ShapeDtypeStruct((cfg["t_ctx"], d), F32), h_shape]
    else:
        ins = (xl_in, x_ctx, w, shift, scale)
        in_specs = [xl_spec, xc_spec, w_spec, mod_spec, mod_spec]
        out_specs = h_spec
        out_shape = h_shape
    out = pl.pallas_call(
        functools.partial(_norm_kernel, has_y=has_y, n_lat_tiles=n_lat),
        grid=(n_lat + n_ctx,),
        in_specs=in_specs, out_specs=out_specs, out_shape=out_shape,
        compiler_params=_cparams(("arbitrary",)),
        name="norm_modulate",
    )(*ins)
    if has_y:
        return out
    return x_lat, x_ctx, out


def _final_kernel(x_ref, y_ref, g_ref, w_ref, o_ref):
    x = x_ref[...] + g_ref[...] * y_ref[...]
    o_ref[...] = x * lax.rsqrt(jnp.mean(x * x, axis=-1, keepdims=True) + EPS) * w_ref[...]


def _final_norm(cfg, x_lat, y, gate, w, *, from_col_major):
    d, tr = cfg["d"], ROW_TILE
    n_lat = cfg["t_lat"] // tr
    per_batch = cfg["seq"] // tr
    cols = cfg["cols"]
    plain = pl.BlockSpec((tr, d), lambda i: (i, 0))
    if from_col_major:
        out_spec = pl.BlockSpec((tr, d), lambda i: (i // cols, i % cols))
        out_shape = jax.ShapeDtypeStruct((cfg["t_lat"] // cols, cols * d), F32)
    else:
        out_spec, out_shape = plain, jax.ShapeDtypeStruct((cfg["t_lat"], d), F32)
    out = pl.pallas_call(
        _final_kernel,
        grid=(n_lat,),
        in_specs=[plain, plain,
                  pl.BlockSpec((None, 1, d), lambda i: (i // per_batch, 0, 0)),
                  pl.BlockSpec((1, d), lambda i: (0, 0))],
        out_specs=out_spec, out_shape=out_shape,
        compiler_params=_cparams(("arbitrary",)),
        name="final_norm",
    )(x_lat, y, gate, w)
    return out.reshape(cfg["t_lat"], d)


def _mm_kernel(a_ref, w_ref, o_ref):
    o_ref[...] = _dot(a_ref[...], w_ref[...]).astype(o_ref.dtype)


def _matmul(a, w, col_lo, col_hi, out_dtype, tm, tn, name):
    m, k = a.shape
    n = col_hi - col_lo
    assert m % tm == 0 and n % tn == 0 and col_lo % tn == 0
    off = col_lo // tn
    return pl.pallas_call(
        _mm_kernel,
        grid=(m // tm, n // tn),
        in_specs=[pl.BlockSpec((tm, k), lambda i, j: (i, 0)),
                  pl.BlockSpec((k, tn), lambda i, j: (0, off + j))],
        out_specs=pl.BlockSpec((tm, tn), lambda i, j: (i, j)),
        out_shape=jax.ShapeDtypeStruct((m, n), out_dtype),
        compiler_params=_cparams(("arbitrary", "arbitrary")),
        name=name,
    )(a, w)


def _swiglu_kernel(a_ref, wg_ref, wu_ref, o_ref):
    a = a_ref[...]
    g = _dot(a, wg_ref[...])
    u = _dot(a, wu_ref[...])
    o_ref[...] = (_silu(g) * u).astype(o_ref.dtype)


def _matmul_swiglu(a, w, tm, tn):
    m, k = a.shape
    d_ff = w.shape[1] // 2
    assert m % tm == 0 and d_ff % tn == 0
    nj = d_ff // tn
    return pl.pallas_call(
        _swiglu_kernel,
        grid=(m // tm, nj),
        in_specs=[pl.BlockSpec((tm, k), lambda i, j: (i, 0)),
                  pl.BlockSpec((k, tn), lambda i, j: (0, j)),
                  pl.BlockSpec((k, tn), lambda i, j: (0, nj + j))],
        out_specs=pl.BlockSpec((tm, tn), lambda i, j: (i, j)),
        out_shape=jax.ShapeDtypeStruct((m, d_ff), BF16),
        compiler_params=_cparams(("arbitrary", "arbitrary")),
        name="ffn_in_swiglu",
    )(a, w, w)


def _conv_kernel(prev_ref, cur_ref, next_ref, w_ref, b_ref, o_ref, scr_ref, *,
                 tiles_per_lat_seq, n_lat_tiles, tiles_per_ctx_seq, n_norm_tiles, n_q_tiles, q_scale):
    i, j = pl.program_id(0), pl.program_id(1)
    ts = cur_ref.shape[0]
    ksz = w_ref.shape[0]
    is_lat = i < n_lat_tiles
    pos = jnp.where(is_lat, i % tiles_per_lat_seq, (i - n_lat_tiles) % tiles_per_ctx_seq)
    per_seq = jnp.where(is_lat, tiles_per_lat_seq, tiles_per_ctx_seq)
    keep_prev = jnp.where(pos == 0, 0.0, 1.0)
    keep_next = jnp.where(pos == per_seq - 1, 0.0, 1.0)
    scr_ref[0:HALO, :] = prev_ref[...].astype(F32) * keep_prev
    scr_ref[HALO:HALO + ts, :] = cur_ref[...].astype(F32)
    scr_ref[HALO + ts:2 * HALO + ts, :] = next_ref[...].astype(F32) * keep_next
    w = w_ref[...]
    acc = b_ref[...] + w[0:1, :] * scr_ref[pl.ds(HALO - ksz // 2, ts), :]
    for k in range(1, ksz):
        acc = acc + w[k:k + 1, :] * scr_ref[pl.ds(HALO - ksz // 2 + k, ts), :]
    y = _silu(acc)

    if n_norm_tiles == 0:
        o_ref[...] = y.astype(o_ref.dtype)
    else:
        @pl.when(j >= n_norm_tiles)
        def _():
            o_ref[...] = y.astype(o_ref.dtype)

        @pl.when(j < n_norm_tiles)
        def _():
            scale = jnp.where(j < n_q_tiles, q_scale, 1.0)
            for h in range(y.shape[1] // GDN_HEAD):
                yh = y[:, h * GDN_HEAD:(h + 1) * GDN_HEAD]
                r = lax.rsqrt(jnp.sum(yh * yh, axis=-1, keepdims=True) + EPS) * scale
                o_ref[:, h * GDN_HEAD:(h + 1) * GDN_HEAD] = (yh * r).astype(o_ref.dtype)


def _conv_silu(cfg, src, col_lo, n_ch, w, b, *, norm_ch=0, q_ch=0, q_scale=1.0):
    t, ts = cfg["t"], CONV_ROWS
    tc = next(c for c in (512, 256, 128)
              if all(v % c == 0 for v in (n_ch, col_lo, norm_ch, q_ch)))
    assert cfg["seq"] % ts == 0 and cfg["ctx_len"] % ts == 0
    off = col_lo // tc
    hb = ts // HALO
    last_hb = t // HALO - 1
    kern = functools.partial(
        _conv_kernel, tiles_per_lat_seq=cfg["seq"] // ts, n_lat_tiles=cfg["t_lat"] // ts,
        tiles_per_ctx_seq=cfg["ctx_len"] // ts, n_norm_tiles=norm_ch // tc, n_q_tiles=q_ch // tc,
        q_scale=q_scale)
    return pl.pallas_call(
        kern,
        grid=(t // ts, n_ch // tc),
        in_specs=[
            pl.BlockSpec((HALO, tc), lambda i, j: (jnp.maximum(i * hb - 1, 0), off + j)),
            pl.BlockSpec((ts, tc), lambda i, j: (i, off + j)),
            pl.BlockSpec((HALO, tc), lambda i, j: (jnp.minimum((i + 1) * hb, last_hb), off + j)),
            pl.BlockSpec((w.shape[0], tc), lambda i, j: (0, j)),
            pl.BlockSpec((1, tc), lambda i, j: (0, j)),
        ],
        out_specs=pl.BlockSpec((ts, tc), lambda i, j: (i, j)),
        out_shape=jax.ShapeDtypeStruct((t, n_ch), BF16),
        scratch_shapes=[pltpu.VMEM((ts + 2 * HALO, tc), F32)],
        compiler_params=_cparams(("arbitrary", "arbitrary")),
        name="conv_silu",
    )(src, src, src, w, b.reshape(1, n_ch))


def _prep_kernel(tail_ref, bias_ref, amul_ref, p1c_ref, p2c_ref, p1r_ref, p2r_ref, *,
                 kind, chunk, p2_off, split_lane, hps):
    tr = tail_ref.shape[0]
    raw = tail_ref[...]
    sp = _softplus(raw + bias_ref[...])
    if kind == "ssd":
        p1 = sp
    else:
        p1 = _sigmoid(raw)
    a = sp * amul_ref[...]
    row = lax.broadcasted_iota(jnp.int32, (tr, tr), 0)
    col = lax.broadcasted_iota(jnp.int32, (tr, tr), 1)
    same = (row // chunk) == (col // chunk)
    lower = jnp.where(same & (col <= row), 1.0, 0.0)
    upper = jnp.where(same & (col >= row), 1.0, 0.0)
    lane = lax.broadcasted_iota(jnp.int32, raw.shape, 1)
    cum = jnp.where(lane < split_lane, _dot_exact_rhs(lower, a), _dot_exact_rhs(upper, a))
    p1t = p1.T
    cumt = cum.T
    for n in range(p1c_ref.shape[0]):
        p1c_ref[n] = p1[:, n * hps:(n + 1) * hps]
        p2c_ref[n] = cum[:, p2_off + n * hps:p2_off + (n + 1) * hps]
        p1r_ref[n] = p1t[n * hps:(n + 1) * hps, :]
        p2r_ref[n] = cumt[p2_off + n * hps:p2_off + (n + 1) * hps, :]


def _prep(cfg, tail, bias_full, amul_full, *, kind, chunk, n_heads, hps):
    t, tr = cfg["t"], ROW_TILE
    nblk = 2 * n_heads // hps
    p2_off = 0 if kind == "ssd" else 2 * n_heads
    split_lane = p2_off + n_heads
    kern = functools.partial(_prep_kernel, kind=kind, chunk=chunk, p2_off=p2_off,
                             split_lane=split_lane, hps=hps)
    col_spec = pl.BlockSpec((nblk, tr, hps), lambda i: (0, i, 0))
    row_spec = pl.BlockSpec((nblk, hps, tr), lambda i: (0, 0, i))
    col_shape = jax.ShapeDtypeStruct((nblk, t, hps), F32)
    row_shape = jax.ShapeDtypeStruct((nblk, hps, t), F32)
    return pl.pallas_call(
        kern,
        grid=(t // tr,),
        in_specs=[pl.BlockSpec((tr, 128), lambda i: (i, 0)),
                  pl.BlockSpec((1, 128), lambda i: (0, 0)),
                  pl.BlockSpec((1, 128), lambda i: (0, 0))],
        out_specs=[col_spec, col_spec, row_spec, row_spec],
        out_shape=[col_shape, col_shape, row_shape, row_shape],
        compiler_params=_cparams(("arbitrary",)),
        name=f"prep_{kind}",
    )(tail, bias_full, amul_full)


def _fwd_block(cfg, b, i):
    ncc, ncl = cfg["ctx_len"] // ROW_TILE, cfg["seq"] // ROW_TILE
    ctx = cfg["batch"] * ncl + b * ncc + i
    lat = b * ncl + (i - ncc)
    return jnp.where(i < ncc, ctx, lat)


def _bwd_block(cfg, b, i):
    ncc, ncl = cfg["ctx_len"] // ROW_TILE, cfg["seq"] // ROW_TILE
    ctx = cfg["batch"] * ncl + b * ncc + (ncc - 1 - i)
    lat = b * ncl + (ncl - 1 - (i - ncc))
    return jnp.where(i < ncc, ctx, lat)


def _ssd_kernel(x_ref, b_ref, c_ref, z_ref, dtc_f_ref, dtc_b_ref, cuc_f_ref, cuc_b_ref,
                dtr_f_ref, dtr_b_ref, cur_f_ref, cur_b_ref, dskip_ref, nw_ref,
                o_ref, sf_ref, sb_ref, store_ref, *, nb, ncc):
    i = pl.program_id(2)
    q = x_ref.shape[0]
    hpg = dtc_f_ref.shape[-1]
    width = x_ref.shape[1]
    hd = width // hpg
    expand = jnp.where(lax.broadcasted_iota(jnp.int32, (hpg, width), 1) // hd
                       == lax.broadcasted_iota(jnp.int32, (hpg, width), 0), 1.0, 0.0)

    def widen(a):
        return _dot_exact_lhs(a, expand)

    x = x_ref[...]
    xf = x.astype(F32)
    bm = b_ref[...]

    @pl.when(i == 0)
    def _():
        sb_ref[...] = jnp.zeros_like(sb_ref)

    @pl.when(i == nb)
    def _():
        sf_ref[...] = jnp.zeros_like(sf_ref)

    @pl.when(i < nb)
    def _():
        cub = cuc_b_ref[...]
        tot = cub[0:1, :]
        wgt = widen(dtc_b_ref[...] * jnp.exp(tot - cub))
        dec = widen(jnp.exp(cub))[0:1, :]
        cs = lax.dot_general(bm, (xf * wgt).astype(BF16), (((0,), (0,)), ((), ())),
                             preferred_element_type=F32)
        store_ref[i] = sb_ref[...].astype(BF16)
        sb_ref[...] = sb_ref[...] * dec + cs

    @pl.when(i >= nb)
    def _():
        i2 = i - nb
        slot = jnp.where(i2 < ncc, ncc - 1 - i2, nb - 1 - (i2 - ncc))
        cm = c_ref[...]
        cuf, cub = cuc_f_ref[...], cuc_b_ref[...]
        totf = cuf[q - 1:q, :]
        ef = widen(jnp.exp(cuf))
        eb = widen(jnp.exp(cub))
        wf = widen(dtc_f_ref[...] * jnp.exp(totf - cuf))
        y = dskip_ref[...] * xf
        y = y + _dot(cm, sf_ref[...].astype(BF16)) * ef
        y = y + _dot(cm, store_ref[slot]) * eb
        cs = lax.dot_general(bm, (xf * wf).astype(BF16), (((0,), (0,)), ((), ())),
                             preferred_element_type=F32)
        sf_ref[...] = sf_ref[...] * ef[q - 1:q, :] + cs

        cb = lax.dot_general(cm, bm, (((1,), (1,)), ((), ())), preferred_element_type=F32)
        li = lax.broadcasted_iota(jnp.int32, (q, q), 0)
        si = lax.broadcasted_iota(jnp.int32, (q, q), 1)
        causal, anti = si <= li, si >= li
        lane = lax.broadcasted_iota(jnp.int32, (q, 2 * hd), 1)
        dtr_f, dtr_b = dtr_f_ref[...], dtr_b_ref[...]
        cur_f, cur_b = cur_f_ref[...], cur_b_ref[...]
        pieces = []
        for p in range(hpg // 2):
            xp = x[:, p * 2 * hd:(p + 1) * 2 * hd]
            acc = None
            for half in range(2):
                r = 2 * p + half
                df = jnp.where(causal, jnp.exp(jnp.where(causal, cuf[:, r:r + 1] - cur_f[r:r + 1, :], 0.0)), 0.0)
                db = jnp.where(anti, jnp.exp(jnp.where(anti, cub[:, r:r + 1] - cur_b[r:r + 1, :], 0.0)), 0.0)
                mix = (cb * (df * dtr_f[r:r + 1, :] + db * dtr_b[r:r + 1, :])).astype(BF16)
                keep = (lane < hd) if half == 0 else (lane >= hd)
                part = _dot(mix, jnp.where(keep, xp, jnp.zeros_like(xp)))
                acc = part if acc is None else acc + part
            pieces.append(acc)
        y = y + jnp.concatenate(pieces, axis=1)
        g = y * _silu(z_ref[...].astype(F32))
        o_ref[...] = (g * lax.rsqrt(jnp.mean(g * g, axis=-1, keepdims=True) + EPS)
                      * nw_ref[...]).astype(o_ref.dtype)


def _ssd_scan(cfg, xbc, zx, dtc, cuc, dtr, cur, d_skip, norm_w):
    t, tr, batch = cfg["t"], ROW_TILE, cfg["batch"]
    g, n = SSD_GROUPS, SSD_STATE
    inner = cfg["d"]
    width = inner // g
    hpg = width // SSD_HEAD_DIM
    assert hpg % 2 == 0 and tr == SSD_CHUNK
    ncc, ncl = cfg["ctx_len"] // tr, cfg["seq"] // tr
    nb = ncc + ncl
    b_off, c_off = inner // n, inner // n + g

    def blk(b, i):
        return jnp.where(i < nb, _bwd_block(cfg, b, i), _fwd_block(cfg, b, i - nb))

    def oblk(b, i):
        return _fwd_block(cfg, b, jnp.maximum(i - nb, 0))

    col = lambda d: pl.BlockSpec((None, tr, hpg), lambda b, gi, i: (d * g + gi, blk(b, i), 0))
    row = lambda d: pl.BlockSpec((None, hpg, tr), lambda b, gi, i: (d * g + gi, 0, blk(b, i)))
    kern = functools.partial(_ssd_kernel, nb=nb, ncc=ncc)
    return pl.pallas_call(
        kern,
        grid=(batch, g, 2 * nb),
        in_specs=[
            pl.BlockSpec((tr, width), lambda b, gi, i: (blk(b, i), gi)),
            pl.BlockSpec((tr, n), lambda b, gi, i: (blk(b, i), b_off + gi)),
            pl.BlockSpec((tr, n), lambda b, gi, i: (blk(b, i), c_off + gi)),
            pl.BlockSpec((tr, width), lambda b, gi, i: (oblk(b, i), gi)),
            col(0), col(1), col(0), col(1), row(0), row(1), row(0), row(1),
            pl.BlockSpec((1, width), lambda b, gi, i: (0, gi)),
            pl.BlockSpec((1, width), lambda b, gi, i: (0, gi)),
        ],
        out_specs=pl.BlockSpec((tr, width), lambda b, gi, i: (oblk(b, i), gi)),
        out_shape=jax.ShapeDtypeStruct((t, inner), BF16),
        scratch_shapes=[pltpu.VMEM((n, width), F32), pltpu.VMEM((n, width), F32),
                        pltpu.VMEM((nb, n, width), BF16)],
        compiler_params=_cparams(("arbitrary", "arbitrary", "arbitrary")),
        name="ssd_scan",
    )(xbc, xbc, xbc, zx, dtc, dtc, cuc, cuc, dtr, dtr, cur, cur, d_skip, norm_w)


def _gdn_kernel(*refs, rev, finish):
    if finish:
        (q_ref, k_ref, v_ref, bc_ref, gc_ref, br_ref, gr_ref, prev_ref, z_ref, nw_ref,
         o_ref, s_ref) = refs
    else:
        q_ref, k_ref, v_ref, bc_ref, gc_ref, br_ref, gr_ref, o_ref, s_ref = refs
    c = GDN_CHUNK
    hw = GDN_HEAD
    n_heads = s_ref.shape[0]
    rep = n_heads // (k_ref.shape[1] // hw)

    @pl.when(pl.program_id(2) == 0)
    def _():
        s_ref[...] = jnp.zeros_like(s_ref)

    ii = lax.broadcasted_iota(jnp.int32, (c, c), 0)
    jj = lax.broadcasted_iota(jnp.int32, (c, c), 1)
    incl = (jj >= ii) if rev else (jj <= ii)
    strict = (jj > ii) if rev else (jj < ii)
    eye = jnp.where(ii == jj, 1.0, 0.0)
    n_chunks = q_ref.shape[0] // c
    order = range(n_chunks - 1, -1, -1) if rev else range(n_chunks)
    last = 0 if rev else c - 1

    for ci in order:
        rows = slice(ci * c, (ci + 1) * c)
        bcol, gcol = bc_ref[rows, :], gc_ref[rows, :]
        brow, grow = br_ref[:, rows], gr_ref[:, rows]
        for kh in range(n_heads // rep):
            qh = q_ref[rows, kh * hw:(kh + 1) * hw]
            kk_in = k_ref[rows, kh * hw:(kh + 1) * hw]
            qk_kk = lax.dot_general(jnp.concatenate([qh, kk_in], axis=0), kk_in,
                                    (((1,), (1,)), ((), ())), preferred_element_type=F32)
            qk, kk = qk_kk[:c], qk_kk[c:]
            kf = kk_in.astype(F32)
            for h in range(kh * rep, (kh + 1) * rep):
                gc1, gr1 = gcol[:, h:h + 1], grow[h:h + 1, :]
                bc1, br1 = bcol[:, h:h + 1], brow[h:h + 1, :]
                dec = jnp.where(incl, jnp.exp(jnp.where(incl, gc1 - gr1, 0.0)), 0.0)
                a = jnp.where(strict, bc1 * kk * dec, 0.0)
                inv = eye - a
                pw = a.astype(BF16)
                steps = c.bit_length() - 1
                for s in range(1, steps):
                    pw_f = _dot(pw, pw)
                    pw = pw_f.astype(BF16)
                    inv = inv + _dot(inv.astype(BF16), pw)
                gtot = gc1[last:last + 1, :]
                u = _dot((inv * br1).astype(BF16), v_ref[rows, h * hw:(h + 1) * hw])
                w = _dot((inv * (br1 * jnp.exp(gr1))).astype(BF16), kk_in)
                st = s_ref[h]
                ws_qs = _dot(jnp.concatenate([w.astype(BF16), qh], axis=0), st.astype(BF16))
                v_new = (u - ws_qs[:c]).astype(BF16)
                out = jnp.exp(gc1) * ws_qs[c:] + _dot((qk * dec).astype(BF16), v_new)
                k_dec = (kf * jnp.exp(gtot - gc1)).astype(BF16)
                s_ref[h] = st * jnp.exp(gtot) + lax.dot_general(
                    k_dec, v_new, (((0,), (0,)), ((), ())), preferred_element_type=F32)
                cols = slice(h * hw, (h + 1) * hw)
                if finish:
                    o = out + prev_ref[rows, cols]
                    o = o * lax.rsqrt(jnp.mean(o * o, axis=-1, keepdims=True) + EPS) * nw_ref[...]
                    o_ref[rows, cols] = (o * _silu(z_ref[rows, cols].astype(F32))).astype(o_ref.dtype)
                else:
                    o_ref[rows, cols] = out


def _gdn_scan(cfg, qkv, bc, gc, br, gr, *, rev, prev=None, qkvz=None, norm_w=None):
    t, tr, batch = cfg["t"], ROW_TILE, cfg["batch"]
    hw = GDN_HEAD
    hv = cfg["d"] // hw
    hk = hv // 2
    hps = min(GDN_HEADS_PER_STEP, hv)
    kps = hps // 2
    nhb = hv // hps
    nb = (cfg["ctx_len"] + cfg["seq"]) // tr
    finish = prev is not None
    d = 1 if rev else 0
    order = _bwd_block if rev else _fwd_block

    def blk(b, i):
        return order(cfg, b, i)

    k_off = hk // kps
    v_off = 2 * hk // hps
    z_off = (2 * hk + hv) // hps
    col = pl.BlockSpec((None, tr, hps), lambda b, hb, i: (d * nhb + hb, blk(b, i), 0))
    row = pl.BlockSpec((None, hps, tr), lambda b, hb, i: (d * nhb + hb, 0, blk(b, i)))
    wide = lambda off: pl.BlockSpec((tr, hps * hw), lambda b, hb, i: (blk(b, i), off + hb))
    in_specs = [
        pl.BlockSpec((tr, kps * hw), lambda b, hb, i: (blk(b, i), hb)),
        pl.BlockSpec((tr, kps * hw), lambda b, hb, i: (blk(b, i), k_off + hb)),
        wide(v_off), col, col, row, row,
    ]
    ins = [qkv, qkv, qkv, bc, gc, br, gr]
    if finish:
        in_specs += [wide(0), wide(z_off), pl.BlockSpec((1, hw), lambda b, hb, i: (0, 0))]
        ins += [prev, qkvz, norm_w]
    return pl.pallas_call(
        functools.partial(_gdn_kernel, rev=rev, finish=finish),
        grid=(batch, nhb, nb),
        in_specs=in_specs,
        out_specs=wide(0),
        out_shape=jax.ShapeDtypeStruct((t, hv * hw), BF16 if finish else F32),
        scratch_shapes=[pltpu.VMEM((hps, hw, hw), F32)],
        compiler_params=_cparams(("arbitrary", "arbitrary", "arbitrary")),
        name="gdn_scan_bwd" if rev else "gdn_scan_fwd",
    )(*ins)


def _pad_lanes(v, width=128):
    v = v.reshape(1, -1).astype(F32)
    return jnp.pad(v, ((0, 0), (0, width - v.shape[1])))


def _pad_cols(w, width=128):
    return jnp.pad(w, ((0, 0), (0, width - w.shape[1])))


def _ssd_mixer(cfg, h, w_in, conv_w, conv_b, a_log, dt_bias, d_skip, norm_w, w_out):
    d = cfg["d"]
    heads = d // SSD_HEAD_DIM
    bc_dim = SSD_GROUPS * SSD_STATE
    main = 2 * d + 2 * bc_dim
    w_main = w_in[:, :main].astype(BF16)
    w_tail = _pad_cols(w_in[:, main:]).astype(BF16)
    tm = cfg["tm"]
    zx = _matmul(h, w_main, 0, main, BF16, tm, _pick(main, (512, 256, 128)), "ssd_in_proj")
    tail = _matmul(h, w_tail, 0, 128, F32, tm, 128, "ssd_in_proj_dt")
    xbc = _conv_silu(cfg, zx, d, d + 2 * bc_dim, conv_w, conv_b)
    a_neg = -jnp.exp(a_log.astype(F32))
    dtc, cuc, dtr, cur = _prep(cfg, tail, _pad_lanes(dt_bias), _pad_lanes(a_neg),
                               kind="ssd", chunk=SSD_CHUNK, n_heads=heads, hps=heads // SSD_GROUPS)
    y = _ssd_scan(cfg, xbc, zx, dtc, cuc, dtr, cur,
                  jnp.repeat(d_skip.astype(F32), SSD_HEAD_DIM).reshape(1, d), norm_w.reshape(1, d))
    return _matmul(y, w_out.astype(BF16), 0, d, F32, tm, _pick(d, (512, 256, 128)), "ssd_out_proj")


def _gdn_mixer(cfg, h, w_in, conv_w, a_log, dt_bias, norm_w, w_out):
    d = cfg["d"]
    hv = d // GDN_HEAD
    key_dim = (hv // 2) * GDN_HEAD
    conv_dim = 2 * key_dim + d
    main = conv_dim + d
    w_main = w_in[:, :main].astype(BF16)
    w_tail = _pad_cols(w_in[:, main:]).astype(BF16)
    tm = cfg["tm"]
    qkvz = _matmul(h, w_main, 0, main, BF16, tm, _pick(main, (512, 256, 128)), "gdn_in_proj")
    tail = _matmul(h, w_tail, 0, 128, F32, tm, 128, "gdn_in_proj_gates")
    qkv = _conv_silu(cfg, qkvz, 0, conv_dim, conv_w, jnp.zeros((conv_dim,), F32),
                     norm_ch=2 * key_dim, q_ch=key_dim, q_scale=GDN_HEAD ** -0.5)
    a_pos = jnp.exp(a_log.astype(F32)).reshape(-1)
    zeros = jnp.zeros((2 * hv,), F32)
    bias_full = _pad_lanes(jnp.concatenate([zeros, dt_bias.astype(F32).reshape(-1)]))
    amul_full = _pad_lanes(jnp.concatenate([zeros, -a_pos]))
    hps = min(GDN_HEADS_PER_STEP, hv)
    bc, gc, br, gr = _prep(cfg, tail, bias_full, amul_full, kind="gdn", chunk=GDN_CHUNK,
                           n_heads=hv, hps=hps)
    o_f = _gdn_scan(cfg, qkv, bc, gc, br, gr, rev=False)
    y = _gdn_scan(cfg, qkv, bc, gc, br, gr, rev=True, prev=o_f, qkvz=qkvz,
                  norm_w=norm_w.reshape(1, GDN_HEAD).astype(F32))
    return _matmul(y, w_out.astype(BF16), 0, d, F32, tm, _pick(d, (512, 256, 128)), "gdn_out_proj")


def _ffn(cfg, h, w_in, w_out):
    d = cfg["d"]
    d_ff = w_in.shape[1] // 2
    hid = _matmul_swiglu(h, w_in.astype(BF16), cfg["tm"], _pick(d_ff, (256, 128)))
    return _matmul(hid, w_out.astype(BF16), 0, d, F32, cfg["tm_out"], _pick(d, (512, 256, 128)),
                   "ffn_out")


def kernel(x, c, ctx, c_ctx, ada_down, ada_up, ada_bias, norm_mix, norm_ffn, ffn_in, ffn_out,
           ssd_in, ssd_conv_w, ssd_conv_b, ssd_a_log, ssd_dt_bias, ssd_d, ssd_norm, ssd_out,
           gdn_in, gdn_conv_w, gdn_a_log, gdn_dt_bias, gdn_norm, gdn_out, final_norm):
    batch, seq, d = x.shape
    ctx_len = ctx.shape[1]
    depth = ada_down.shape[0]
    rows = seq // GRID_W
    assert rows == ROW_TILE and ctx_len % CONV_ROWS == 0 and seq % CONV_ROWS == 0
    t_lat, t_ctx = batch * seq, batch * ctx_len
    t = t_lat + t_ctx
    cfg = dict(d=d, batch=batch, seq=seq, ctx_len=ctx_len, cols=GRID_W, t_lat=t_lat, t_ctx=t_ctx, t=t,
               tm=_pick(t, (1536, 1024, 512, 256, 128)), tm_out=_pick(t, (512, 256, 128)))
    assert t_ctx % GRID_W == 0

    n_streams = batch + 1
    cv = jnp.concatenate([c.astype(F32), c_ctx.astype(F32)[None, :],
                          jnp.zeros((8 - n_streams % 8 if n_streams % 8 else 0, d), F32)], axis=0)
    mods = _ada_modulation(cv, ada_down, ada_up, ada_bias)

    def mod(layer, which):
        return mods[layer, :n_streams, which * d:(which + 1) * d].reshape(n_streams, 1, d)

    x_lat = x.reshape(t_lat, d)
    x_ctx = ctx.reshape(t_ctx, d)
    pending, pending_gate = None, None
    col_major = False
    for i in range(depth):
        want_col_major = (i // N_MIXERS) % 2 == 1
        assert want_col_major or not col_major
        j = i // N_MIXERS
        x_lat, x_ctx, h = _norm_modulate(cfg, x_lat, x_ctx, pending, pending_gate,
                                         norm_mix[i].reshape(1, d), mod(i, 0), mod(i, 1),
                                         to_col_major=want_col_major and not col_major)
        col_major = want_col_major
        if i % N_MIXERS == 0:
            y = _ssd_mixer(cfg, h, ssd_in[j], ssd_conv_w[j], ssd_conv_b[j], ssd_a_log[j],
                           ssd_dt_bias[j], ssd_d[j], ssd_norm[j], ssd_out[j])
        else:
            y = _gdn_mixer(cfg, h, gdn_in[j], gdn_conv_w[j], gdn_a_log[j], gdn_dt_bias[j],
                           gdn_norm[j], gdn_out[j])
        x_lat, x_ctx, h = _norm_modulate(cfg, x_lat, x_ctx, y, mod(i, 2),
                                         norm_ffn[i].reshape(1, d), mod(i, 3), mod(i, 4),
                                         to_col_major=False)
        pending, pending_gate = _ffn(cfg, h, ffn_in[i], ffn_out[i]), mod(i, 5)
    out = _final_norm(cfg, x_lat, pending, pending_gate, final_norm.reshape(1, d),
                      from_col_major=col_major)
    return out.reshape(batch, seq, d)
```

```python
import functools

import jax
import jax.numpy as jnp
from jax import lax
from jax.experimental import pallas as pl
from jax.experimental.pallas import tpu as pltpu

F32 = jnp.float32
BF16 = jnp.bfloat16
EPS = 1e-6

GRID_W = 64
N_MIXERS = 2
N_MOD = 6
SSD_HEAD_DIM = 64
SSD_GROUPS = 8
SSD_STATE = 128
SSD_CHUNK = 128
GDN_HEAD = 128
GDN_CHUNK = 64
GDN_HEADS_PER_STEP = 8

ROW_TILE = 128
CONV_ROWS = 256
HALO = 8
V7X_VMEM_LIMIT = 56 * 1024 * 1024


def _cparams(sem, vmem=None):
    return pltpu.CompilerParams(dimension_semantics=sem, vmem_limit_bytes=vmem or V7X_VMEM_LIMIT)


def _sigmoid(x):
    return 1.0 / (1.0 + jnp.exp(-x))


def _silu(x):
    return x * _sigmoid(x)


def _softplus(x):
    return jnp.maximum(x, 0.0) + jnp.log1p(jnp.exp(-jnp.abs(x)))


def _pick(n, candidates):
    for c in candidates:
        if n % c == 0:
            return c
    raise ValueError(f"no tile for {n} among {candidates}")


def _split3(a):
    hi = a.astype(BF16).astype(F32)
    r1 = a - hi
    mid = r1.astype(BF16).astype(F32)
    lo = (r1 - mid).astype(BF16).astype(F32)
    return hi, mid, lo


def _dot(a, b):
    return jnp.dot(a, b, preferred_element_type=F32)


def _dot_exact_rhs(m, a):
    hi, mid, lo = _split3(a)
    mb = m.astype(BF16)
    return _dot(mb, hi.astype(BF16)) + _dot(mb, mid.astype(BF16)) + _dot(mb, lo.astype(BF16))


def _dot_exact_lhs(a, m):
    hi, mid, lo = _split3(a)
    mb = m.astype(BF16)
    return _dot(hi.astype(BF16), mb) + _dot(mid.astype(BF16), mb) + _dot(lo.astype(BF16), mb)


def _ada_kernel(cv_ref, down_ref, up_ref, bias_ref, o_ref):
    t = jnp.dot(_silu(cv_ref[...]), down_ref[...], preferred_element_type=F32,
                precision=lax.Precision.HIGHEST)
    o_ref[...] = jnp.dot(t, up_ref[...], preferred_element_type=F32,
                         precision=lax.Precision.HIGHEST) + bias_ref[...]


def _ada_modulation(cv, down, up, bias):
    depth, d, r = down.shape
    n = up.shape[-1]
    tn = _pick(n, (6144, 3072, 1536, 768, 512, 256, 128))
    rows = cv.shape[0]
    return pl.pallas_call(
        _ada_kernel,
        grid=(depth, n // tn),
        in_specs=[
            pl.BlockSpec((rows, d), lambda l, j: (0, 0)),
            pl.BlockSpec((None, d, r), lambda l, j: (l, 0, 0)),
            pl.BlockSpec((None, r, tn), lambda l, j: (l, 0, j)),
            pl.BlockSpec((None, 1, tn), lambda l, j: (l, 0, j)),
        ],
        out_specs=pl.BlockSpec((None, rows, tn), lambda l, j: (l, 0, j)),
        out_shape=jax.ShapeDtypeStruct((depth, rows, n), F32),
        compiler_params=_cparams(("arbitrary", "arbitrary")),
        name="ada_modulation",
    )(cv, down, up, bias.reshape(depth, 1, n))


def _norm_kernel(*refs, has_y, n_lat_tiles):
    if has_y:
        (xl_ref, xc_ref, yl_ref, yc_ref, g_ref, w_ref, sh_ref, sc_ref,
         xlo_ref, xco_ref, h_ref) = refs
    else:
        xl_ref, xc_ref, w_ref, sh_ref, sc_ref, h_ref = refs

    def run(x_ref, y_ref, xo_ref):
        x = x_ref[...]
        if has_y:
            x = x + g_ref[...] * y_ref[...]
            xo_ref[...] = x
        xn = x * lax.rsqrt(jnp.mean(x * x, axis=-1, keepdims=True) + EPS) * w_ref[...]
        h_ref[...] = (xn * (1.0 + sc_ref[...]) + sh_ref[...]).astype(h_ref.dtype)

    i = pl.program_id(0)

    @pl.when(i < n_lat_tiles)
    def _():
        run(xl_ref, yl_ref if has_y else None, xlo_ref if has_y else None)

    @pl.when(i >= n_lat_tiles)
    def _():
        run(xc_ref, yc_ref if has_y else None, xco_ref if has_y else None)


def _norm_modulate(cfg, x_lat, x_ctx, y, gate, w, shift, scale, *, to_col_major):
    d, tr = cfg["d"], ROW_TILE
    n_lat, n_ctx = cfg["t_lat"] // tr, cfg["t_ctx"] // tr
    per_batch = cfg["seq"] // tr
    cols = cfg["cols"]
    has_y = y is not None

    def lat_tile(i):
        return jnp.minimum(i, n_lat - 1)

    def ctx_tile(i):
        return jnp.maximum(i - n_lat, 0)

    def stream(i):
        return jnp.where(i < n_lat, i // per_batch, cfg["batch"])

    if to_col_major:
        def strided(i):
            il = lat_tile(i)
            return (il // cols, il % cols)
        xl_in = x_lat.reshape(cfg["t_lat"] // cols, cols * d)
        xl_spec = pl.BlockSpec((tr, d), strided)
    else:
        xl_in = x_lat
        xl_spec = pl.BlockSpec((tr, d), lambda i: (lat_tile(i), 0))
    xc_spec = pl.BlockSpec((tr, d), lambda i: (ctx_tile(i), 0))
    mod_spec = pl.BlockSpec((None, 1, d), lambda i: (stream(i), 0, 0))
    w_spec = pl.BlockSpec((1, d), lambda i: (0, 0))
    h_spec = pl.BlockSpec((tr, d), lambda i: (i, 0))
    h_shape = jax.ShapeDtypeStruct((cfg["t"], d), BF16)

    if has_y:
        if to_col_major:
            yl_in = y.reshape(cfg["t"] // cols, cols * d)
            yl_spec = pl.BlockSpec((tr, d), strided)
        else:
            yl_in = y
            yl_spec = pl.BlockSpec((tr, d), lambda i: (lat_tile(i), 0))
        yc_spec = pl.BlockSpec((tr, d), lambda i: (n_lat + ctx_tile(i), 0))
        ins = (xl_in, x_ctx, yl_in, y, gate, w, shift, scale)
        in_specs = [xl_spec, xc_spec, yl_spec, yc_spec, mod_spec, w_spec, mod_spec, mod_spec]
        out_specs = [pl.BlockSpec((tr, d), lambda i: (lat_tile(i), 0)), xc_spec, h_spec]
        out_shape = [jax.ShapeDtypeStruct((cfg["t_lat"], d), F32),
                     jax.ShapeDtypeStruct((cfg["t_ctx"], d), F32), h_shape]
    else:
        ins = (xl_in, x_ctx, w, shift, scale)
        in_specs = [xl_spec, xc_spec, w_spec, mod_spec, mod_spec]
        out_specs = h_spec
        out_shape = h_shape
    out = pl.pallas_call(
        functools.partial(_norm_kernel, has_y=has_y, n_lat_tiles=n_lat),
        grid=(n_lat + n_ctx,),
        in_specs=in_specs, out_specs=out_specs, out_shape=out_shape,
        compiler_params=_cparams(("arbitrary",)),
        name="norm_modulate",
    )(*ins)
    if has_y:
        return out
    return x_lat, x_ctx, out


def _final_kernel(x_ref, y_ref, g_ref, w_ref, o_ref):
    x = x_ref[...] + g_ref[...] * y_ref[...]
    o_ref[...] = x * lax.rsqrt(jnp.mean(x * x, axis=-1, keepdims=True) + EPS) * w_ref[...]


def _final_norm(cfg, x_lat, y, gate, w, *, from_col_major):
    d, tr = cfg["d"], ROW_TILE
    n_lat = cfg["t_lat"] // tr
    per_batch = cfg["seq"] // tr
    cols = cfg["cols"]
    plain = pl.BlockSpec((tr, d), lambda i: (i, 0))
    if from_col_major:
        out_spec = pl.BlockSpec((tr, d), lambda i: (i // cols, i % cols))
        out_shape = jax.ShapeDtypeStruct((cfg["t_lat"] // cols, cols * d), F32)
    else:
        out_spec, out_shape = plain, jax.ShapeDtypeStruct((cfg["t_lat"], d), F32)
    out = pl.pallas_call(
        _final_kernel,
        grid=(n_lat,),
        in_specs=[plain, plain,
                  pl.BlockSpec((None, 1, d), lambda i: (i // per_batch, 0, 0)),
                  pl.BlockSpec((1, d), lambda i: (0, 0))],
        out_specs=out_spec, out_shape=out_shape,
        compiler_params=_cparams(("arbitrary",)),
        name="final_norm",
    )(x_lat, y, gate, w)
    return out.reshape(cfg["t_lat"], d)


def _mm_kernel(a_ref, w_ref, o_ref):
    o_ref[...] = _dot(a_ref[...], w_ref[...]).astype(o_ref.dtype)


def _matmul(a, w, col_lo, col_hi, out_dtype, tm, tn, name):
    m, k = a.shape
    n = col_hi - col_lo
    assert m % tm == 0 and n % tn == 0 and col_lo % tn == 0
    off = col_lo // tn
    return pl.pallas_call(
        _mm_kernel,
        grid=(m // tm, n // tn),
        in_specs=[pl.BlockSpec((tm, k), lambda i, j: (i, 0)),
                  pl.BlockSpec((k, tn), lambda i, j: (0, off + j))],
        out_specs=pl.BlockSpec((tm, tn), lambda i, j: (i, j)),
        out_shape=jax.ShapeDtypeStruct((m, n), out_dtype),
        compiler_params=_cparams(("arbitrary", "arbitrary")),
        name=name,
    )(a, w)


def _swiglu_kernel(a_ref, wg_ref, wu_ref, o_ref):
    a = a_ref[...]
    g = _dot(a, wg_ref[...])
    u = _dot(a, wu_ref[...])
    o_ref[...] = (_silu(g) * u).astype(o_ref.dtype)


def _matmul_swiglu(a, w, tm, tn):
    m, k = a.shape
    d_ff = w.shape[1] // 2
    assert m % tm == 0 and d_ff % tn == 0
    nj = d_ff // tn
    return pl.pallas_call(
        _swiglu_kernel,
        grid=(m // tm, nj),
        in_specs=[pl.BlockSpec((tm, k), lambda i, j: (i, 0)),
                  pl.BlockSpec((k, tn), lambda i, j: (0, j)),
                  pl.BlockSpec((k, tn), lambda i, j: (0, nj + j))],
        out_specs=pl.BlockSpec((tm, tn), lambda i, j: (i, j)),
        out_shape=jax.ShapeDtypeStruct((m, d_ff), BF16),
        compiler_params=_cparams(("arbitrary", "arbitrary")),
        name="ffn_in_swiglu",
    )(a, w, w)


def _conv_kernel(prev_ref, cur_ref, next_ref, w_ref, b_ref, o_ref, scr_ref, *,
                 tiles_per_lat_seq, n_lat_tiles, tiles_per_ctx_seq, n_norm_tiles, n_q_tiles, q_scale):
    i, j = pl.program_id(0), pl.program_id(1)
    ts = cur_ref.shape[0]
    ksz = w_ref.shape[0]
    is_lat = i < n_lat_tiles
    pos = jnp.where(is_lat, i % tiles_per_lat_seq, (i - n_lat_tiles) % tiles_per_ctx_seq)
    per_seq = jnp.where(is_lat, tiles_per_lat_seq, tiles_per_ctx_seq)
    keep_prev = jnp.where(pos == 0, 0.0, 1.0)
    keep_next = jnp.where(pos == per_seq - 1, 0.0, 1.0)
    scr_ref[0:HALO, :] = prev_ref[...].astype(F32) * keep_prev
    scr_ref[HALO:HALO + ts, :] = cur_ref[...].astype(F32)
    scr_ref[HALO + ts:2 * HALO + ts, :] = next_ref[...].astype(F32) * keep_next
    w = w_ref[...]
    acc = b_ref[...] + w[0:1, :] * scr_ref[pl.ds(HALO - ksz // 2, ts), :]
    for k in range(1, ksz):
        acc = acc + w[k:k + 1, :] * scr_ref[pl.ds(HALO - ksz // 2 + k, ts), :]
    y = _silu(acc)

    if n_norm_tiles == 0:
        o_ref[...] = y.astype(o_ref.dtype)
    else:
        @pl.when(j >= n_norm_tiles)
        def _():
            o_ref[...] = y.astype(o_ref.dtype)

        @pl.when(j < n_norm_tiles)
        def _():
            scale = jnp.where(j < n_q_tiles, q_scale, 1.0)
            for h in range(y.shape[1] // GDN_HEAD):
                yh = y[:, h * GDN_HEAD:(h + 1) * GDN_HEAD]
                r = lax.rsqrt(jnp.sum(yh * yh, axis=-1, keepdims=True) + EPS) * scale
                o_ref[:, h * GDN_HEAD:(h + 1) * GDN_HEAD] = (yh * r).astype(o_ref.dtype)


def _conv_silu(cfg, src, col_lo, n_ch, w, b, *, norm_ch=0, q_ch=0, q_scale=1.0):
    t, ts = cfg["t"], CONV_ROWS
    tc = next(c for c in (512, 256, 128)
              if all(v % c == 0 for v in (n_ch, col_lo, norm_ch, q_ch)))
    assert cfg["seq"] % ts == 0 and cfg["ctx_len"] % ts == 0
    off = col_lo // tc
    hb = ts // HALO
    last_hb = t // HALO - 1
    kern = functools.partial(
        _conv_kernel, tiles_per_lat_seq=cfg["seq"] // ts, n_lat_tiles=cfg["t_lat"] // ts,
        tiles_per_ctx_seq=cfg["ctx_len"] // ts, n_norm_tiles=norm_ch // tc, n_q_tiles=q_ch // tc,
        q_scale=q_scale)
    return pl.pallas_call(
        kern,
        grid=(t // ts, n_ch // tc),
        in_specs=[
            pl.BlockSpec((HALO, tc), lambda i, j: (jnp.maximum(i * hb - 1, 0), off + j)),
            pl.BlockSpec((ts, tc), lambda i, j: (i, off + j)),
            pl.BlockSpec((HALO, tc), lambda i, j: (jnp.minimum((i + 1) * hb, last_hb), off + j)),
            pl.BlockSpec((w.shape[0], tc), lambda i, j: (0, j)),
            pl.BlockSpec((1, tc), lambda i, j: (0, j)),
        ],
        out_specs=pl.BlockSpec((ts, tc), lambda i, j: (i, j)),
        out_shape=jax.ShapeDtypeStruct((t, n_ch), BF16),
        scratch_shapes=[pltpu.VMEM((ts + 2 * HALO, tc), F32)],
        compiler_params=_cparams(("arbitrary", "arbitrary")),
        name="conv_silu",
    )(src, src, src, w, b.reshape(1, n_ch))


def _prep_kernel(tail_ref, bias_ref, amul_ref, p1c_ref, p2c_ref, p1r_ref, p2r_ref, *,
                 kind, chunk, p2_off, split_lane, hps):
    tr = tail_ref.shape[0]
    raw = tail_ref[...]
    sp = _softplus(raw + bias_ref[...])
    if kind == "ssd":
        p1 = sp
    else:
        p1 = _sigmoid(raw)
    a = sp * amul_ref[...]
    row = lax.broadcasted_iota(jnp.int32, (tr, tr), 0)
    col = lax.broadcasted_iota(jnp.int32, (tr, tr), 1)
    same = (row // chunk) == (col // chunk)
    lower = jnp.where(same & (col <= row), 1.0, 0.0)
    upper = jnp.where(same & (col >= row), 1.0, 0.0)
    lane = lax.broadcasted_iota(jnp.int32, raw.shape, 1)
    cum = jnp.where(lane < split_lane, _dot_exact_rhs(lower, a), _dot_exact_rhs(upper, a))
    p1t = p1.T
    cumt = cum.T
    for n in range(p1c_ref.shape[0]):
        p1c_ref[n] = p1[:, n * hps:(n + 1) * hps]
        p2c_ref[n] = cum[:, p2_off + n * hps:p2_off + (n + 1) * hps]
        p1r_ref[n] = p1t[n * hps:(n + 1) * hps, :]
        p2r_ref[n] = cumt[p2_off + n * hps:p2_off + (n + 1) * hps, :]


def _prep(cfg, tail, bias_full, amul_full, *, kind, chunk, n_heads, hps):
    t, tr = cfg["t"], ROW_TILE
    nblk = 2 * n_heads // hps
    p2_off = 0 if kind == "ssd" else 2 * n_heads
    split_lane = p2_off + n_heads
    kern = functools.partial(_prep_kernel, kind=kind, chunk=chunk, p2_off=p2_off,
                             split_lane=split_lane, hps=hps)
    col_spec = pl.BlockSpec((nblk, tr, hps), lambda i: (0, i, 0))
    row_spec = pl.BlockSpec((nblk, hps, tr), lambda i: (0, 0, i))
    col_shape = jax.ShapeDtypeStruct((nblk, t, hps), F32)
    row_shape = jax.ShapeDtypeStruct((nblk, hps, t), F32)
    return pl.pallas_call(
        kern,
        grid=(t // tr,),
        in_specs=[pl.BlockSpec((tr, 128), lambda i: (i, 0)),
                  pl.BlockSpec((1, 128), lambda i: (0, 0)),
                  pl.BlockSpec((1, 128), lambda i: (0, 0))],
        out_specs=[col_spec, col_spec, row_spec, row_spec],
        out_shape=[col_shape, col_shape, row_shape, row_shape],
        compiler_params=_cparams(("arbitrary",)),
        name=f"prep_{kind}",
    )(tail, bias_full, amul_full)


def _fwd_block(cfg, b, i):
    ncc, ncl = cfg["ctx_len"] // ROW_TILE, cfg["seq"] // ROW_TILE
    ctx = cfg["batch"] * ncl + b * ncc + i
    lat = b * ncl + (i - ncc)
    return jnp.where(i < ncc, ctx, lat)


def _bwd_block(cfg, b, i):
    ncc, ncl = cfg["ctx_len"] // ROW_TILE, cfg["seq"] // ROW_TILE
    ctx = cfg["batch"] * ncl + b * ncc + (ncc - 1 - i)
    lat = b * ncl + (ncl - 1 - (i - ncc))
    return jnp.where(i < ncc, ctx, lat)


def _ssd_kernel(x_ref, b_ref, c_ref, z_ref, dtc_f_ref, dtc_b_ref, cuc_f_ref, cuc_b_ref,
                dtr_f_ref, dtr_b_ref, cur_f_ref, cur_b_ref, dskip_ref, nw_ref,
                o_ref, sf_ref, sb_ref, store_ref, *, nb, ncc):
    i = pl.program_id(2)
    q = x_ref.shape[0]
    hpg = dtc_f_ref.shape[-1]
    width = x_ref.shape[1]
    hd = width // hpg
    expand = jnp.where(lax.broadcasted_iota(jnp.int32, (hpg, width), 1) // hd
                       == lax.broadcasted_iota(jnp.int32, (hpg, width), 0), 1.0, 0.0)

    def widen(a):
        return _dot_exact_lhs(a, expand)

    x = x_ref[...]
    xf = x.astype(F32)
    bm = b_ref[...]

    @pl.when(i == 0)
    def _():
        sb_ref[...] = jnp.zeros_like(sb_ref)

    @pl.when(i == nb)
    def _():
        sf_ref[...] = jnp.zeros_like(sf_ref)

    @pl.when(i < nb)
    def _():
        cub = cuc_b_ref[...]
        tot = cub[0:1, :]
        wgt = widen(dtc_b_ref[...] * jnp.exp(tot - cub))
        dec = widen(jnp.exp(cub))[0:1, :]
        cs = lax.dot_general(bm, (xf * wgt).astype(BF16), (((0,), (0,)), ((), ())),
                             preferred_element_type=F32)
        store_ref[i] = sb_ref[...].astype(BF16)
        sb_ref[...] = sb_ref[...] * dec + cs

    @pl.when(i >= nb)
    def _():
        i2 = i - nb
        slot = jnp.where(i2 < ncc, ncc - 1 - i2, nb - 1 - (i2 - ncc))
        cm = c_ref[...]
        cuf, cub = cuc_f_ref[...], cuc_b_ref[...]
        totf = cuf[q - 1:q, :]
        ef = widen(jnp.exp(cuf))
        eb = widen(jnp.exp(cub))
        wf = widen(dtc_f_ref[...] * jnp.exp(totf - cuf))
        y = dskip_ref[...] * xf
        y = y + _dot(cm, sf_ref[...].astype(BF16)) * ef
        y = y + _dot(cm, store_ref[slot]) * eb
        cs = lax.dot_general(bm, (xf * wf).astype(BF16), (((0,), (0,)), ((), ())),
                             preferred_element_type=F32)
        sf_ref[...] = sf_ref[...] * ef[q - 1:q, :] + cs

        cb = lax.dot_general(cm, bm, (((1,), (1,)), ((), ())), preferred_element_type=F32)
        li = lax.broadcasted_iota(jnp.int32, (q, q), 0)
        si = lax.broadcasted_iota(jnp.int32, (q, q), 1)
        causal, anti = si <= li, si >= li
        lane = lax.broadcasted_iota(jnp.int32, (q, 2 * hd), 1)
        dtr_f, dtr_b = dtr_f_ref[...], dtr_b_ref[...]
        cur_f, cur_b = cur_f_ref[...], cur_b_ref[...]
        pieces = []
        for p in range(hpg // 2):
            xp = x[:, p * 2 * hd:(p + 1) * 2 * hd]
            acc = None
            for half in range(2):
                r = 2 * p + half
                df = jnp.where(causal, jnp.exp(jnp.where(causal, cuf[:, r:r + 1] - cur_f[r:r + 1, :], 0.0)), 0.0)
                db = jnp.where(anti, jnp.exp(jnp.where(anti, cub[:, r:r + 1] - cur_b[r:r + 1, :], 0.0)), 0.0)
                mix = (cb * (df * dtr_f[r:r + 1, :] + db * dtr_b[r:r + 1, :])).astype(BF16)
                keep = (lane < hd) if half == 0 else (lane >= hd)
                part = _dot(mix, jnp.where(keep, xp, jnp.zeros_like(xp)))
                acc = part if acc is None else acc + part
            pieces.append(acc)
        y = y + jnp.concatenate(pieces, axis=1)
        g = y * _silu(z_ref[...].astype(F32))
        o_ref[...] = (g * lax.rsqrt(jnp.mean(g * g, axis=-1, keepdims=True) + EPS)
                      * nw_ref[...]).astype(o_ref.dtype)


def _ssd_scan(cfg, xbc, zx, dtc, cuc, dtr, cur, d_skip, norm_w):
    t, tr, batch = cfg["t"], ROW_TILE, cfg["batch"]
    g, n = SSD_GROUPS, SSD_STATE
    inner = cfg["d"]
    width = inner // g
    hpg = width // SSD_HEAD_DIM
    assert hpg % 2 == 0 and tr == SSD_CHUNK
    ncc, ncl = cfg["ctx_len"] // tr, cfg["seq"] // tr
    nb = ncc + ncl
    b_off, c_off = inner // n, inner // n + g

    def blk(b, i):
        return jnp.where(i < nb, _bwd_block(cfg, b, i), _fwd_block(cfg, b, i - nb))

    def oblk(b, i):
        return _fwd_block(cfg, b, jnp.maximum(i - nb, 0))

    col = lambda d: pl.BlockSpec((None, tr, hpg), lambda b, gi, i: (d * g + gi, blk(b, i), 0))
    row = lambda d: pl.BlockSpec((None, hpg, tr), lambda b, gi, i: (d * g + gi, 0, blk(b, i)))
    kern = functools.partial(_ssd_kernel, nb=nb, ncc=ncc)
    return pl.pallas_call(
        kern,
        grid=(batch, g, 2 * nb),
        in_specs=[
            pl.BlockSpec((tr, width), lambda b, gi, i: (blk(b, i), gi)),
            pl.BlockSpec((tr, n), lambda b, gi, i: (blk(b, i), b_off + gi)),
            pl.BlockSpec((tr, n), lambda b, gi, i: (blk(b, i), c_off + gi)),
            pl.BlockSpec((tr, width), lambda b, gi, i: (oblk(b, i), gi)),
            col(0), col(1), col(0), col(1), row(0), row(1), row(0), row(1),
            pl.BlockSpec((1, width), lambda b, gi, i: (0, gi)),
            pl.BlockSpec((1, width), lambda b, gi, i: (0, gi)),
        ],
        out_specs=pl.BlockSpec((tr, width), lambda b, gi, i: (oblk(b, i), gi)),
        out_shape=jax.ShapeDtypeStruct((t, inner), BF16),
        scratch_shapes=[pltpu.VMEM((n, width), F32), pltpu.VMEM((n, width), F32),
                        pltpu.VMEM((nb, n, width), BF16)],
        compiler_params=_cparams(("arbitrary", "arbitrary", "arbitrary")),
        name="ssd_scan",
    )(xbc, xbc, xbc, zx, dtc, dtc, cuc, cuc, dtr, dtr, cur, cur, d_skip, norm_w)


def _gdn_kernel(*refs, rev, finish):
    if finish:
        (q_ref, k_ref, v_ref, bc_ref, gc_ref, br_ref, gr_ref, prev_ref, z_ref, nw_ref,
         o_ref, s_ref) = refs
    else:
        q_ref, k_ref, v_ref, bc_ref, gc_ref, br_ref, gr_ref, o_ref, s_ref = refs
    c = GDN_CHUNK
    hw = GDN_HEAD
    n_heads = s_ref.shape[0]
    rep = n_heads // (k_ref.shape[1] // hw)

    @pl.when(pl.program_id(2) == 0)
    def _():
        s_ref[...] = jnp.zeros_like(s_ref)

    ii = lax.broadcasted_iota(jnp.int32, (c, c), 0)
    jj = lax.broadcasted_iota(jnp.int32, (c, c), 1)
    incl = (jj >= ii) if rev else (jj <= ii)
    strict = (jj > ii) if rev else (jj < ii)
    eye = jnp.where(ii == jj, 1.0, 0.0)
    n_chunks = q_ref.shape[0] // c
    order = range(n_chunks - 1, -1, -1) if rev else range(n_chunks)
    last = 0 if rev else c - 1

    heads = range(n_heads)
    chunk_rows = {ci: slice(ci * c, (ci + 1) * c) for ci in order}
    items = [(ci, h) for ci in order for h in heads]

    qk, kk = {}, {}
    for ci in order:
        rows = chunk_rows[ci]
        for kh in range(n_heads // rep):
            k_in = k_ref[rows, kh * hw:(kh + 1) * hw]
            both = lax.dot_general(jnp.concatenate([q_ref[rows, kh * hw:(kh + 1) * hw], k_in], axis=0),
                                   k_in, (((1,), (1,)), ((), ())), preferred_element_type=F32)
            qk[ci, kh], kk[ci, kh] = both[:c], both[c:]

    inv, pw, qkd = {}, {}, {}
    for ci, h in items:
        rows = chunk_rows[ci]
        gc1, gr1 = gc_ref[rows, h:h + 1], gr_ref[h:h + 1, rows]
        dec = jnp.where(incl, jnp.exp(jnp.where(incl, gc1 - gr1, 0.0)), 0.0)
        a = jnp.where(strict, bc_ref[rows, h:h + 1] * kk[ci, h // rep] * dec, 0.0)
        inv[ci, h] = eye - a
        pw[ci, h] = a.astype(BF16)
        qkd[ci, h] = (qk[ci, h // rep] * dec).astype(BF16)
    for it in items:
        pw[it] = _dot(pw[it], pw[it]).astype(BF16)
    steps = c.bit_length() - 1
    for s in range(1, steps):
        if s < steps - 1:
            both = {it: _dot(jnp.concatenate([inv[it].astype(BF16), pw[it]], axis=0), pw[it])
                    for it in items}
            for it in items:
                inv[it] = inv[it] + both[it][:c]
                pw[it] = both[it][c:].astype(BF16)
        else:
            upd = {it: _dot(inv[it].astype(BF16), pw[it]) for it in items}
            for it in items:
                inv[it] = inv[it] + upd[it]

    uw = {}
    for ci, h in items:
        rows = chunk_rows[ci]
        bc1, gc1 = bc_ref[rows, h:h + 1], gc_ref[rows, h:h + 1]
        kh = h // rep
        rhs = jnp.concatenate(
            [(v_ref[rows, h * hw:(h + 1) * hw].astype(F32) * bc1).astype(BF16),
             (k_ref[rows, kh * hw:(kh + 1) * hw].astype(F32) * (bc1 * jnp.exp(gc1))).astype(BF16)], axis=1)
        uw[ci, h] = _dot(inv[ci, h].astype(BF16), rhs)

    for ci in order:
        rows = chunk_rows[ci]
        st = {h: s_ref[h] for h in heads}
        ws_qs = {h: _dot(jnp.concatenate([uw[ci, h][:, hw:].astype(BF16),
                                          q_ref[rows, (h // rep) * hw:(h // rep + 1) * hw]], axis=0),
                         st[h].astype(BF16)) for h in heads}
        v_new = {h: (uw[ci, h][:, :hw] - ws_qs[h][:c]).astype(BF16) for h in heads}
        for h in heads:
            gc1 = gc_ref[rows, h:h + 1]
            gtot = gc1[last:last + 1, :]
            kh = h // rep
            k_dec = (k_ref[rows, kh * hw:(kh + 1) * hw].astype(F32) * jnp.exp(gtot - gc1)).astype(BF16)
            s_ref[h] = st[h] * jnp.exp(gtot) + lax.dot_general(
                k_dec, v_new[h], (((0,), (0,)), ((), ())), preferred_element_type=F32)
        for h in heads:
            out = jnp.exp(gc_ref[rows, h:h + 1]) * ws_qs[h][c:] + _dot(qkd[ci, h], v_new[h])
            cols = slice(h * hw, (h + 1) * hw)
            if finish:
                o = out + prev_ref[rows, cols]
                o = o * lax.rsqrt(jnp.mean(o * o, axis=-1, keepdims=True) + EPS) * nw_ref[...]
                o_ref[rows, cols] = (o * _silu(z_ref[rows, cols].astype(F32))).astype(o_ref.dtype)
            else:
                o_ref[rows, cols] = out


def _gdn_scan(cfg, qkv, bc, gc, br, gr, *, rev, prev=None, qkvz=None, norm_w=None):
    t, tr, batch = cfg["t"], ROW_TILE, cfg["batch"]
    hw = GDN_HEAD
    hv = cfg["d"] // hw
    hk = hv // 2
    hps = min(GDN_HEADS_PER_STEP, hv)
    kps = hps // 2
    nhb = hv // hps
    nb = (cfg["ctx_len"] + cfg["seq"]) // tr
    finish = prev is not None
    d = 1 if rev else 0
    order = _bwd_block if rev else _fwd_block

    def blk(b, i):
        return order(cfg, b, i)

    k_off = hk // kps
    v_off = 2 * hk // hps
    z_off = (2 * hk + hv) // hps
    col = pl.BlockSpec((None, tr, hps), lambda b, hb, i: (d * nhb + hb, blk(b, i), 0))
    row = pl.BlockSpec((None, hps, tr), lambda b, hb, i: (d * nhb + hb, 0, blk(b, i)))
    wide = lambda off: pl.BlockSpec((tr, hps * hw), lambda b, hb, i: (blk(b, i), off + hb))
    in_specs = [
        pl.BlockSpec((tr, kps * hw), lambda b, hb, i: (blk(b, i), hb)),
        pl.BlockSpec((tr, kps * hw), lambda b, hb, i: (blk(b, i), k_off + hb)),
        wide(v_off), col, col, row, row,
    ]
    ins = [qkv, qkv, qkv, bc, gc, br, gr]
    if finish:
        in_specs += [wide(0), wide(z_off), pl.BlockSpec((1, hw), lambda b, hb, i: (0, 0))]
        ins += [prev, qkvz, norm_w]
    return pl.pallas_call(
        functools.partial(_gdn_kernel, rev=rev, finish=finish),
        grid=(batch, nhb, nb),
        in_specs=in_specs,
        out_specs=wide(0),
        out_shape=jax.ShapeDtypeStruct((t, hv * hw), BF16 if finish else F32),
        scratch_shapes=[pltpu.VMEM((hps, hw, hw), F32)],
        compiler_params=_cparams(("arbitrary", "arbitrary", "arbitrary")),
        name="gdn_scan_bwd" if rev else "gdn_scan_fwd",
    )(*ins)


def _pad_lanes(v, width=128):
    v = v.reshape(1, -1).astype(F32)
    return jnp.pad(v, ((0, 0), (0, width - v.shape[1])))


def _pad_cols(w, width=128):
    return jnp.pad(w, ((0, 0), (0, width - w.shape[1])))


def _ssd_mixer(cfg, h, w_in, conv_w, conv_b, a_log, dt_bias, d_skip, norm_w, w_out):
    d = cfg["d"]
    heads = d // SSD_HEAD_DIM
    bc_dim = SSD_GROUPS * SSD_STATE
    main = 2 * d + 2 * bc_dim
    w_main = w_in[:, :main].astype(BF16)
    w_tail = _pad_cols(w_in[:, main:]).astype(BF16)
    tm = cfg["tm"]
    zx = _matmul(h, w_main, 0, main, BF16, tm, _pick(main, (512, 256, 128)), "ssd_in_proj")
    tail = _matmul(h, w_tail, 0, 128, F32, tm, 128, "ssd_in_proj_dt")
    xbc = _conv_silu(cfg, zx, d, d + 2 * bc_dim, conv_w, conv_b)
    a_neg = -jnp.exp(a_log.astype(F32))
    dtc, cuc, dtr, cur = _prep(cfg, tail, _pad_lanes(dt_bias), _pad_lanes(a_neg),
                               kind="ssd", chunk=SSD_CHUNK, n_heads=heads, hps=heads // SSD_GROUPS)
    y = _ssd_scan(cfg, xbc, zx, dtc, cuc, dtr, cur,
                  jnp.repeat(d_skip.astype(F32), SSD_HEAD_DIM).reshape(1, d), norm_w.reshape(1, d))
    return _matmul(y, w_out.astype(BF16), 0, d, F32, tm, _pick(d, (512, 256, 128)), "ssd_out_proj")


def _gdn_mixer(cfg, h, w_in, conv_w, a_log, dt_bias, norm_w, w_out):
    d = cfg["d"]
    hv = d // GDN_HEAD
    key_dim = (hv // 2) * GDN_HEAD
    conv_dim = 2 * key_dim + d
    main = conv_dim + d
    w_main = w_in[:, :main].astype(BF16)
    w_tail = _pad_cols(w_in[:, main:]).astype(BF16)
    tm = cfg["tm"]
    qkvz = _matmul(h, w_main, 0, main, BF16, tm, _pick(main, (512, 256, 128)), "gdn_in_proj")
    tail = _matmul(h, w_tail, 0, 128, F32, tm, 128, "gdn_in_proj_gates")
    qkv = _conv_silu(cfg, qkvz, 0, conv_dim, conv_w, jnp.zeros((conv_dim,), F32),
                     norm_ch=2 * key_dim, q_ch=key_dim, q_scale=GDN_HEAD ** -0.5)
    a_pos = jnp.exp(a_log.astype(F32)).reshape(-1)
    zeros = jnp.zeros((2 * hv,), F32)
    bias_full = _pad_lanes(jnp.concatenate([zeros, dt_bias.astype(F32).reshape(-1)]))
    amul_full = _pad_lanes(jnp.concatenate([zeros, -a_pos]))
    hps = min(GDN_HEADS_PER_STEP, hv)
    bc, gc, br, gr = _prep(cfg, tail, bias_full, amul_full, kind="gdn", chunk=GDN_CHUNK,
                           n_heads=hv, hps=hps)
    o_f = _gdn_scan(cfg, qkv, bc, gc, br, gr, rev=False)
    y = _gdn_scan(cfg, qkv, bc, gc, br, gr, rev=True, prev=o_f, qkvz=qkvz,
                  norm_w=norm_w.reshape(1, GDN_HEAD).astype(F32))
    return _matmul(y, w_out.astype(BF16), 0, d, F32, tm, _pick(d, (512, 256, 128)), "gdn_out_proj")


def _ffn(cfg, h, w_in, w_out):
    d = cfg["d"]
    d_ff = w_in.shape[1] // 2
    hid = _matmul_swiglu(h, w_in.astype(BF16), cfg["tm"], _pick(d_ff, (256, 128)))
    return _matmul(hid, w_out.astype(BF16), 0, d, F32, cfg["tm_out"], _pick(d, (512, 256, 128)),
                   "ffn_out")


def kernel(x, c, ctx, c_ctx, ada_down, ada_up, ada_bias, norm_mix, norm_ffn, ffn_in, ffn_out,
           ssd_in, ssd_conv_w, ssd_conv_b, ssd_a_log, ssd_dt_bias, ssd_d, ssd_norm, ssd_out,
           gdn_in, gdn_conv_w, gdn_a_log, gdn_dt_bias, gdn_norm, gdn_out, final_norm):
    batch, seq, d = x.shape
    ctx_len = ctx.shape[1]
    depth = ada_down.shape[0]
    rows = seq // GRID_W
    assert rows == ROW_TILE and ctx_len % CONV_ROWS == 0 and seq % CONV_ROWS == 0
    t_lat, t_ctx = batch * seq, batch * ctx_len
    t = t_lat + t_ctx
    cfg = dict(d=d, batch=batch, seq=seq, ctx_len=ctx_len, cols=GRID_W, t_lat=t_lat, t_ctx=t_ctx, t=t,
               tm=_pick(t, (1536, 1024, 512, 256, 128)), tm_out=_pick(t, (512, 256, 128)))
    assert t_ctx % GRID_W == 0

    n_streams = batch + 1
    cv = jnp.concatenate([c.astype(F32), c_ctx.astype(F32)[None, :],
                          jnp.zeros((8 - n_streams % 8 if n_streams % 8 else 0, d), F32)], axis=0)
    mods = _ada_modulation(cv, ada_down, ada_up, ada_bias)

    def mod(layer, which):
        return mods[layer, :n_streams, which * d:(which + 1) * d].reshape(n_streams, 1, d)

    x_lat = x.reshape(t_lat, d)
    x_ctx = ctx.reshape(t_ctx, d)
    pending, pending_gate = None, None
    col_major = False
    for i in range(depth):
        want_col_major = (i // N_MIXERS) % 2 == 1
        assert want_col_major or not col_major
        j = i // N_MIXERS
        x_lat, x_ctx, h = _norm_modulate(cfg, x_lat, x_ctx, pending, pending_gate,
                                         norm_mix[i].reshape(1, d), mod(i, 0), mod(i, 1),
                                         to_col_major=want_col_major and not col_major)
        col_major = want_col_major
        if i % N_MIXERS == 0:
            y = _ssd_mixer(cfg, h, ssd_in[j], ssd_conv_w[j], ssd_conv_b[j], ssd_a_log[j],
                           ssd_dt_bias[j], ssd_d[j], ssd_norm[j], ssd_out[j])
        else:
            y = _gdn_mixer(cfg, h, gdn_in[j], gdn_conv_w[j], gdn_a_log[j], gdn_dt_bias[j],
                           gdn_norm[j], gdn_out[j])
        x_lat, x_ctx, h = _norm_modulate(cfg, x_lat, x_ctx, y, mod(i, 2),
                                         norm_ffn[i].reshape(1, d), mod(i, 3), mod(i, 4),
                                         to_col_major=False)
        pending, pending_gate = _ffn(cfg, h, ffn_in[i], ffn_out[i]), mod(i, 5)
    out = _final_norm(cfg, x_lat, pending, pending_gate, final_norm.reshape(1, d),
                      from_col_major=col_major)
    return out.reshape(batch, seq, d)
```

```python
import functools

import jax
import jax.numpy as jnp
from jax import lax
from jax.experimental import pallas as pl
from jax.experimental.pallas import tpu as pltpu

F32 = jnp.float32
BF16 = jnp.bfloat16
EPS = 1e-6

GRID_W = 64
N_MIXERS = 2
N_MOD = 6
SSD_HEAD_DIM = 64
SSD_GROUPS = 8
SSD_STATE = 128
SSD_CHUNK = 128
GDN_HEAD = 128
GDN_CHUNK = 64
GDN_HEADS_PER_STEP = 8

ROW_TILE = 128
CONV_ROWS = 256
HALO = 8
LANES = 128
CONV_PITCH = 36
V7X_VMEM_LIMIT = 56 * 1024 * 1024


def _cparams(sem, vmem=None):
    return pltpu.CompilerParams(dimension_semantics=sem, vmem_limit_bytes=vmem or V7X_VMEM_LIMIT)


def _sigmoid(x):
    return 1.0 / (1.0 + jnp.exp(-x))


def _silu(x):
    return x * _sigmoid(x)


def _softplus(x):
    return jnp.maximum(x, 0.0) + jnp.log1p(jnp.exp(-jnp.abs(x)))


def _pick(n, candidates):
    for c in candidates:
        if n % c == 0:
            return c
    raise ValueError(f"no tile for {n} among {candidates}")


def _split3(a):
    hi = a.astype(BF16).astype(F32)
    r1 = a - hi
    mid = r1.astype(BF16).astype(F32)
    lo = (r1 - mid).astype(BF16).astype(F32)
    return hi, mid, lo


def _dot(a, b):
    return jnp.dot(a, b, preferred_element_type=F32)


def _dot_exact_rhs(m, a):
    hi, mid, lo = _split3(a)
    mb = m.astype(BF16)
    return _dot(mb, hi.astype(BF16)) + _dot(mb, mid.astype(BF16)) + _dot(mb, lo.astype(BF16))


def _ada_kernel(cv_ref, down_ref, up_ref, bias_ref, o_ref):
    t = jnp.dot(_silu(cv_ref[...]), down_ref[...], preferred_element_type=F32,
                precision=lax.Precision.HIGHEST)
    o_ref[...] = jnp.dot(t, up_ref[...], preferred_element_type=F32,
                         precision=lax.Precision.HIGHEST) + bias_ref[...]


def _ada_modulation(cv, down, up, bias):
    depth, d, r = down.shape
    n = up.shape[-1]
    tn = _pick(n, (6144, 3072, 1536, 768, 512, 256, 128))
    rows = cv.shape[0]
    return pl.pallas_call(
        _ada_kernel,
        grid=(depth, n // tn),
        in_specs=[
            pl.BlockSpec((rows, d), lambda l, j: (0, 0)),
            pl.BlockSpec((None, d, r), lambda l, j: (l, 0, 0)),
            pl.BlockSpec((None, r, tn), lambda l, j: (l, 0, j)),
            pl.BlockSpec((None, 1, tn), lambda l, j: (l, 0, j)),
        ],
        out_specs=pl.BlockSpec((None, rows, tn), lambda l, j: (l, 0, j)),
        out_shape=jax.ShapeDtypeStruct((depth, rows, n), F32),
        compiler_params=_cparams(("arbitrary", "arbitrary")),
        name="ada_modulation",
    )(cv, down, up, bias.reshape(depth, 1, n))


def _norm_kernel(*refs, has_y, n_lat_tiles):
    if has_y:
        (xl_ref, xc_ref, yl_ref, yc_ref, g_ref, w_ref, sh_ref, sc_ref,
         xlo_ref, xco_ref, h_ref) = refs
    else:
        xl_ref, xc_ref, w_ref, sh_ref, sc_ref, h_ref = refs

    def run(x_ref, y_ref, xo_ref):
        x = x_ref[...]
        if has_y:
            x = x + g_ref[...] * y_ref[...]
            xo_ref[...] = x
        xn = x * lax.rsqrt(jnp.mean(x * x, axis=-1, keepdims=True) + EPS) * w_ref[...]
        h_ref[...] = (xn * (1.0 + sc_ref[...]) + sh_ref[...]).astype(h_ref.dtype)

    i = pl.program_id(0)

    @pl.when(i < n_lat_tiles)
    def _():
        run(xl_ref, yl_ref if has_y else None, xlo_ref if has_y else None)

    @pl.when(i >= n_lat_tiles)
    def _():
        run(xc_ref, yc_ref if has_y else None, xco_ref if has_y else None)


def _norm_modulate(cfg, x_lat, x_ctx, y, gate, w, shift, scale, *, to_col_major):
    d, tr = cfg["d"], ROW_TILE
    n_lat, n_ctx = cfg["t_lat"] // tr, cfg["t_ctx"] // tr
    per_batch = cfg["seq"] // tr
    cols = cfg["cols"]
    has_y = y is not None

    def lat_tile(i):
        return jnp.minimum(i, n_lat - 1)

    def ctx_tile(i):
        return jnp.maximum(i - n_lat, 0)

    def stream(i):
        return jnp.where(i < n_lat, i // per_batch, cfg["batch"])

    if to_col_major:
        def strided(i):
            il = lat_tile(i)
            return (il // cols, il % cols)
        xl_in = x_lat.reshape(cfg["t_lat"] // cols, cols * d)
        xl_spec = pl.BlockSpec((tr, d), strided)
    else:
        xl_in = x_lat
        xl_spec = pl.BlockSpec((tr, d), lambda i: (lat_tile(i), 0))
    xc_spec = pl.BlockSpec((tr, d), lambda i: (ctx_tile(i), 0))
    mod_spec = pl.BlockSpec((None, 1, d), lambda i: (stream(i), 0, 0))
    w_spec = pl.BlockSpec((1, d), lambda i: (0, 0))
    h_spec = pl.BlockSpec((tr, d), lambda i: (i, 0))
    h_shape = jax.ShapeDtypeStruct((cfg["t"], d), BF16)

    if has_y:
        if to_col_major:
            yl_in = y.reshape(cfg["t"] // cols, cols * d)
            yl_spec = pl.BlockSpec((tr, d), strided)
        else:
            yl_in = y
            yl_spec = pl.BlockSpec((tr, d), lambda i: (lat_tile(i), 0))
        yc_spec = pl.BlockSpec((tr, d), lambda i: (n_lat + ctx_tile(i), 0))
        ins = (xl_in, x_ctx, yl_in, y, gate, w, shift, scale)
        in_specs = [xl_spec, xc_spec, yl_spec, yc_spec, mod_spec, w_spec, mod_spec, mod_spec]
        out_specs = [pl.BlockSpec((tr, d), lambda i: (lat_tile(i), 0)), xc_spec, h_spec]
        out_shape = [jax.ShapeDtypeStruct((cfg["t_lat"], d), F32),
                     jax.ShapeDtypeStruct((cfg["t_ctx"], d), F32), h_shape]
    else:
        ins = (xl_in, x_ctx, w, shift, scale)
        in_specs = [xl_spec, xc_spec, w_spec, mod_spec, mod_spec]
        out_specs = h_spec
        out_shape = h_shape
    out = pl.pallas_call(
        functools.partial(_norm_kernel, has_y=has_y, n_lat_tiles=n_lat),
        grid=(n_lat + n_ctx,),
        in_specs=in_specs, out_specs=out_specs, out_shape=out_shape,
        compiler_params=_cparams(("arbitrary",)),
        name="norm_modulate",
    )(*ins)
    if has_y:
        return out
    return x_lat, x_ctx, out


def _final_kernel(x_ref, y_ref, g_ref, w_ref, o_ref):
    x = x_ref[...] + g_ref[...] * y_ref[...]
    o_ref[...] = x * lax.rsqrt(jnp.mean(x * x, axis=-1, keepdims=True) + EPS) * w_ref[...]


def _final_norm(cfg, x_lat, y, gate, w, *, from_col_major):
    d, tr = cfg["d"], ROW_TILE
    n_lat = cfg["t_lat"] // tr
    per_batch = cfg["seq"] // tr
    cols = cfg["cols"]
    plain = pl.BlockSpec((tr, d), lambda i: (i, 0))
    if from_col_major:
        out_spec = pl.BlockSpec((tr, d), lambda i: (i // cols, i % cols))
        out_shape = jax.ShapeDtypeStruct((cfg["t_lat"] // cols, cols * d), F32)
    else:
        out_spec, out_shape = plain, jax.ShapeDtypeStruct((cfg["t_lat"], d), F32)
    out = pl.pallas_call(
        _final_kernel,
        grid=(n_lat,),
        in_specs=[plain, plain,
                  pl.BlockSpec((None, 1, d), lambda i: (i // per_batch, 0, 0)),
                  pl.BlockSpec((1, d), lambda i: (0, 0))],
        out_specs=out_spec, out_shape=out_shape,
        compiler_params=_cparams(("arbitrary",)),
        name="final_norm",
    )(x_lat, y, gate, w)
    return out.reshape(cfg["t_lat"], d)


def _mm_kernel(a_ref, w_ref, o_ref):
    o_ref[...] = _dot(a_ref[...], w_ref[...]).astype(o_ref.dtype)


def _matmul(a, w, layer, col_lo, col_hi, out_dtype, tm, tn, name):
    m, k = a.shape
    n = col_hi - col_lo
    assert m % tm == 0 and n % tn == 0 and col_lo % tn == 0
    off = col_lo // tn
    return pl.pallas_call(
        _mm_kernel,
        grid=(m // tm, n // tn),
        in_specs=[pl.BlockSpec((tm, k), lambda i, j: (i, 0)),
                  pl.BlockSpec((None, k, tn), lambda i, j: (layer, 0, off + j))],
        out_specs=pl.BlockSpec((tm, tn), lambda i, j: (i, j)),
        out_shape=jax.ShapeDtypeStruct((m, n), out_dtype),
        compiler_params=_cparams(("arbitrary", "arbitrary")),
        name=name,
    )(a, w)


def _swiglu_kernel(a_ref, wg_ref, wu_ref, o_ref):
    a = a_ref[...]
    g = _dot(a, wg_ref[...])
    u = _dot(a, wu_ref[...])
    o_ref[...] = (_silu(g) * u).astype(o_ref.dtype)


def _matmul_swiglu(a, w, layer, tm, tn):
    m, k = a.shape
    d_ff = w.shape[-1] // 2
    assert m % tm == 0 and d_ff % tn == 0
    nj = d_ff // tn
    return pl.pallas_call(
        _swiglu_kernel,
        grid=(m // tm, nj),
        in_specs=[pl.BlockSpec((tm, k), lambda i, j: (i, 0)),
                  pl.BlockSpec((None, k, tn), lambda i, j: (layer, 0, j)),
                  pl.BlockSpec((None, k, tn), lambda i, j: (layer, 0, nj + j))],
        out_specs=pl.BlockSpec((tm, tn), lambda i, j: (i, j)),
        out_shape=jax.ShapeDtypeStruct((m, d_ff), BF16),
        compiler_params=_cparams(("arbitrary", "arbitrary")),
        name="ffn_in_swiglu",
    )(a, w, w)


def _conv_kernel(prev_ref, cur_ref, next_ref, w_ref, b_ref, o_ref, scr_ref, out_ref, *,
                 tiles_per_lat_seq, n_lat_tiles, tiles_per_ctx_seq, n_norm_tiles, n_q_tiles, q_scale):
    i, j = pl.program_id(0), pl.program_id(1)
    ts = cur_ref.shape[0]
    ksz = w_ref.shape[0]
    is_lat = i < n_lat_tiles
    pos = jnp.where(is_lat, i % tiles_per_lat_seq, (i - n_lat_tiles) % tiles_per_ctx_seq)
    per_seq = jnp.where(is_lat, tiles_per_lat_seq, tiles_per_ctx_seq)
    keep_prev = jnp.where(pos == 0, 0.0, 1.0)
    keep_next = jnp.where(pos == per_seq - 1, 0.0, 1.0)
    if n_norm_tiles:
        normed = j < n_norm_tiles
        scale = jnp.where(j < n_q_tiles, q_scale, 1.0)
    first = HALO - ksz // 2
    for sl in range(cur_ref.shape[1] // LANES):
        lanes = slice(sl * LANES, (sl + 1) * LANES)
        scr_ref[sl, 0:HALO, :] = prev_ref[:, lanes].astype(F32) * keep_prev
        scr_ref[sl, HALO:HALO + ts, :] = cur_ref[:, lanes].astype(F32)
        scr_ref[sl, HALO + ts:2 * HALO + ts, :] = next_ref[:, lanes].astype(F32) * keep_next
        scr_ref[sl, 2 * HALO + ts:, :] = jnp.zeros((scr_ref.shape[1] - 2 * HALO - ts, LANES), F32)
        wk = [jnp.broadcast_to(w_ref[k:k + 1, lanes], (8, LANES)) for k in range(ksz)]
        bias = jnp.broadcast_to(b_ref[:, lanes], (8, LANES))
        taps = [scr_ref[sl, pl.ds(first + g, 8, stride=CONV_PITCH), :]
                for g in range(CONV_PITCH + ksz - 1)]
        for g in range(CONV_PITCH):
            acc = bias + wk[0] * taps[g]
            for k in range(1, ksz):
                acc = acc + wk[k] * taps[g + k]
            y = _silu(acc)
            if n_norm_tiles:
                r = lax.rsqrt(jnp.sum(y * y, axis=-1, keepdims=True) + EPS) * scale
                y = y * jnp.where(normed, r, 1.0)
            out_ref[sl, pl.ds(g, 8, stride=CONV_PITCH), :] = y
        o_ref[:, lanes] = out_ref[sl, 0:ts, :].astype(o_ref.dtype)


def _conv_silu(cfg, src, col_lo, n_ch, w, b, *, norm_ch=0, q_ch=0, q_scale=1.0):
    t, ts = cfg["t"], CONV_ROWS
    tc = next(c for c in (512, 256, 128)
              if all(v % c == 0 for v in (n_ch, col_lo, norm_ch, q_ch)))
    assert cfg["seq"] % ts == 0 and cfg["ctx_len"] % ts == 0 and 8 * CONV_PITCH >= ts
    ksz = w.shape[0]
    last_row = (HALO - ksz // 2) + (CONV_PITCH + ksz - 2) + 7 * CONV_PITCH
    in_rows = -(-(last_row + 1) // 8) * 8
    assert in_rows >= ts + 2 * HALO
    off = col_lo // tc
    hb = ts // HALO
    last_hb = t // HALO - 1
    kern = functools.partial(
        _conv_kernel, tiles_per_lat_seq=cfg["seq"] // ts, n_lat_tiles=cfg["t_lat"] // ts,
        tiles_per_ctx_seq=cfg["ctx_len"] // ts, n_norm_tiles=norm_ch // tc, n_q_tiles=q_ch // tc,
        q_scale=q_scale)
    return pl.pallas_call(
        kern,
        grid=(t // ts, n_ch // tc),
        in_specs=[
            pl.BlockSpec((HALO, tc), lambda i, j: (jnp.maximum(i * hb - 1, 0), off + j)),
            pl.BlockSpec((ts, tc), lambda i, j: (i, off + j)),
            pl.BlockSpec((HALO, tc), lambda i, j: (jnp.minimum((i + 1) * hb, last_hb), off + j)),
            pl.BlockSpec((w.shape[0], tc), lambda i, j: (0, j)),
            pl.BlockSpec((1, tc), lambda i, j: (0, j)),
        ],
        out_specs=pl.BlockSpec((ts, tc), lambda i, j: (i, j)),
        out_shape=jax.ShapeDtypeStruct((t, n_ch), BF16),
        scratch_shapes=[pltpu.VMEM((tc // LANES, in_rows, LANES), F32),
                        pltpu.VMEM((tc // LANES, 8 * CONV_PITCH, LANES), F32)],
        compiler_params=_cparams(("arbitrary", "arbitrary")),
        name="conv_silu",
    )(src, src, src, w, b.reshape(1, n_ch))


def _prep_kernel(tail_ref, bias_ref, amul_ref, p1c_ref, p2c_ref, p1r_ref, p2r_ref, *,
                 kind, chunk, p2_off, split_lane, hps):
    tr = tail_ref.shape[0]
    raw = tail_ref[...]
    sp = _softplus(raw + bias_ref[...])
    if kind == "ssd":
        p1 = sp
    else:
        p1 = _sigmoid(raw)
    a = sp * amul_ref[...]
    row = lax.broadcasted_iota(jnp.int32, (tr, tr), 0)
    col = lax.broadcasted_iota(jnp.int32, (tr, tr), 1)
    same = (row // chunk) == (col // chunk)
    lower = jnp.where(same & (col <= row), 1.0, 0.0)
    upper = jnp.where(same & (col >= row), 1.0, 0.0)
    lane = lax.broadcasted_iota(jnp.int32, raw.shape, 1)
    cum = jnp.where(lane < split_lane, _dot_exact_rhs(lower, a), _dot_exact_rhs(upper, a))
    p1t = p1.T
    cumt = cum.T
    for n in range(p1c_ref.shape[0]):
        p1c_ref[n] = p1[:, n * hps:(n + 1) * hps]
        p2c_ref[n] = cum[:, p2_off + n * hps:p2_off + (n + 1) * hps]
        p1r_ref[n] = p1t[n * hps:(n + 1) * hps, :]
        p2r_ref[n] = cumt[p2_off + n * hps:p2_off + (n + 1) * hps, :]


def _prep(cfg, tail, bias_full, amul_full, *, kind, chunk, n_heads, hps):
    t, tr = cfg["t"], ROW_TILE
    nblk = 2 * n_heads // hps
    p2_off = 0 if kind == "ssd" else 2 * n_heads
    split_lane = p2_off + n_heads
    kern = functools.partial(_prep_kernel, kind=kind, chunk=chunk, p2_off=p2_off,
                             split_lane=split_lane, hps=hps)
    col_spec = pl.BlockSpec((nblk, tr, hps), lambda i: (0, i, 0))
    row_spec = pl.BlockSpec((nblk, hps, tr), lambda i: (0, 0, i))
    col_shape = jax.ShapeDtypeStruct((nblk, t, hps), F32)
    row_shape = jax.ShapeDtypeStruct((nblk, hps, t), F32)
    return pl.pallas_call(
        kern,
        grid=(t // tr,),
        in_specs=[pl.BlockSpec((tr, 128), lambda i: (i, 0)),
                  pl.BlockSpec((1, 128), lambda i: (0, 0)),
                  pl.BlockSpec((1, 128), lambda i: (0, 0))],
        out_specs=[col_spec, col_spec, row_spec, row_spec],
        out_shape=[col_shape, col_shape, row_shape, row_shape],
        compiler_params=_cparams(("arbitrary",)),
        name=f"prep_{kind}",
    )(tail, bias_full, amul_full)


def _fwd_block(cfg, b, i):
    ncc, ncl = cfg["ctx_len"] // ROW_TILE, cfg["seq"] // ROW_TILE
    ctx = cfg["batch"] * ncl + b * ncc + i
    lat = b * ncl + (i - ncc)
    return jnp.where(i < ncc, ctx, lat)


def _bwd_block(cfg, b, i):
    ncc, ncl = cfg["ctx_len"] // ROW_TILE, cfg["seq"] // ROW_TILE
    ctx = cfg["batch"] * ncl + b * ncc + (ncc - 1 - i)
    lat = b * ncl + (ncl - 1 - (i - ncc))
    return jnp.where(i < ncc, ctx, lat)


def _ssd_kernel(x_ref, b_ref, c_ref, z_ref, dtc_f_ref, dtc_b_ref, cuc_f_ref, cuc_b_ref,
                dtr_f_ref, dtr_b_ref, cur_f_ref, cur_b_ref, dskip_ref, nw_ref,
                o_ref, sf_ref, sb_ref, store_ref, *, nb, ncc):
    i = pl.program_id(2)
    q = x_ref.shape[0]
    hpg = dtc_f_ref.shape[-1]
    width = x_ref.shape[1]
    hd = width // hpg
    expand = jnp.where(lax.broadcasted_iota(jnp.int32, (hpg, width), 1) // hd
                       == lax.broadcasted_iota(jnp.int32, (hpg, width), 0), 1.0, 0.0)

    def widen(a):
        hi = a.astype(BF16)
        lo = (a - hi.astype(F32)).astype(BF16)
        eb16 = expand.astype(BF16)
        return _dot(hi, eb16) + _dot(lo, eb16)

    x = x_ref[...]
    xf = x.astype(F32)
    bm = b_ref[...]

    @pl.when(i == 0)
    def _():
        sb_ref[...] = jnp.zeros_like(sb_ref)

    @pl.when(i == nb)
    def _():
        sf_ref[...] = jnp.zeros_like(sf_ref)

    @pl.when(i < nb)
    def _():
        cub = cuc_b_ref[...]
        tot = cub[0:1, :]
        wide = widen(jnp.concatenate([dtc_b_ref[...] * jnp.exp(tot - cub), jnp.exp(cub[0:8, :])], axis=0))
        wgt, dec = wide[:q], wide[q:q + 1]
        cs = lax.dot_general(bm, (xf * wgt).astype(BF16), (((0,), (0,)), ((), ())),
                             preferred_element_type=F32)
        store_ref[i] = sb_ref[...].astype(BF16)
        sb_ref[...] = sb_ref[...] * dec + cs

    @pl.when(i >= nb)
    def _():
        i2 = i - nb
        slot = jnp.where(i2 < ncc, ncc - 1 - i2, nb - 1 - (i2 - ncc))
        cm = c_ref[...]
        cuf, cub = cuc_f_ref[...], cuc_b_ref[...]
        totf = cuf[q - 1:q, :]
        cb = lax.dot_general(cm, bm, (((1,), (1,)), ((), ())), preferred_element_type=F32)
        li = lax.broadcasted_iota(jnp.int32, (q, q), 0)
        si = lax.broadcasted_iota(jnp.int32, (q, q), 1)
        causal = si <= li
        cb_diag = jnp.sum(jnp.where(si == li, cb, 0.0), axis=-1, keepdims=True)
        wide = widen(jnp.concatenate([jnp.exp(cuf), jnp.exp(cub), dtc_f_ref[...] * jnp.exp(totf - cuf),
                                      dtc_b_ref[...] * cb_diag], axis=0))
        ef, eb, wf, own_b = wide[:q], wide[q:2 * q], wide[2 * q:3 * q], wide[3 * q:]
        y = (dskip_ref[...] + own_b) * xf
        y = y + _dot(cm, sf_ref[...].astype(BF16)) * ef
        y = y + _dot(cm, store_ref[slot]) * eb
        cs = lax.dot_general(bm, (xf * wf).astype(BF16), (((0,), (0,)), ((), ())),
                             preferred_element_type=F32)
        sf_ref[...] = sf_ref[...] * ef[q - 1:q, :] + cs

        lane = lax.broadcasted_iota(jnp.int32, (q, 2 * hd), 1)
        dtr_f, dtr_b = dtr_f_ref[...], dtr_b_ref[...]
        cur_f, cur_b = cur_f_ref[...], cur_b_ref[...]
        pieces = []
        for p in range(hpg // 2):
            xp = x[:, p * 2 * hd:(p + 1) * 2 * hd]
            acc = None
            for half in range(2):
                r = 2 * p + half
                log_dec = jnp.where(causal, cuf[:, r:r + 1] - cur_f[r:r + 1, :],
                                    cub[:, r:r + 1] - cur_b[r:r + 1, :])
                dt_row = jnp.where(causal, dtr_f[r:r + 1, :], dtr_b[r:r + 1, :])
                mix = (cb * (jnp.exp(log_dec) * dt_row)).astype(BF16)
                keep = (lane < hd) if half == 0 else (lane >= hd)
                part = _dot(mix, jnp.where(keep, xp, jnp.zeros_like(xp)))
                acc = part if acc is None else acc + part
            pieces.append(acc)
        y = y + jnp.concatenate(pieces, axis=1)
        g = y * _silu(z_ref[...].astype(F32))
        o_ref[...] = (g * lax.rsqrt(jnp.mean(g * g, axis=-1, keepdims=True) + EPS)
                      * nw_ref[...]).astype(o_ref.dtype)


def _ssd_scan(cfg, xbc, zx, dtc, cuc, dtr, cur, d_skip, norm_w):
    t, tr, batch = cfg["t"], ROW_TILE, cfg["batch"]
    g, n = SSD_GROUPS, SSD_STATE
    inner = cfg["d"]
    width = inner // g
    hpg = width // SSD_HEAD_DIM
    assert hpg % 2 == 0 and tr == SSD_CHUNK
    ncc, ncl = cfg["ctx_len"] // tr, cfg["seq"] // tr
    nb = ncc + ncl
    b_off, c_off = inner // n, inner // n + g

    def blk(b, i):
        return jnp.where(i < nb, _bwd_block(cfg, b, i), _fwd_block(cfg, b, i - nb))

    def oblk(b, i):
        return _fwd_block(cfg, b, jnp.maximum(i - nb, 0))

    col = lambda d: pl.BlockSpec((None, tr, hpg), lambda b, gi, i: (d * g + gi, blk(b, i), 0))
    row = lambda d: pl.BlockSpec((None, hpg, tr), lambda b, gi, i: (d * g + gi, 0, blk(b, i)))
    kern = functools.partial(_ssd_kernel, nb=nb, ncc=ncc)
    return pl.pallas_call(
        kern,
        grid=(batch, g, 2 * nb),
        in_specs=[
            pl.BlockSpec((tr, width), lambda b, gi, i: (blk(b, i), gi)),
            pl.BlockSpec((tr, n), lambda b, gi, i: (blk(b, i), b_off + gi)),
            pl.BlockSpec((tr, n), lambda b, gi, i: (blk(b, i), c_off + gi)),
            pl.BlockSpec((tr, width), lambda b, gi, i: (oblk(b, i), gi)),
            col(0), col(1), col(0), col(1), row(0), row(1), row(0), row(1),
            pl.BlockSpec((1, width), lambda b, gi, i: (0, gi)),
            pl.BlockSpec((1, width), lambda b, gi, i: (0, gi)),
        ],
        out_specs=pl.BlockSpec((tr, width), lambda b, gi, i: (oblk(b, i), gi)),
        out_shape=jax.ShapeDtypeStruct((t, inner), BF16),
        scratch_shapes=[pltpu.VMEM((n, width), F32), pltpu.VMEM((n, width), F32),
                        pltpu.VMEM((nb, n, width), BF16)],
        compiler_params=_cparams(("arbitrary", "arbitrary", "arbitrary")),
        name="ssd_scan",
    )(xbc, xbc, xbc, zx, dtc, dtc, cuc, cuc, dtr, dtr, cur, cur, d_skip, norm_w)


def _gdn_kernel(*refs, rev, finish):
    if finish:
        (q_ref, k_ref, v_ref, bc_ref, gc_ref, gr_ref, prev_ref, z_ref, nw_ref,
         o_ref, s_ref) = refs
    else:
        q_ref, k_ref, v_ref, bc_ref, gc_ref, gr_ref, o_ref, s_ref = refs
    c = GDN_CHUNK
    hw = GDN_HEAD
    n_heads = s_ref.shape[0]
    rep = n_heads // (k_ref.shape[1] // hw)

    @pl.when(pl.program_id(2) == 0)
    def _():
        s_ref[...] = jnp.zeros_like(s_ref)

    ii = lax.broadcasted_iota(jnp.int32, (c, c), 0)
    jj = lax.broadcasted_iota(jnp.int32, (c, c), 1)
    incl = (jj >= ii) if rev else (jj <= ii)
    strict = (jj > ii) if rev else (jj < ii)
    eye = jnp.where(ii == jj, 1.0, 0.0)
    n_chunks = q_ref.shape[0] // c
    order = range(n_chunks - 1, -1, -1) if rev else range(n_chunks)
    last = 0 if rev else c - 1

    heads = range(n_heads)
    chunk_rows = {ci: slice(ci * c, (ci + 1) * c) for ci in order}
    items = [(ci, h) for ci in order for h in heads]

    qk, kk = {}, {}
    for ci in order:
        rows = chunk_rows[ci]
        for kh in range(n_heads // rep):
            k_in = k_ref[rows, kh * hw:(kh + 1) * hw]
            both = lax.dot_general(jnp.concatenate([q_ref[rows, kh * hw:(kh + 1) * hw], k_in], axis=0),
                                   k_in, (((1,), (1,)), ((), ())), preferred_element_type=F32)
            qk[ci, kh], kk[ci, kh] = both[:c], both[c:]

    inv, pw, qkd = {}, {}, {}
    for ci, h in items:
        rows = chunk_rows[ci]
        gc1, gr1 = gc_ref[rows, h:h + 1], gr_ref[h:h + 1, rows]
        dec = jnp.where(incl, jnp.exp(jnp.where(incl, gc1 - gr1, 0.0)), 0.0)
        a = jnp.where(strict, bc_ref[rows, h:h + 1] * kk[ci, h // rep] * dec, 0.0)
        inv[ci, h] = eye - a
        pw[ci, h] = a.astype(BF16)
        qkd[ci, h] = (qk[ci, h // rep] * dec).astype(BF16)
    for it in items:
        pw[it] = _dot(pw[it], pw[it]).astype(BF16)
    steps = c.bit_length() - 1
    for s in range(1, steps):
        if s < steps - 1:
            both = {it: _dot(jnp.concatenate([inv[it].astype(BF16), pw[it]], axis=0), pw[it])
                    for it in items}
            for it in items:
                inv[it] = inv[it] + both[it][:c]
                pw[it] = both[it][c:].astype(BF16)
        else:
            upd = {it: _dot(inv[it].astype(BF16), pw[it]) for it in items}
            for it in items:
                inv[it] = inv[it] + upd[it]

    uw = {}
    for ci, h in items:
        rows = chunk_rows[ci]
        bc1, gc1 = bc_ref[rows, h:h + 1], gc_ref[rows, h:h + 1]
        kh = h // rep
        rhs = jnp.concatenate(
            [(v_ref[rows, h * hw:(h + 1) * hw].astype(F32) * bc1).astype(BF16),
             (k_ref[rows, kh * hw:(kh + 1) * hw].astype(F32) * (bc1 * jnp.exp(gc1))).astype(BF16)], axis=1)
        uw[ci, h] = _dot(inv[ci, h].astype(BF16), rhs)

    for ci in order:
        rows = chunk_rows[ci]
        st = {h: s_ref[h] for h in heads}
        ws_qs = {h: _dot(jnp.concatenate([uw[ci, h][:, hw:].astype(BF16),
                                          q_ref[rows, (h // rep) * hw:(h // rep + 1) * hw]], axis=0),
                         st[h].astype(BF16)) for h in heads}
        v_new = {h: (uw[ci, h][:, :hw] - ws_qs[h][:c]).astype(BF16) for h in heads}
        for h in heads:
            gc1 = gc_ref[rows, h:h + 1]
            gtot = gc1[last:last + 1, :]
            kh = h // rep
            k_dec = (k_ref[rows, kh * hw:(kh + 1) * hw].astype(F32) * jnp.exp(gtot - gc1)).astype(BF16)
            s_ref[h] = st[h] * jnp.exp(gtot) + lax.dot_general(
                k_dec, v_new[h], (((0,), (0,)), ((), ())), preferred_element_type=F32)
        for h in heads:
            out = jnp.exp(gc_ref[rows, h:h + 1]) * ws_qs[h][c:] + _dot(qkd[ci, h], v_new[h])
            cols = slice(h * hw, (h + 1) * hw)
            if finish:
                o = out + prev_ref[rows, cols]
                o = o * lax.rsqrt(jnp.mean(o * o, axis=-1, keepdims=True) + EPS) * nw_ref[...]
                o_ref[rows, cols] = (o * _silu(z_ref[rows, cols].astype(F32))).astype(o_ref.dtype)
            else:
                o_ref[rows, cols] = out


def _gdn_scan(cfg, qkv, bc, gc, gr, *, rev, prev=None, qkvz=None, norm_w=None):
    t, tr, batch = cfg["t"], ROW_TILE, cfg["batch"]
    hw = GDN_HEAD
    hv = cfg["d"] // hw
    hk = hv // 2
    hps = min(GDN_HEADS_PER_STEP, hv)
    kps = hps // 2
    nhb = hv // hps
    nb = (cfg["ctx_len"] + cfg["seq"]) // tr
    finish = prev is not None
    d = 1 if rev else 0
    order = _bwd_block if rev else _fwd_block

    def blk(b, i):
        return order(cfg, b, i)

    k_off = hk // kps
    v_off = 2 * hk // hps
    z_off = (2 * hk + hv) // hps
    col = pl.BlockSpec((None, tr, hps), lambda b, hb, i: (d * nhb + hb, blk(b, i), 0))
    row = pl.BlockSpec((None, hps, tr), lambda b, hb, i: (d * nhb + hb, 0, blk(b, i)))
    wide = lambda off: pl.BlockSpec((tr, hps * hw), lambda b, hb, i: (blk(b, i), off + hb))
    in_specs = [
        pl.BlockSpec((tr, kps * hw), lambda b, hb, i: (blk(b, i), hb)),
        pl.BlockSpec((tr, kps * hw), lambda b, hb, i: (blk(b, i), k_off + hb)),
        wide(v_off), col, col, row,
    ]
    ins = [qkv, qkv, qkv, bc, gc, gr]
    if finish:
        in_specs += [wide(0), wide(z_off), pl.BlockSpec((1, hw), lambda b, hb, i: (0, 0))]
        ins += [prev, qkvz, norm_w]
    return pl.pallas_call(
        functools.partial(_gdn_kernel, rev=rev, finish=finish),
        grid=(batch, nhb, nb),
        in_specs=in_specs,
        out_specs=wide(0),
        out_shape=jax.ShapeDtypeStruct((t, hv * hw), BF16 if finish else F32),
        scratch_shapes=[pltpu.VMEM((hps, hw, hw), F32)],
        compiler_params=_cparams(("arbitrary", "arbitrary", "arbitrary")),
        name="gdn_scan_bwd" if rev else "gdn_scan_fwd",
    )(*ins)


def _pad_lanes(v, width=128):
    v = v.reshape(1, -1).astype(F32)
    return jnp.pad(v, ((0, 0), (0, width - v.shape[1])))


def _pad_cols(w, width=128):
    return jnp.pad(w, ((0, 0), (0, width - w.shape[1])))


def _tail_weights(w_bf, w_f32, layer, main):
    if w_f32.shape[-1] - main == LANES:
        return w_bf, layer, main
    return _pad_cols(w_f32[layer][:, main:]).astype(BF16)[None], 0, 0


def _ssd_mixer(cfg, h, layer, w_in_bf, w_in, conv_w, conv_b, a_log, dt_bias, d_skip, norm_w, w_out_bf):
    d = cfg["d"]
    heads = d // SSD_HEAD_DIM
    bc_dim = SSD_GROUPS * SSD_STATE
    main = 2 * d + 2 * bc_dim
    tm = cfg["tm"]
    zx = _matmul(h, w_in_bf, layer, 0, main, BF16, tm, _pick(main, (512, 256, 128)), "ssd_in_proj")
    w_tail, tail_layer, tail_lo = _tail_weights(w_in_bf, w_in, layer, main)
    tail = _matmul(h, w_tail, tail_layer, tail_lo, tail_lo + LANES, F32, tm, LANES, "ssd_in_proj_dt")
    xbc = _conv_silu(cfg, zx, d, d + 2 * bc_dim, conv_w, conv_b)
    a_neg = -jnp.exp(a_log.astype(F32))
    dtc, cuc, dtr, cur = _prep(cfg, tail, _pad_lanes(dt_bias), _pad_lanes(a_neg),
                               kind="ssd", chunk=SSD_CHUNK, n_heads=heads, hps=heads // SSD_GROUPS)
    y = _ssd_scan(cfg, xbc, zx, dtc, cuc, dtr, cur,
                  jnp.repeat(d_skip.astype(F32), SSD_HEAD_DIM).reshape(1, d), norm_w.reshape(1, d))
    return _matmul(y, w_out_bf, layer, 0, d, BF16, tm, _pick(d, (512, 256, 128)), "ssd_out_proj")


def _gdn_mixer(cfg, h, layer, w_in_bf, w_in, conv_w, a_log, dt_bias, norm_w, w_out_bf):
    d = cfg["d"]
    hv = d // GDN_HEAD
    key_dim = (hv // 2) * GDN_HEAD
    conv_dim = 2 * key_dim + d
    main = conv_dim + d
    tm = cfg["tm"]
    qkvz = _matmul(h, w_in_bf, layer, 0, main, BF16, tm, _pick(main, (512, 256, 128)), "gdn_in_proj")
    w_tail, tail_layer, tail_lo = _tail_weights(w_in_bf, w_in, layer, main)
    tail = _matmul(h, w_tail, tail_layer, tail_lo, tail_lo + LANES, F32, tm, LANES, "gdn_in_proj_gates")
    qkv = _conv_silu(cfg, qkvz, 0, conv_dim, conv_w, jnp.zeros((conv_dim,), F32),
                     norm_ch=2 * key_dim, q_ch=key_dim, q_scale=GDN_HEAD ** -0.5)
    a_pos = jnp.exp(a_log.astype(F32)).reshape(-1)
    zeros = jnp.zeros((2 * hv,), F32)
    bias_full = _pad_lanes(jnp.concatenate([zeros, dt_bias.astype(F32).reshape(-1)]))
    amul_full = _pad_lanes(jnp.concatenate([zeros, -a_pos]))
    hps = min(GDN_HEADS_PER_STEP, hv)
    bc, gc, _, gr = _prep(cfg, tail, bias_full, amul_full, kind="gdn", chunk=GDN_CHUNK,
                           n_heads=hv, hps=hps)
    o_f = _gdn_scan(cfg, qkv, bc, gc, gr, rev=False)
    y = _gdn_scan(cfg, qkv, bc, gc, gr, rev=True, prev=o_f, qkvz=qkvz,
                  norm_w=norm_w.reshape(1, GDN_HEAD).astype(F32))
    return _matmul(y, w_out_bf, layer, 0, d, BF16, tm, _pick(d, (512, 256, 128)), "gdn_out_proj")


def _ffn(cfg, h, layer, w_in_bf, w_out_bf):
    d = cfg["d"]
    d_ff = w_in_bf.shape[-1] // 2
    hid = _matmul_swiglu(h, w_in_bf, layer, cfg["tm"], _pick(d_ff, (256, 128)))
    return _matmul(hid, w_out_bf, layer, 0, d, BF16, cfg["tm_out"], _pick(d, (512, 256, 128)), "ffn_out")


def kernel(x, c, ctx, c_ctx, ada_down, ada_up, ada_bias, norm_mix, norm_ffn, ffn_in, ffn_out,
           ssd_in, ssd_conv_w, ssd_conv_b, ssd_a_log, ssd_dt_bias, ssd_d, ssd_norm, ssd_out,
           gdn_in, gdn_conv_w, gdn_a_log, gdn_dt_bias, gdn_norm, gdn_out, final_norm):
    batch, seq, d = x.shape
    ctx_len = ctx.shape[1]
    depth = ada_down.shape[0]
    rows = seq // GRID_W
    assert rows == ROW_TILE and ctx_len % CONV_ROWS == 0 and seq % CONV_ROWS == 0
    t_lat, t_ctx = batch * seq, batch * ctx_len
    t = t_lat + t_ctx
    cfg = dict(d=d, batch=batch, seq=seq, ctx_len=ctx_len, cols=GRID_W, t_lat=t_lat, t_ctx=t_ctx, t=t,
               tm=_pick(t, (1536, 1024, 512, 256, 128)), tm_out=_pick(t, (512, 256, 128)))
    assert t_ctx % GRID_W == 0

    n_streams = batch + 1
    cv = jnp.concatenate([c.astype(F32), c_ctx.astype(F32)[None, :],
                          jnp.zeros((8 - n_streams % 8 if n_streams % 8 else 0, d), F32)], axis=0)
    mods = _ada_modulation(cv, ada_down, ada_up, ada_bias)

    def mod(layer, which):
        return mods[layer, :n_streams, which * d:(which + 1) * d].reshape(n_streams, 1, d)

    ssd_in_bf, ssd_out_bf = ssd_in.astype(BF16), ssd_out.astype(BF16)
    gdn_in_bf, gdn_out_bf = gdn_in.astype(BF16), gdn_out.astype(BF16)
    ffn_in_bf, ffn_out_bf = ffn_in.astype(BF16), ffn_out.astype(BF16)

    x_lat = x.reshape(t_lat, d)
    x_ctx = ctx.reshape(t_ctx, d)
    pending, pending_gate = None, None
    col_major = False
    for i in range(depth):
        want_col_major = (i // N_MIXERS) % 2 == 1
        assert want_col_major or not col_major
        j = i // N_MIXERS
        x_lat, x_ctx, h = _norm_modulate(cfg, x_lat, x_ctx, pending, pending_gate,
                                         norm_mix[i].reshape(1, d), mod(i, 0), mod(i, 1),
                                         to_col_major=want_col_major and not col_major)
        col_major = want_col_major
        if i % N_MIXERS == 0:
            y = _ssd_mixer(cfg, h, j, ssd_in_bf, ssd_in, ssd_conv_w[j], ssd_conv_b[j], ssd_a_log[j],
                           ssd_dt_bias[j], ssd_d[j], ssd_norm[j], ssd_out_bf)
        else:
            y = _gdn_mixer(cfg, h, j, gdn_in_bf, gdn_in, gdn_conv_w[j], gdn_a_log[j], gdn_dt_bias[j],
                           gdn_norm[j], gdn_out_bf)
        x_lat, x_ctx, h = _norm_modulate(cfg, x_lat, x_ctx, y, mod(i, 2),
                                         norm_ffn[i].reshape(1, d), mod(i, 3), mod(i, 4),
                                         to_col_major=False)
        pending, pending_gate = _ffn(cfg, h, i, ffn_in_bf, ffn_out_bf), mod(i, 5)
    out = _final_norm(cfg, x_lat, pending, pending_gate, final_norm.reshape(1, d),
                      from_col_major=col_major)
    return out.reshape(batch, seq, d)
```

```python
import functools

import jax
import jax.numpy as jnp
from jax import lax
from jax.experimental import pallas as pl
from jax.experimental.pallas import tpu as pltpu

F32 = jnp.float32
BF16 = jnp.bfloat16
EPS = 1e-6

GRID_W = 64
N_MIXERS = 2
N_MOD = 6
SSD_HEAD_DIM = 64
SSD_GROUPS = 8
SSD_STATE = 128
SSD_CHUNK = 128
SSD_GROUPS_PER_STEP = 2
GDN_HEAD = 128
GDN_CHUNK = 64
GDN_HEADS_PER_STEP = 8

ROW_TILE = 128
CONV_ROWS = 256
HALO = 8
LANES = 128
CONV_PITCH = 36
CONV_SLAB_UNROLL = 2
V7X_VMEM_LIMIT = 56 * 1024 * 1024


def _cparams(sem, vmem=None):
    return pltpu.CompilerParams(dimension_semantics=sem, vmem_limit_bytes=vmem or V7X_VMEM_LIMIT)


def _sigmoid(x):
    return 1.0 / (1.0 + jnp.exp(-x))


def _silu(x):
    h = 0.5 * x
    return h + h * jnp.tanh(h)


def _softplus(x):
    return jnp.maximum(x, 0.0) + jnp.log1p(jnp.exp(-jnp.abs(x)))


def _pick(n, candidates):
    for c in candidates:
        if n % c == 0:
            return c
    raise ValueError(f"no tile for {n} among {candidates}")


def _split3(a):
    hi = a.astype(BF16).astype(F32)
    r1 = a - hi
    mid = r1.astype(BF16).astype(F32)
    lo = (r1 - mid).astype(BF16).astype(F32)
    return hi, mid, lo


def _dot(a, b):
    return jnp.dot(a, b, preferred_element_type=F32)


def _dot_exact_rhs(m, a):
    hi, mid, lo = _split3(a)
    mb = m.astype(BF16)
    return _dot(mb, hi.astype(BF16)) + _dot(mb, mid.astype(BF16)) + _dot(mb, lo.astype(BF16))


def _ada_kernel(cv_ref, down_ref, up_ref, bias_ref, o_ref):
    t = jnp.dot(_silu(cv_ref[...]), down_ref[...], preferred_element_type=F32,
                precision=lax.Precision.HIGHEST)
    o_ref[...] = jnp.dot(t, up_ref[...], preferred_element_type=F32,
                         precision=lax.Precision.HIGHEST) + bias_ref[...]


def _ada_modulation(cv, down, up, bias):
    depth, d, r = down.shape
    n = up.shape[-1]
    tn = _pick(n, (6144, 3072, 1536, 768, 512, 256, 128))
    rows = cv.shape[0]
    return pl.pallas_call(
        _ada_kernel,
        grid=(depth, n // tn),
        in_specs=[
            pl.BlockSpec((rows, d), lambda l, j: (0, 0)),
            pl.BlockSpec((None, d, r), lambda l, j: (l, 0, 0)),
            pl.BlockSpec((None, r, tn), lambda l, j: (l, 0, j)),
            pl.BlockSpec((None, 1, tn), lambda l, j: (l, 0, j)),
        ],
        out_specs=pl.BlockSpec((None, rows, tn), lambda l, j: (l, 0, j)),
        out_shape=jax.ShapeDtypeStruct((depth, rows, n), F32),
        compiler_params=_cparams(("arbitrary", "arbitrary")),
        name="ada_modulation",
    )(cv, down, up, bias.reshape(depth, 1, n))


def _norm_kernel(*refs, has_y, n_lat_tiles):
    if has_y:
        (xl_ref, xc_ref, yl_ref, yc_ref, g_ref, w_ref, sh_ref, sc_ref,
         xlo_ref, xco_ref, h_ref) = refs
    else:
        xl_ref, xc_ref, w_ref, sh_ref, sc_ref, h_ref = refs

    def run(x_ref, y_ref, xo_ref):
        x = x_ref[...]
        if has_y:
            x = x + g_ref[...] * y_ref[...]
            xo_ref[...] = x
        xn = x * lax.rsqrt(jnp.mean(x * x, axis=-1, keepdims=True) + EPS) * w_ref[...]
        h_ref[...] = (xn * (1.0 + sc_ref[...]) + sh_ref[...]).astype(h_ref.dtype)

    i = pl.program_id(0)

    @pl.when(i < n_lat_tiles)
    def _():
        run(xl_ref, yl_ref if has_y else None, xlo_ref if has_y else None)

    @pl.when(i >= n_lat_tiles)
    def _():
        run(xc_ref, yc_ref if has_y else None, xco_ref if has_y else None)


def _norm_modulate(cfg, x_lat, x_ctx, y, gate, w, shift, scale, *, to_col_major):
    d, tr = cfg["d"], ROW_TILE
    n_lat, n_ctx = cfg["t_lat"] // tr, cfg["t_ctx"] // tr
    per_batch = cfg["seq"] // tr
    cols = cfg["cols"]
    has_y = y is not None

    def lat_tile(i):
        return jnp.minimum(i, n_lat - 1)

    def ctx_tile(i):
        return jnp.maximum(i - n_lat, 0)

    def stream(i):
        return jnp.where(i < n_lat, i // per_batch, cfg["batch"])

    if to_col_major:
        def strided(i):
            il = lat_tile(i)
            return (il // cols, il % cols)
        xl_in = x_lat.reshape(cfg["t_lat"] // cols, cols * d)
        xl_spec = pl.BlockSpec((tr, d), strided)
    else:
        xl_in = x_lat
        xl_spec = pl.BlockSpec((tr, d), lambda i: (lat_tile(i), 0))
    xc_spec = pl.BlockSpec((tr, d), lambda i: (ctx_tile(i), 0))
    mod_spec = pl.BlockSpec((None, 1, d), lambda i: (stream(i), 0, 0))
    w_spec = pl.BlockSpec((1, d), lambda i: (0, 0))
    h_spec = pl.BlockSpec((tr, d), lambda i: (i, 0))
    h_shape = jax.ShapeDtypeStruct((cfg["t"], d), BF16)

    if has_y:
        if to_col_major:
            yl_in = y.reshape(cfg["t"] // cols, cols * d)
            yl_spec = pl.BlockSpec((tr, d), strided)
        else:
            yl_in = y
            yl_spec = pl.BlockSpec((tr, d), lambda i: (lat_tile(i), 0))
        yc_spec = pl.BlockSpec((tr, d), lambda i: (n_lat + ctx_tile(i), 0))
        ins = (xl_in, x_ctx, yl_in, y, gate, w, shift, scale)
        in_specs = [xl_spec, xc_spec, yl_spec, yc_spec, mod_spec, w_spec, mod_spec, mod_spec]
        out_specs = [pl.BlockSpec((tr, d), lambda i: (lat_tile(i), 0)), xc_spec, h_spec]
        out_shape = [jax.ShapeDtypeStruct((cfg["t_lat"], d), F32),
                     jax.ShapeDtypeStruct((cfg["t_ctx"], d), F32), h_shape]
    else:
        ins = (xl_in, x_ctx, w, shift, scale)
        in_specs = [xl_spec, xc_spec, w_spec, mod_spec, mod_spec]
        out_specs = h_spec
        out_shape = h_shape
    out = pl.pallas_call(
        functools.partial(_norm_kernel, has_y=has_y, n_lat_tiles=n_lat),
        grid=(n_lat + n_ctx,),
        in_specs=in_specs, out_specs=out_specs, out_shape=out_shape,
        compiler_params=_cparams(("arbitrary",)),
        name="norm_modulate",
    )(*ins)
    if has_y:
        return out
    return x_lat, x_ctx, out


def _final_kernel(x_ref, y_ref, g_ref, w_ref, o_ref):
    x = x_ref[...] + g_ref[...] * y_ref[...]
    o_ref[...] = x * lax.rsqrt(jnp.mean(x * x, axis=-1, keepdims=True) + EPS) * w_ref[...]


def _final_norm(cfg, x_lat, y, gate, w, *, from_col_major):
    d, tr = cfg["d"], ROW_TILE
    n_lat = cfg["t_lat"] // tr
    per_batch = cfg["seq"] // tr
    cols = cfg["cols"]
    plain = pl.BlockSpec((tr, d), lambda i: (i, 0))
    if from_col_major:
        out_spec = pl.BlockSpec((tr, d), lambda i: (i // cols, i % cols))
        out_shape = jax.ShapeDtypeStruct((cfg["t_lat"] // cols, cols * d), F32)
    else:
        out_spec, out_shape = plain, jax.ShapeDtypeStruct((cfg["t_lat"], d), F32)
    out = pl.pallas_call(
        _final_kernel,
        grid=(n_lat,),
        in_specs=[plain, plain,
                  pl.BlockSpec((None, 1, d), lambda i: (i // per_batch, 0, 0)),
                  pl.BlockSpec((1, d), lambda i: (0, 0))],
        out_specs=out_spec, out_shape=out_shape,
        compiler_params=_cparams(("arbitrary",)),
        name="final_norm",
    )(x_lat, y, gate, w)
    return out.reshape(cfg["t_lat"], d)


def _mm_kernel(a_ref, w_ref, o_ref):
    o_ref[...] = _dot(a_ref[...], w_ref[...]).astype(o_ref.dtype)


def _matmul(a, w, layer, col_lo, col_hi, out_dtype, tm, tn, name):
    m, k = a.shape
    n = col_hi - col_lo
    assert m % tm == 0 and n % tn == 0 and col_lo % tn == 0
    off = col_lo // tn
    return pl.pallas_call(
        _mm_kernel,
        grid=(m // tm, n // tn),
        in_specs=[pl.BlockSpec((tm, k), lambda i, j: (i, 0)),
                  pl.BlockSpec((None, k, tn), lambda i, j: (layer, 0, off + j))],
        out_specs=pl.BlockSpec((tm, tn), lambda i, j: (i, j)),
        out_shape=jax.ShapeDtypeStruct((m, n), out_dtype),
        compiler_params=_cparams(("arbitrary", "arbitrary")),
        name=name,
    )(a, w)


def _swiglu_kernel(a_ref, wg_ref, wu_ref, o_ref):
    a = a_ref[...]
    g = _dot(a, wg_ref[...])
    u = _dot(a, wu_ref[...])
    o_ref[...] = (_silu(g) * u).astype(o_ref.dtype)


def _matmul_swiglu(a, w, layer, tm, tn):
    m, k = a.shape
    d_ff = w.shape[-1] // 2
    assert m % tm == 0 and d_ff % tn == 0
    nj = d_ff // tn
    return pl.pallas_call(
        _swiglu_kernel,
        grid=(m // tm, nj),
        in_specs=[pl.BlockSpec((tm, k), lambda i, j: (i, 0)),
                  pl.BlockSpec((None, k, tn), lambda i, j: (layer, 0, j)),
                  pl.BlockSpec((None, k, tn), lambda i, j: (layer, 0, nj + j))],
        out_specs=pl.BlockSpec((tm, tn), lambda i, j: (i, j)),
        out_shape=jax.ShapeDtypeStruct((m, d_ff), BF16),
        compiler_params=_cparams(("arbitrary", "arbitrary")),
        name="ffn_in_swiglu",
    )(a, w, w)


def _conv_kernel(prev_ref, cur_ref, next_ref, w_ref, b_ref, o_ref, scr_ref, out_ref, *,
                 tiles_per_lat_seq, n_lat_tiles, tiles_per_ctx_seq, n_norm_tiles, n_q_tiles, q_scale):
    i, j = pl.program_id(0), pl.program_id(1)
    ts = cur_ref.shape[0]
    ksz = w_ref.shape[0]
    is_lat = i < n_lat_tiles
    pos = jnp.where(is_lat, i % tiles_per_lat_seq, (i - n_lat_tiles) % tiles_per_ctx_seq)
    per_seq = jnp.where(is_lat, tiles_per_lat_seq, tiles_per_ctx_seq)
    keep_prev = jnp.where(pos == 0, 0.0, 1.0)
    keep_next = jnp.where(pos == per_seq - 1, 0.0, 1.0)
    if n_norm_tiles:
        normed = j < n_norm_tiles
        scale = jnp.where(j < n_q_tiles, q_scale, 1.0)
    first = HALO - ksz // 2
    n_slabs = cur_ref.shape[1] // LANES

    def one_slab(sl):
        lanes = pl.ds(pl.multiple_of(sl * LANES, LANES), LANES)
        scr_ref[sl, 0:HALO, :] = prev_ref[:, lanes].astype(F32) * keep_prev
        scr_ref[sl, HALO:HALO + ts, :] = cur_ref[:, lanes].astype(F32)
        scr_ref[sl, HALO + ts:2 * HALO + ts, :] = next_ref[:, lanes].astype(F32) * keep_next
        scr_ref[sl, 2 * HALO + ts:, :] = jnp.zeros((scr_ref.shape[1] - 2 * HALO - ts, LANES), F32)
        wk = [jnp.broadcast_to(w_ref[k:k + 1, lanes], (8, LANES)) for k in range(ksz)]
        bias = jnp.broadcast_to(b_ref[:, lanes], (8, LANES))
        taps = [scr_ref[sl, pl.ds(first + g, 8, stride=CONV_PITCH), :]
                for g in range(CONV_PITCH + ksz - 1)]
        for g in range(CONV_PITCH):
            acc = bias + wk[0] * taps[g]
            for k in range(1, ksz):
                acc = acc + wk[k] * taps[g + k]
            y = _silu(acc)
            if n_norm_tiles:
                r = lax.rsqrt(jnp.sum(y * y, axis=-1, keepdims=True) + EPS) * scale
                y = y * jnp.where(normed, r, 1.0)
            out_ref[sl, pl.ds(g, 8, stride=CONV_PITCH), :] = y
        o_ref[:, lanes] = out_ref[sl, 0:ts, :].astype(o_ref.dtype)

    @pl.loop(0, n_slabs // CONV_SLAB_UNROLL)
    def _(it):
        for u in range(CONV_SLAB_UNROLL):
            one_slab(it * CONV_SLAB_UNROLL + u)


def _conv_silu(cfg, src, col_lo, n_ch, w, b, *, norm_ch=0, q_ch=0, q_scale=1.0):
    t, ts = cfg["t"], CONV_ROWS
    tc = next(c for c in (2048, 1024, 512, 256)
              if all(v % c == 0 for v in (n_ch, col_lo, norm_ch, q_ch)))
    assert (tc // LANES) % CONV_SLAB_UNROLL == 0
    assert cfg["seq"] % ts == 0 and cfg["ctx_len"] % ts == 0 and 8 * CONV_PITCH >= ts
    ksz = w.shape[0]
    last_row = (HALO - ksz // 2) + (CONV_PITCH + ksz - 2) + 7 * CONV_PITCH
    in_rows = -(-(last_row + 1) // 8) * 8
    assert in_rows >= ts + 2 * HALO
    off = col_lo // tc
    hb = ts // HALO
    last_hb = t // HALO - 1
    kern = functools.partial(
        _conv_kernel, tiles_per_lat_seq=cfg["seq"] // ts, n_lat_tiles=cfg["t_lat"] // ts,
        tiles_per_ctx_seq=cfg["ctx_len"] // ts, n_norm_tiles=norm_ch // tc, n_q_tiles=q_ch // tc,
        q_scale=q_scale)
    return pl.pallas_call(
        kern,
        grid=(t // ts, n_ch // tc),
        in_specs=[
            pl.BlockSpec((HALO, tc), lambda i, j: (jnp.maximum(i * hb - 1, 0), off + j)),
            pl.BlockSpec((ts, tc), lambda i, j: (i, off + j)),
            pl.BlockSpec((HALO, tc), lambda i, j: (jnp.minimum((i + 1) * hb, last_hb), off + j)),
            pl.BlockSpec((w.shape[0], tc), lambda i, j: (0, j)),
            pl.BlockSpec((1, tc), lambda i, j: (0, j)),
        ],
        out_specs=pl.BlockSpec((ts, tc), lambda i, j: (i, j)),
        out_shape=jax.ShapeDtypeStruct((t, n_ch), BF16),
        scratch_shapes=[pltpu.VMEM((tc // LANES, in_rows, LANES), F32),
                        pltpu.VMEM((tc // LANES, 8 * CONV_PITCH, LANES), F32)],
        compiler_params=_cparams(("arbitrary", "arbitrary")),
        name="conv_silu",
    )(src, src, src, w, b.reshape(1, n_ch))


def _prep_kernel(tail_ref, bias_ref, amul_ref, p1c_ref, p2c_ref, p1r_ref, p2r_ref, *,
                 kind, chunk, p2_off, split_lane, hps):
    tr = tail_ref.shape[0]
    raw = tail_ref[...]
    sp = _softplus(raw + bias_ref[...])
    if kind == "ssd":
        p1 = sp
    else:
        p1 = _sigmoid(raw)
    a = sp * amul_ref[...]
    row = lax.broadcasted_iota(jnp.int32, (tr, tr), 0)
    col = lax.broadcasted_iota(jnp.int32, (tr, tr), 1)
    same = (row // chunk) == (col // chunk)
    lower = jnp.where(same & (col <= row), 1.0, 0.0)
    upper = jnp.where(same & (col >= row), 1.0, 0.0)
    lane = lax.broadcasted_iota(jnp.int32, raw.shape, 1)
    cum = jnp.where(lane < split_lane, _dot_exact_rhs(lower, a), _dot_exact_rhs(upper, a))
    p1t = p1.T
    cumt = cum.T
    for n in range(p1c_ref.shape[0]):
        p1c_ref[n] = p1[:, n * hps:(n + 1) * hps]
        p2c_ref[n] = cum[:, p2_off + n * hps:p2_off + (n + 1) * hps]
        p1r_ref[n] = p1t[n * hps:(n + 1) * hps, :]
        p2r_ref[n] = cumt[p2_off + n * hps:p2_off + (n + 1) * hps, :]


def _prep(cfg, tail, bias_full, amul_full, *, kind, chunk, n_heads, hps):
    t, tr = cfg["t"], ROW_TILE
    nblk = 2 * n_heads // hps
    p2_off = 0 if kind == "ssd" else 2 * n_heads
    split_lane = p2_off + n_heads
    kern = functools.partial(_prep_kernel, kind=kind, chunk=chunk, p2_off=p2_off,
                             split_lane=split_lane, hps=hps)
    col_spec = pl.BlockSpec((nblk, tr, hps), lambda i: (0, i, 0))
    row_spec = pl.BlockSpec((nblk, hps, tr), lambda i: (0, 0, i))
    col_shape = jax.ShapeDtypeStruct((nblk, t, hps), F32)
    row_shape = jax.ShapeDtypeStruct((nblk, hps, t), F32)
    return pl.pallas_call(
        kern,
        grid=(t // tr,),
        in_specs=[pl.BlockSpec((tr, 128), lambda i: (i, 0)),
                  pl.BlockSpec((1, 128), lambda i: (0, 0)),
                  pl.BlockSpec((1, 128), lambda i: (0, 0))],
        out_specs=[col_spec, col_spec, row_spec, row_spec],
        out_shape=[col_shape, col_shape, row_shape, row_shape],
        compiler_params=_cparams(("arbitrary",)),
        name=f"prep_{kind}",
    )(tail, bias_full, amul_full)


def _fwd_block(cfg, b, i):
    ncc, ncl = cfg["ctx_len"] // ROW_TILE, cfg["seq"] // ROW_TILE
    ctx = cfg["batch"] * ncl + b * ncc + i
    lat = b * ncl + (i - ncc)
    return jnp.where(i < ncc, ctx, lat)


def _bwd_block(cfg, b, i):
    ncc, ncl = cfg["ctx_len"] // ROW_TILE, cfg["seq"] // ROW_TILE
    ctx = cfg["batch"] * ncl + b * ncc + (ncc - 1 - i)
    lat = b * ncl + (ncl - 1 - (i - ncc))
    return jnp.where(i < ncc, ctx, lat)


def _ssd_kernel(x_ref, b_ref, c_ref, z_ref, dtc_f_ref, dtc_b_ref, cuc_f_ref, cuc_b_ref,
                dtr_f_ref, dtr_b_ref, cur_f_ref, cur_b_ref, dskip_ref, nw_ref,
                o_ref, sf_ref, sb_ref, store_ref, *, nb, ncc):
    i = pl.program_id(2)
    q = x_ref.shape[0]
    gps = dtc_f_ref.shape[0]
    hpg = dtc_f_ref.shape[-1]
    width = x_ref.shape[1] // gps
    n = b_ref.shape[1] // gps
    hd = width // hpg
    groups = range(gps)
    expand = jnp.where(lax.broadcasted_iota(jnp.int32, (hpg, width), 1) // hd
                       == lax.broadcasted_iota(jnp.int32, (hpg, width), 0), 1.0, 0.0).astype(BF16)

    def widen(a):
        hi = a.astype(BF16)
        lo = (a - hi.astype(F32)).astype(BF16)
        return _dot(hi, expand) + _dot(lo, expand)

    def cols(gg, w):
        return slice(gg * w, (gg + 1) * w)

    @pl.when(i == 0)
    def _():
        sb_ref[...] = jnp.zeros_like(sb_ref)

    @pl.when(i == nb)
    def _():
        sf_ref[...] = jnp.zeros_like(sf_ref)

    @pl.when(i < nb)
    def _():
        for gg in groups:
            cub = cuc_b_ref[gg]
            tot = cub[0:1, :]
            wide = widen(jnp.concatenate([dtc_b_ref[gg] * jnp.exp(tot - cub), jnp.exp(cub[0:8, :])], axis=0))
            wgt, dec = wide[:q], wide[q:q + 1]
            xs = (x_ref[:, cols(gg, width)].astype(F32) * wgt).astype(BF16)
            cs = lax.dot_general(b_ref[:, cols(gg, n)], xs, (((0,), (0,)), ((), ())),
                                 preferred_element_type=F32)
            store_ref[i, gg] = sb_ref[gg].astype(BF16)
            sb_ref[gg] = sb_ref[gg] * dec + cs

    @pl.when(i >= nb)
    def _():
        i2 = i - nb
        slot = jnp.where(i2 < ncc, ncc - 1 - i2, nb - 1 - (i2 - ncc))
        li = lax.broadcasted_iota(jnp.int32, (q, q), 0)
        si = lax.broadcasted_iota(jnp.int32, (q, q), 1)
        causal = si <= li
        lane = lax.broadcasted_iota(jnp.int32, (q, 2 * hd), 1)
        cb = [lax.dot_general(c_ref[:, cols(gg, n)], b_ref[:, cols(gg, n)], (((1,), (1,)), ((), ())),
                              preferred_element_type=F32) for gg in groups]
        wide = []
        for gg in groups:
            cuf = cuc_f_ref[gg]
            wide.append(widen(jnp.concatenate(
                [jnp.exp(cuf), jnp.exp(cuc_b_ref[gg]), dtc_f_ref[gg] * jnp.exp(cuf[q - 1:q, :] - cuf)], axis=0)))
        y_off = []
        for gg in groups:
            ef, eb, wf = wide[gg][:q], wide[gg][q:2 * q], wide[gg][2 * q:]
            cm = c_ref[:, cols(gg, n)]
            y_off.append(_dot(cm, sf_ref[gg].astype(BF16)) * ef + _dot(cm, store_ref[slot, gg]) * eb)
            xs = (x_ref[:, cols(gg, width)].astype(F32) * wf).astype(BF16)
            cs = lax.dot_general(b_ref[:, cols(gg, n)], xs, (((0,), (0,)), ((), ())),
                                 preferred_element_type=F32)
            sf_ref[gg] = sf_ref[gg] * ef[q - 1:q, :] + cs

        y_diag = [[] for _ in groups]
        for p in range(hpg // 2):
            for gg in groups:
                xp = x_ref[:, gg * width + p * 2 * hd:gg * width + (p + 1) * 2 * hd]
                acc = None
                for half in range(2):
                    r = 2 * p + half
                    log_dec = jnp.where(causal, cuc_f_ref[gg, :, r:r + 1] - cur_f_ref[gg, r:r + 1, :],
                                        cuc_b_ref[gg, :, r:r + 1] - cur_b_ref[gg, r:r + 1, :])
                    dt_row = jnp.where(causal, dtr_f_ref[gg, r:r + 1, :], dtr_b_ref[gg, r:r + 1, :])
                    mix = (cb[gg] * (jnp.exp(log_dec) * dt_row)).astype(BF16)
                    keep = (lane < hd) if half == 0 else (lane >= hd)
                    part = _dot(mix, jnp.where(keep, xp, jnp.zeros_like(xp)))
                    acc = part if acc is None else acc + part
                y_diag[gg].append(acc)

        for gg in groups:
            cb_diag = jnp.sum(jnp.where(si == li, cb[gg], 0.0), axis=-1, keepdims=True)
            own_b = widen(dtc_b_ref[gg] * cb_diag)
            xf = x_ref[:, cols(gg, width)].astype(F32)
            y = (dskip_ref[:, cols(gg, width)] + own_b) * xf + y_off[gg] + jnp.concatenate(y_diag[gg], axis=1)
            gated = y * _silu(z_ref[:, cols(gg, width)].astype(F32))
            o_ref[:, cols(gg, width)] = (
                gated * lax.rsqrt(jnp.mean(gated * gated, axis=-1, keepdims=True) + EPS)
                * nw_ref[:, cols(gg, width)]).astype(o_ref.dtype)


def _ssd_scan(cfg, xbc, zx, dtc, cuc, dtr, cur, d_skip, norm_w):
    t, tr, batch = cfg["t"], ROW_TILE, cfg["batch"]
    g, n = SSD_GROUPS, SSD_STATE
    gps = SSD_GROUPS_PER_STEP if g % SSD_GROUPS_PER_STEP == 0 else 1
    inner = cfg["d"]
    width = inner // g
    hpg = width // SSD_HEAD_DIM
    assert hpg % 2 == 0 and tr == SSD_CHUNK and inner % (gps * n) == 0
    ncc, ncl = cfg["ctx_len"] // tr, cfg["seq"] // tr
    nb = ncc + ncl
    b_off, c_off = inner // (gps * n), (inner + g * n) // (gps * n)

    def blk(b, i):
        return jnp.where(i < nb, _bwd_block(cfg, b, i), _fwd_block(cfg, b, i - nb))

    def oblk(b, i):
        return _fwd_block(cfg, b, jnp.maximum(i - nb, 0))

    col = lambda d: pl.BlockSpec((gps, tr, hpg), lambda b, gi, i: (d * (g // gps) + gi, blk(b, i), 0))
    row = lambda d: pl.BlockSpec((gps, hpg, tr), lambda b, gi, i: (d * (g // gps) + gi, 0, blk(b, i)))
    kern = functools.partial(_ssd_kernel, nb=nb, ncc=ncc)
    return pl.pallas_call(
        kern,
        grid=(batch, g // gps, 2 * nb),
        in_specs=[
            pl.BlockSpec((tr, gps * width), lambda b, gi, i: (blk(b, i), gi)),
            pl.BlockSpec((tr, gps * n), lambda b, gi, i: (blk(b, i), b_off + gi)),
            pl.BlockSpec((tr, gps * n), lambda b, gi, i: (blk(b, i), c_off + gi)),
            pl.BlockSpec((tr, gps * width), lambda b, gi, i: (oblk(b, i), gi)),
            col(0), col(1), col(0), col(1), row(0), row(1), row(0), row(1),
            pl.BlockSpec((1, gps * width), lambda b, gi, i: (0, gi)),
            pl.BlockSpec((1, gps * width), lambda b, gi, i: (0, gi)),
        ],
        out_specs=pl.BlockSpec((tr, gps * width), lambda b, gi, i: (oblk(b, i), gi)),
        out_shape=jax.ShapeDtypeStruct((t, inner), BF16),
        scratch_shapes=[pltpu.VMEM((gps, n, width), F32), pltpu.VMEM((gps, n, width), F32),
                        pltpu.VMEM((nb, gps, n, width), BF16)],
        compiler_params=_cparams(("arbitrary", "arbitrary", "arbitrary")),
        name="ssd_scan",
    )(xbc, xbc, xbc, zx, dtc, dtc, cuc, cuc, dtr, dtr, cur, cur, d_skip, norm_w)


def _gdn_kernel(*refs, rev, finish):
    if finish:
        (q_ref, k_ref, v_ref, bc_ref, gc_ref, gr_ref, prev_ref, z_ref, nw_ref,
         o_ref, s_ref) = refs
    else:
        q_ref, k_ref, v_ref, bc_ref, gc_ref, gr_ref, o_ref, s_ref = refs
    c = GDN_CHUNK
    hw = GDN_HEAD
    n_heads = s_ref.shape[0]
    rep = n_heads // (k_ref.shape[1] // hw)

    @pl.when(pl.program_id(2) == 0)
    def _():
        s_ref[...] = jnp.zeros_like(s_ref)

    ii = lax.broadcasted_iota(jnp.int32, (c, c), 0)
    jj = lax.broadcasted_iota(jnp.int32, (c, c), 1)
    incl = (jj >= ii) if rev else (jj <= ii)
    strict = (jj > ii) if rev else (jj < ii)
    eye = jnp.where(ii == jj, 1.0, 0.0)
    n_chunks = q_ref.shape[0] // c
    order = range(n_chunks - 1, -1, -1) if rev else range(n_chunks)
    last = 0 if rev else c - 1

    heads = range(n_heads)
    chunk_rows = {ci: slice(ci * c, (ci + 1) * c) for ci in order}
    items = [(ci, h) for ci in order for h in heads]

    qk, kk = {}, {}
    for ci in order:
        rows = chunk_rows[ci]
        for kh in range(n_heads // rep):
            k_in = k_ref[rows, kh * hw:(kh + 1) * hw]
            both = lax.dot_general(jnp.concatenate([q_ref[rows, kh * hw:(kh + 1) * hw], k_in], axis=0),
                                   k_in, (((1,), (1,)), ((), ())), preferred_element_type=F32)
            qk[ci, kh], kk[ci, kh] = both[:c], both[c:]

    inv, pw, qkd = {}, {}, {}
    for ci, h in items:
        rows = chunk_rows[ci]
        gc1, gr1 = gc_ref[rows, h:h + 1], gr_ref[h:h + 1, rows]
        dec = jnp.where(incl, jnp.exp(jnp.where(incl, gc1 - gr1, 0.0)), 0.0)
        a = jnp.where(strict, bc_ref[rows, h:h + 1] * kk[ci, h // rep] * dec, 0.0)
        inv[ci, h] = eye - a
        pw[ci, h] = a.astype(BF16)
        qkd[ci, h] = (qk[ci, h // rep] * dec).astype(BF16)
    for it in items:
        pw[it] = _dot(pw[it], pw[it]).astype(BF16)
    steps = c.bit_length() - 1
    for s in range(1, steps):
        if s < steps - 1:
            both = {it: _dot(jnp.concatenate([inv[it].astype(BF16), pw[it]], axis=0), pw[it])
                    for it in items}
            for it in items:
                inv[it] = inv[it] + both[it][:c]
                pw[it] = both[it][c:].astype(BF16)
        else:
            upd = {it: _dot(inv[it].astype(BF16), pw[it]) for it in items}
            for it in items:
                inv[it] = inv[it] + upd[it]

    uw = {}
    for ci, h in items:
        rows = chunk_rows[ci]
        bc1, gc1 = bc_ref[rows, h:h + 1], gc_ref[rows, h:h + 1]
        kh = h // rep
        rhs = jnp.concatenate(
            [(v_ref[rows, h * hw:(h + 1) * hw].astype(F32) * bc1).astype(BF16),
             (k_ref[rows, kh * hw:(kh + 1) * hw].astype(F32) * (bc1 * jnp.exp(gc1))).astype(BF16)], axis=1)
        uw[ci, h] = _dot(inv[ci, h].astype(BF16), rhs)

    for ci in order:
        rows = chunk_rows[ci]
        st = {h: s_ref[h] for h in heads}
        ws_qs = {h: _dot(jnp.concatenate([uw[ci, h][:, hw:].astype(BF16),
                                          q_ref[rows, (h // rep) * hw:(h // rep + 1) * hw]], axis=0),
                         st[h].astype(BF16)) for h in heads}
        v_new = {h: (uw[ci, h][:, :hw] - ws_qs[h][:c]).astype(BF16) for h in heads}
        for h in heads:
            gc1 = gc_ref[rows, h:h + 1]
            gtot = gc1[last:last + 1, :]
            kh = h // rep
            k_dec = (k_ref[rows, kh * hw:(kh + 1) * hw].astype(F32) * jnp.exp(gtot - gc1)).astype(BF16)
            s_ref[h] = st[h] * jnp.exp(gtot) + lax.dot_general(
                k_dec, v_new[h], (((0,), (0,)), ((), ())), preferred_element_type=F32)
        for h in heads:
            out = jnp.exp(gc_ref[rows, h:h + 1]) * ws_qs[h][c:] + _dot(qkd[ci, h], v_new[h])
            cols = slice(h * hw, (h + 1) * hw)
            if finish:
                o = out + prev_ref[rows, cols]
                o = o * lax.rsqrt(jnp.mean(o * o, axis=-1, keepdims=True) + EPS) * nw_ref[...]
                o_ref[rows, cols] = (o * _silu(z_ref[rows, cols].astype(F32))).astype(o_ref.dtype)
            else:
                o_ref[rows, cols] = out


def _gdn_scan(cfg, qkv, bc, gc, gr, *, rev, prev=None, qkvz=None, norm_w=None):
    t, tr, batch = cfg["t"], ROW_TILE, cfg["batch"]
    hw = GDN_HEAD
    hv = cfg["d"] // hw
    hk = hv // 2
    hps = min(GDN_HEADS_PER_STEP, hv)
    kps = hps // 2
    nhb = hv // hps
    nb = (cfg["ctx_len"] + cfg["seq"]) // tr
    finish = prev is not None
    d = 1 if rev else 0
    order = _bwd_block if rev else _fwd_block

    def blk(b, i):
        return order(cfg, b, i)

    k_off = hk // kps
    v_off = 2 * hk // hps
    z_off = (2 * hk + hv) // hps
    col = pl.BlockSpec((None, tr, hps), lambda b, hb, i: (d * nhb + hb, blk(b, i), 0))
    row = pl.BlockSpec((None, hps, tr), lambda b, hb, i: (d * nhb + hb, 0, blk(b, i)))
    wide = lambda off: pl.BlockSpec((tr, hps * hw), lambda b, hb, i: (blk(b, i), off + hb))
    in_specs = [
        pl.BlockSpec((tr, kps * hw), lambda b, hb, i: (blk(b, i), hb)),
        pl.BlockSpec((tr, kps * hw), lambda b, hb, i: (blk(b, i), k_off + hb)),
        wide(v_off), col, col, row,
    ]
    ins = [qkv, qkv, qkv, bc, gc, gr]
    if finish:
        in_specs += [wide(0), wide(z_off), pl.BlockSpec((1, hw), lambda b, hb, i: (0, 0))]
        ins += [prev, qkvz, norm_w]
    return pl.pallas_call(
        functools.partial(_gdn_kernel, rev=rev, finish=finish),
        grid=(batch, nhb, nb),
        in_specs=in_specs,
        out_specs=wide(0),
        out_shape=jax.ShapeDtypeStruct((t, hv * hw), BF16 if finish else F32),
        scratch_shapes=[pltpu.VMEM((hps, hw, hw), F32)],
        compiler_params=_cparams(("arbitrary", "arbitrary", "arbitrary")),
        name="gdn_scan_bwd" if rev else "gdn_scan_fwd",
    )(*ins)


def _pad_lanes(v, width=128):
    v = v.reshape(1, -1).astype(F32)
    return jnp.pad(v, ((0, 0), (0, width - v.shape[1])))


def _pad_cols(w, width=128):
    return jnp.pad(w, ((0, 0), (0, width - w.shape[1])))


def _tail_weights(w_bf, w_f32, layer, main):
    if w_f32.shape[-1] - main == LANES:
        return w_bf, layer, main
    return _pad_cols(w_f32[layer][:, main:]).astype(BF16)[None], 0, 0


def _ssd_mixer(cfg, h, layer, w_in_bf, w_in, conv_w, conv_b, a_log, dt_bias, d_skip, norm_w, w_out_bf):
    d = cfg["d"]
    heads = d // SSD_HEAD_DIM
    bc_dim = SSD_GROUPS * SSD_STATE
    main = 2 * d + 2 * bc_dim
    tm = cfg["tm"]
    zx = _matmul(h, w_in_bf, layer, 0, main, BF16, tm, _pick(main, (512, 256, 128)), "ssd_in_proj")
    w_tail, tail_layer, tail_lo = _tail_weights(w_in_bf, w_in, layer, main)
    tail = _matmul(h, w_tail, tail_layer, tail_lo, tail_lo + LANES, F32, tm, LANES, "ssd_in_proj_dt")
    xbc = _conv_silu(cfg, zx, d, d + 2 * bc_dim, conv_w, conv_b)
    a_neg = -jnp.exp(a_log.astype(F32))
    dtc, cuc, dtr, cur = _prep(cfg, tail, _pad_lanes(dt_bias), _pad_lanes(a_neg),
                               kind="ssd", chunk=SSD_CHUNK, n_heads=heads, hps=heads // SSD_GROUPS)
    y = _ssd_scan(cfg, xbc, zx, dtc, cuc, dtr, cur,
                  jnp.repeat(d_skip.astype(F32), SSD_HEAD_DIM).reshape(1, d), norm_w.reshape(1, d))
    return _matmul(y, w_out_bf, layer, 0, d, BF16, tm, _pick(d, (512, 256, 128)), "ssd_out_proj")


def _gdn_mixer(cfg, h, layer, w_in_bf, w_in, conv_w, a_log, dt_bias, norm_w, w_out_bf):
    d = cfg["d"]
    hv = d // GDN_HEAD
    key_dim = (hv // 2) * GDN_HEAD
    conv_dim = 2 * key_dim + d
    main = conv_dim + d
    tm = cfg["tm"]
    qkvz = _matmul(h, w_in_bf, layer, 0, main, BF16, tm, _pick(main, (512, 256, 128)), "gdn_in_proj")
    w_tail, tail_layer, tail_lo = _tail_weights(w_in_bf, w_in, layer, main)
    tail = _matmul(h, w_tail, tail_layer, tail_lo, tail_lo + LANES, F32, tm, LANES, "gdn_in_proj_gates")
    qkv = _conv_silu(cfg, qkvz, 0, conv_dim, conv_w, jnp.zeros((conv_dim,), F32),
                     norm_ch=2 * key_dim, q_ch=key_dim, q_scale=GDN_HEAD ** -0.5)
    a_pos = jnp.exp(a_log.astype(F32)).reshape(-1)
    zeros = jnp.zeros((2 * hv,), F32)
    bias_full = _pad_lanes(jnp.concatenate([zeros, dt_bias.astype(F32).reshape(-1)]))
    amul_full = _pad_lanes(jnp.concatenate([zeros, -a_pos]))
    hps = min(GDN_HEADS_PER_STEP, hv)
    bc, gc, _, gr = _prep(cfg, tail, bias_full, amul_full, kind="gdn", chunk=GDN_CHUNK,
                           n_heads=hv, hps=hps)
    o_f = _gdn_scan(cfg, qkv, bc, gc, gr, rev=False)
    y = _gdn_scan(cfg, qkv, bc, gc, gr, rev=True, prev=o_f, qkvz=qkvz,
                  norm_w=norm_w.reshape(1, GDN_HEAD).astype(F32))
    return _matmul(y, w_out_bf, layer, 0, d, BF16, tm, _pick(d, (512, 256, 128)), "gdn_out_proj")


def _ffn(cfg, h, layer, w_in_bf, w_out_bf):
    d = cfg["d"]
    d_ff = w_in_bf.shape[-1] // 2
    hid = _matmul_swiglu(h, w_in_bf, layer, cfg["tm"], _pick(d_ff, (256, 128)))
    return _matmul(hid, w_out_bf, layer, 0, d, BF16, cfg["tm_out"], _pick(d, (512, 256, 128)), "ffn_out")


def kernel(x, c, ctx, c_ctx, ada_down, ada_up, ada_bias, norm_mix, norm_ffn, ffn_in, ffn_out,
           ssd_in, ssd_conv_w, ssd_conv_b, ssd_a_log, ssd_dt_bias, ssd_d, ssd_norm, ssd_out,
           gdn_in, gdn_conv_w, gdn_a_log, gdn_dt_bias, gdn_norm, gdn_out, final_norm):
    batch, seq, d = x.shape
    ctx_len = ctx.shape[1]
    depth = ada_down.shape[0]
    rows = seq // GRID_W
    assert rows == ROW_TILE and ctx_len % CONV_ROWS == 0 and seq % CONV_ROWS == 0
    t_lat, t_ctx = batch * seq, batch * ctx_len
    t = t_lat + t_ctx
    cfg = dict(d=d, batch=batch, seq=seq, ctx_len=ctx_len, cols=GRID_W, t_lat=t_lat, t_ctx=t_ctx, t=t,
               tm=_pick(t, (1536, 1024, 512, 256, 128)), tm_out=_pick(t, (512, 256, 128)))
    assert t_ctx % GRID_W == 0

    n_streams = batch + 1
    cv = jnp.concatenate([c.astype(F32), c_ctx.astype(F32)[None, :],
                          jnp.zeros((8 - n_streams % 8 if n_streams % 8 else 0, d), F32)], axis=0)
    mods = _ada_modulation(cv, ada_down, ada_up, ada_bias)

    def mod(layer, which):
        return mods[layer, :n_streams, which * d:(which + 1) * d].reshape(n_streams, 1, d)

    ssd_in_bf, ssd_out_bf = ssd_in.astype(BF16), ssd_out.astype(BF16)
    gdn_in_bf, gdn_out_bf = gdn_in.astype(BF16), gdn_out.astype(BF16)
    ffn_in_bf, ffn_out_bf = ffn_in.astype(BF16), ffn_out.astype(BF16)

    x_lat = x.reshape(t_lat, d)
    x_ctx = ctx.reshape(t_ctx, d)
    pending, pending_gate = None, None
    col_major = False
    for i in range(depth):
        want_col_major = (i // N_MIXERS) % 2 == 1
        assert want_col_major or not col_major
        j = i // N_MIXERS
        x_lat, x_ctx, h = _norm_modulate(cfg, x_lat, x_ctx, pending, pending_gate,
                                         norm_mix[i].reshape(1, d), mod(i, 0), mod(i, 1),
                                         to_col_major=want_col_major and not col_major)
        col_major = want_col_major
        if i % N_MIXERS == 0:
            y = _ssd_mixer(cfg, h, j, ssd_in_bf, ssd_in, ssd_conv_w[j], ssd_conv_b[j], ssd_a_log[j],
                           ssd_dt_bias[j], ssd_d[j], ssd_norm[j], ssd_out_bf)
        else:
            y = _gdn_mixer(cfg, h, j, gdn_in_bf, gdn_in, gdn_conv_w[j], gdn_a_log[j], gdn_dt_bias[j],
                           gdn_norm[j], gdn_out_bf)
        x_lat, x_ctx, h = _norm_modulate(cfg, x_lat, x_ctx, y, mod(i, 2),
                                         norm_ffn[i].reshape(1, d), mod(i, 3), mod(i, 4),
                                         to_col_major=False)
        pending, pending_gate = _ffn(cfg, h, i, ffn_in_bf, ffn_out_bf), mod(i, 5)
    out = _final_norm(cfg, x_lat, pending, pending_gate, final_norm.reshape(1, d),
                      from_col_major=col_major)
    return out.reshape(batch, seq, d)
```

```python
import functools

import jax
import jax.numpy as jnp
from jax import lax
from jax.experimental import pallas as pl
from jax.experimental.pallas import tpu as pltpu

F32 = jnp.float32
BF16 = jnp.bfloat16
EPS = 1e-6

GRID_W = 64
N_MIXERS = 2
N_MOD = 6
SSD_HEAD_DIM = 64
SSD_GROUPS = 8
SSD_STATE = 128
SSD_CHUNK = 128
SSD_GROUPS_PER_STEP = 4
GDN_HEAD = 128
GDN_CHUNK = 64
GDN_HEADS_PER_STEP = 8

ROW_TILE = 128
CONV_ROWS = 256
HALO = 8
LANES = 128
CONV_PITCH = 36
CONV_SLAB_UNROLL = 2
V7X_VMEM_LIMIT = 56 * 1024 * 1024


def _cparams(sem, vmem=None):
    return pltpu.CompilerParams(dimension_semantics=sem, vmem_limit_bytes=vmem or V7X_VMEM_LIMIT)


def _sigmoid(x):
    return 1.0 / (1.0 + jnp.exp(-x))


def _silu(x):
    h = 0.5 * x
    return h + h * jnp.tanh(h)


def _softplus(x):
    return jnp.maximum(x, 0.0) + jnp.log1p(jnp.exp(-jnp.abs(x)))


def _pick(n, candidates):
    for c in candidates:
        if n % c == 0:
            return c
    raise ValueError(f"no tile for {n} among {candidates}")


def _split3(a):
    hi = a.astype(BF16).astype(F32)
    r1 = a - hi
    mid = r1.astype(BF16).astype(F32)
    lo = (r1 - mid).astype(BF16).astype(F32)
    return hi, mid, lo


def _dot(a, b):
    return jnp.dot(a, b, preferred_element_type=F32)


def _dot_exact_rhs(m, a):
    hi, mid, lo = _split3(a)
    mb = m.astype(BF16)
    return _dot(mb, hi.astype(BF16)) + _dot(mb, mid.astype(BF16)) + _dot(mb, lo.astype(BF16))


def _ada_kernel(cv_ref, down_ref, up_ref, bias_ref, o_ref):
    t = jnp.dot(_silu(cv_ref[...]), down_ref[...], preferred_element_type=F32,
                precision=lax.Precision.HIGHEST)
    o_ref[...] = jnp.dot(t, up_ref[...], preferred_element_type=F32,
                         precision=lax.Precision.HIGHEST) + bias_ref[...]


def _ada_modulation(cv, down, up, bias):
    depth, d, r = down.shape
    n = up.shape[-1]
    tn = _pick(n, (6144, 3072, 1536, 768, 512, 256, 128))
    rows = cv.shape[0]
    return pl.pallas_call(
        _ada_kernel,
        grid=(depth, n // tn),
        in_specs=[
            pl.BlockSpec((rows, d), lambda l, j: (0, 0)),
            pl.BlockSpec((None, d, r), lambda l, j: (l, 0, 0)),
            pl.BlockSpec((None, r, tn), lambda l, j: (l, 0, j)),
            pl.BlockSpec((None, 1, tn), lambda l, j: (l, 0, j)),
        ],
        out_specs=pl.BlockSpec((None, rows, tn), lambda l, j: (l, 0, j)),
        out_shape=jax.ShapeDtypeStruct((depth, rows, n), F32),
        compiler_params=_cparams(("arbitrary", "arbitrary")),
        name="ada_modulation",
    )(cv, down, up, bias.reshape(depth, 1, n))


def _norm_kernel(*refs, has_y, n_lat_tiles):
    if has_y:
        (xl_ref, xc_ref, yl_ref, yc_ref, g_ref, w_ref, sh_ref, sc_ref,
         xlo_ref, xco_ref, h_ref) = refs
    else:
        xl_ref, xc_ref, w_ref, sh_ref, sc_ref, h_ref = refs

    def run(x_ref, y_ref, xo_ref):
        x = x_ref[...]
        if has_y:
            x = x + g_ref[...] * y_ref[...]
            xo_ref[...] = x
        xn = x * lax.rsqrt(jnp.mean(x * x, axis=-1, keepdims=True) + EPS) * w_ref[...]
        h_ref[...] = (xn * (1.0 + sc_ref[...]) + sh_ref[...]).astype(h_ref.dtype)

    i = pl.program_id(0)

    @pl.when(i < n_lat_tiles)
    def _():
        run(xl_ref, yl_ref if has_y else None, xlo_ref if has_y else None)

    @pl.when(i >= n_lat_tiles)
    def _():
        run(xc_ref, yc_ref if has_y else None, xco_ref if has_y else None)


def _norm_modulate(cfg, x_lat, x_ctx, y, gate, w, shift, scale, *, to_col_major):
    d, tr = cfg["d"], ROW_TILE
    n_lat, n_ctx = cfg["t_lat"] // tr, cfg["t_ctx"] // tr
    per_batch = cfg["seq"] // tr
    cols = cfg["cols"]
    has_y = y is not None

    def lat_tile(i):
        return jnp.minimum(i, n_lat - 1)

    def ctx_tile(i):
        return jnp.maximum(i - n_lat, 0)

    def stream(i):
        return jnp.where(i < n_lat, i // per_batch, cfg["batch"])

    if to_col_major:
        def strided(i):
            il = lat_tile(i)
            return (il // cols, il % cols)
        xl_in = x_lat.reshape(cfg["t_lat"] // cols, cols * d)
        xl_spec = pl.BlockSpec((tr, d), strided)
    else:
        xl_in = x_lat
        xl_spec = pl.BlockSpec((tr, d), lambda i: (lat_tile(i), 0))
    xc_spec = pl.BlockSpec((tr, d), lambda i: (ctx_tile(i), 0))
    mod_spec = pl.BlockSpec((None, 1, d), lambda i: (stream(i), 0, 0))
    w_spec = pl.BlockSpec((1, d), lambda i: (0, 0))
    h_spec = pl.BlockSpec((tr, d), lambda i: (i, 0))
    h_shape = jax.ShapeDtypeStruct((cfg["t"], d), BF16)

    if has_y:
        if to_col_major:
            yl_in = y.reshape(cfg["t"] // cols, cols * d)
            yl_spec = pl.BlockSpec((tr, d), strided)
        else:
            yl_in = y
            yl_spec = pl.BlockSpec((tr, d), lambda i: (lat_tile(i), 0))
        yc_spec = pl.BlockSpec((tr, d), lambda i: (n_lat + ctx_tile(i), 0))
        ins = (xl_in, x_ctx, yl_in, y, gate, w, shift, scale)
        in_specs = [xl_spec, xc_spec, yl_spec, yc_spec, mod_spec, w_spec, mod_spec, mod_spec]
        out_specs = [pl.BlockSpec((tr, d), lambda i: (lat_tile(i), 0)), xc_spec, h_spec]
        out_shape = [jax.ShapeDtypeStruct((cfg["t_lat"], d), F32),
                     jax.ShapeDtypeStruct((cfg["t_ctx"], d), F32), h_shape]
    else:
        ins = (xl_in, x_ctx, w, shift, scale)
        in_specs = [xl_spec, xc_spec, w_spec, mod_spec, mod_spec]
        out_specs = h_spec
        out_shape = h_shape
    out = pl.pallas_call(
        functools.partial(_norm_kernel, has_y=has_y, n_lat_tiles=n_lat),
        grid=(n_lat + n_ctx,),
        in_specs=in_specs, out_specs=out_specs, out_shape=out_shape,
        compiler_params=_cparams(("arbitrary",)),
        name="norm_modulate",
    )(*ins)
    if has_y:
        return out
    return x_lat, x_ctx, out


def _final_kernel(x_ref, y_ref, g_ref, w_ref, o_ref):
    x = x_ref[...] + g_ref[...] * y_ref[...]
    o_ref[...] = x * lax.rsqrt(jnp.mean(x * x, axis=-1, keepdims=True) + EPS) * w_ref[...]


def _final_norm(cfg, x_lat, y, gate, w, *, from_col_major):
    d, tr = cfg["d"], ROW_TILE
    n_lat = cfg["t_lat"] // tr
    per_batch = cfg["seq"] // tr
    cols = cfg["cols"]
    plain = pl.BlockSpec((tr, d), lambda i: (i, 0))
    if from_col_major:
        out_spec = pl.BlockSpec((tr, d), lambda i: (i // cols, i % cols))
        out_shape = jax.ShapeDtypeStruct((cfg["t_lat"] // cols, cols * d), F32)
    else:
        out_spec, out_shape = plain, jax.ShapeDtypeStruct((cfg["t_lat"], d), F32)
    out = pl.pallas_call(
        _final_kernel,
        grid=(n_lat,),
        in_specs=[plain, plain,
                  pl.BlockSpec((None, 1, d), lambda i: (i // per_batch, 0, 0)),
                  pl.BlockSpec((1, d), lambda i: (0, 0))],
        out_specs=out_spec, out_shape=out_shape,
        compiler_params=_cparams(("arbitrary",)),
        name="final_norm",
    )(x_lat, y, gate, w)
    return out.reshape(cfg["t_lat"], d)


def _mm_kernel(a_ref, w_ref, o_ref):
    o_ref[...] = _dot(a_ref[...], w_ref[...].astype(BF16)).astype(o_ref.dtype)


def _matmul(a, w, layer, col_lo, col_hi, out_dtype, tm, tn, name):
    m, k = a.shape
    n = col_hi - col_lo
    assert m % tm == 0 and n % tn == 0 and col_lo % tn == 0
    off = col_lo // tn
    return pl.pallas_call(
        _mm_kernel,
        grid=(m // tm, n // tn),
        in_specs=[pl.BlockSpec((tm, k), lambda i, j: (i, 0)),
                  pl.BlockSpec((None, k, tn), lambda i, j: (layer, 0, off + j))],
        out_specs=pl.BlockSpec((tm, tn), lambda i, j: (i, j)),
        out_shape=jax.ShapeDtypeStruct((m, n), out_dtype),
        compiler_params=_cparams(("arbitrary", "arbitrary")),
        name=name,
    )(a, w)


def _swiglu_kernel(a_ref, wg_ref, wu_ref, o_ref):
    a = a_ref[...]
    g = _dot(a, wg_ref[...].astype(BF16))
    u = _dot(a, wu_ref[...].astype(BF16))
    o_ref[...] = (_silu(g) * u).astype(o_ref.dtype)


def _matmul_swiglu(a, w, layer, tm, tn):
    m, k = a.shape
    d_ff = w.shape[-1] // 2
    assert m % tm == 0 and d_ff % tn == 0
    nj = d_ff // tn
    return pl.pallas_call(
        _swiglu_kernel,
        grid=(m // tm, nj),
        in_specs=[pl.BlockSpec((tm, k), lambda i, j: (i, 0)),
                  pl.BlockSpec((None, k, tn), lambda i, j: (layer, 0, j)),
                  pl.BlockSpec((None, k, tn), lambda i, j: (layer, 0, nj + j))],
        out_specs=pl.BlockSpec((tm, tn), lambda i, j: (i, j)),
        out_shape=jax.ShapeDtypeStruct((m, d_ff), BF16),
        compiler_params=_cparams(("arbitrary", "arbitrary")),
        name="ffn_in_swiglu",
    )(a, w, w)


def _conv_kernel(prev_ref, cur_ref, next_ref, w_ref, b_ref, o_ref, scr_ref, out_ref, *,
                 tiles_per_lat_seq, n_lat_tiles, tiles_per_ctx_seq, n_norm_tiles, n_q_tiles, q_scale):
    i, j = pl.program_id(0), pl.program_id(1)
    ts = cur_ref.shape[0]
    ksz = w_ref.shape[0]
    is_lat = i < n_lat_tiles
    pos = jnp.where(is_lat, i % tiles_per_lat_seq, (i - n_lat_tiles) % tiles_per_ctx_seq)
    per_seq = jnp.where(is_lat, tiles_per_lat_seq, tiles_per_ctx_seq)
    keep_prev = jnp.where(pos == 0, 0.0, 1.0)
    keep_next = jnp.where(pos == per_seq - 1, 0.0, 1.0)
    if n_norm_tiles:
        normed = j < n_norm_tiles
        scale = jnp.where(j < n_q_tiles, q_scale, 1.0)
    first = HALO - ksz // 2
    n_slabs = cur_ref.shape[1] // LANES

    def one_slab(sl):
        lanes = pl.ds(pl.multiple_of(sl * LANES, LANES), LANES)
        scr_ref[sl, 0:HALO, :] = prev_ref[:, lanes].astype(F32) * keep_prev
        scr_ref[sl, HALO:HALO + ts, :] = cur_ref[:, lanes].astype(F32)
        scr_ref[sl, HALO + ts:2 * HALO + ts, :] = next_ref[:, lanes].astype(F32) * keep_next
        scr_ref[sl, 2 * HALO + ts:, :] = jnp.zeros((scr_ref.shape[1] - 2 * HALO - ts, LANES), F32)
        wk = [jnp.broadcast_to(w_ref[k:k + 1, lanes], (8, LANES)) for k in range(ksz)]
        bias = jnp.broadcast_to(b_ref[:, lanes], (8, LANES))
        taps = [scr_ref[sl, pl.ds(first + g, 8, stride=CONV_PITCH), :]
                for g in range(CONV_PITCH + ksz - 1)]
        for g in range(CONV_PITCH):
            acc = bias + wk[0] * taps[g]
            for k in range(1, ksz):
                acc = acc + wk[k] * taps[g + k]
            y = _silu(acc)
            if n_norm_tiles:
                r = lax.rsqrt(jnp.sum(y * y, axis=-1, keepdims=True) + EPS) * scale
                y = y * jnp.where(normed, r, 1.0)
            out_ref[sl, pl.ds(g, 8, stride=CONV_PITCH), :] = y
        o_ref[:, lanes] = out_ref[sl, 0:ts, :].astype(o_ref.dtype)

    @pl.loop(0, n_slabs // CONV_SLAB_UNROLL)
    def _(it):
        for u in range(CONV_SLAB_UNROLL):
            one_slab(it * CONV_SLAB_UNROLL + u)


def _conv_silu(cfg, src, col_lo, n_ch, w, b, *, norm_ch=0, q_ch=0, q_scale=1.0):
    t, ts = cfg["t"], CONV_ROWS
    tc = next(c for c in (2048, 1024, 512, 256)
              if all(v % c == 0 for v in (n_ch, col_lo, norm_ch, q_ch)))
    assert (tc // LANES) % CONV_SLAB_UNROLL == 0
    assert cfg["seq"] % ts == 0 and cfg["ctx_len"] % ts == 0 and 8 * CONV_PITCH >= ts
    ksz = w.shape[0]
    last_row = (HALO - ksz // 2) + (CONV_PITCH + ksz - 2) + 7 * CONV_PITCH
    in_rows = -(-(last_row + 1) // 8) * 8
    assert in_rows >= ts + 2 * HALO
    off = col_lo // tc
    hb = ts // HALO
    last_hb = t // HALO - 1
    kern = functools.partial(
        _conv_kernel, tiles_per_lat_seq=cfg["seq"] // ts, n_lat_tiles=cfg["t_lat"] // ts,
        tiles_per_ctx_seq=cfg["ctx_len"] // ts, n_norm_tiles=norm_ch // tc, n_q_tiles=q_ch // tc,
        q_scale=q_scale)
    return pl.pallas_call(
        kern,
        grid=(t // ts, n_ch // tc),
        in_specs=[
            pl.BlockSpec((HALO, tc), lambda i, j: (jnp.maximum(i * hb - 1, 0), off + j)),
            pl.BlockSpec((ts, tc), lambda i, j: (i, off + j)),
            pl.BlockSpec((HALO, tc), lambda i, j: (jnp.minimum((i + 1) * hb, last_hb), off + j)),
            pl.BlockSpec((w.shape[0], tc), lambda i, j: (0, j)),
            pl.BlockSpec((1, tc), lambda i, j: (0, j)),
        ],
        out_specs=pl.BlockSpec((ts, tc), lambda i, j: (i, j)),
        out_shape=jax.ShapeDtypeStruct((t, n_ch), BF16),
        scratch_shapes=[pltpu.VMEM((tc // LANES, in_rows, LANES), F32),
                        pltpu.VMEM((tc // LANES, 8 * CONV_PITCH, LANES), F32)],
        compiler_params=_cparams(("arbitrary", "arbitrary")),
        name="conv_silu",
    )(src, src, src, w, b.reshape(1, n_ch))


def _prep_kernel(tail_ref, bias_ref, amul_ref, p1c_ref, p2c_ref, p1r_ref, p2r_ref, *,
                 kind, chunk, p2_off, split_lane, hps):
    tr = tail_ref.shape[0]
    raw = tail_ref[...]
    sp = _softplus(raw + bias_ref[...])
    if kind == "ssd":
        p1 = sp
    else:
        p1 = _sigmoid(raw)
    a = sp * amul_ref[...]
    row = lax.broadcasted_iota(jnp.int32, (tr, tr), 0)
    col = lax.broadcasted_iota(jnp.int32, (tr, tr), 1)
    same = (row // chunk) == (col // chunk)
    lower = jnp.where(same & (col <= row), 1.0, 0.0)
    upper = jnp.where(same & (col >= row), 1.0, 0.0)
    lane = lax.broadcasted_iota(jnp.int32, raw.shape, 1)
    cum = jnp.where(lane < split_lane, _dot_exact_rhs(lower, a), _dot_exact_rhs(upper, a))
    p1t = p1.T
    cumt = cum.T
    for n in range(p1c_ref.shape[0]):
        p1c_ref[n] = p1[:, n * hps:(n + 1) * hps]
        p2c_ref[n] = cum[:, p2_off + n * hps:p2_off + (n + 1) * hps]
        p1r_ref[n] = p1t[n * hps:(n + 1) * hps, :]
        p2r_ref[n] = cumt[p2_off + n * hps:p2_off + (n + 1) * hps, :]


def _prep(cfg, tail, bias_full, amul_full, *, kind, chunk, n_heads, hps):
    t, tr = cfg["t"], ROW_TILE
    nblk = 2 * n_heads // hps
    p2_off = 0 if kind == "ssd" else 2 * n_heads
    split_lane = p2_off + n_heads
    kern = functools.partial(_prep_kernel, kind=kind, chunk=chunk, p2_off=p2_off,
                             split_lane=split_lane, hps=hps)
    col_spec = pl.BlockSpec((nblk, tr, hps), lambda i: (0, i, 0))
    row_spec = pl.BlockSpec((nblk, hps, tr), lambda i: (0, 0, i))
    col_shape = jax.ShapeDtypeStruct((nblk, t, hps), F32)
    row_shape = jax.ShapeDtypeStruct((nblk, hps, t), F32)
    return pl.pallas_call(
        kern,
        grid=(t // tr,),
        in_specs=[pl.BlockSpec((tr, 128), lambda i: (i, 0)),
                  pl.BlockSpec((1, 128), lambda i: (0, 0)),
                  pl.BlockSpec((1, 128), lambda i: (0, 0))],
        out_specs=[col_spec, col_spec, row_spec, row_spec],
        out_shape=[col_shape, col_shape, row_shape, row_shape],
        compiler_params=_cparams(("arbitrary",)),
        name=f"prep_{kind}",
    )(tail, bias_full, amul_full)


def _fwd_block(cfg, b, i):
    ncc, ncl = cfg["ctx_len"] // ROW_TILE, cfg["seq"] // ROW_TILE
    ctx = cfg["batch"] * ncl + b * ncc + i
    lat = b * ncl + (i - ncc)
    return jnp.where(i < ncc, ctx, lat)


def _bwd_block(cfg, b, i):
    ncc, ncl = cfg["ctx_len"] // ROW_TILE, cfg["seq"] // ROW_TILE
    ctx = cfg["batch"] * ncl + b * ncc + (ncc - 1 - i)
    lat = b * ncl + (ncl - 1 - (i - ncc))
    return jnp.where(i < ncc, ctx, lat)


def _ssd_kernel(x_ref, b_ref, c_ref, z_ref, dtc_f_ref, dtc_b_ref, cuc_f_ref, cuc_b_ref,
                dtr_f_ref, dtr_b_ref, cur_f_ref, cur_b_ref, dskip_ref, nw_ref,
                o_ref, sf_ref, sb_ref, store_ref, *, nb, ncc):
    i = pl.program_id(2)
    q = x_ref.shape[0]
    gps = dtc_f_ref.shape[0]
    hpg = dtc_f_ref.shape[-1]
    width = x_ref.shape[1] // gps
    n = b_ref.shape[1] // gps
    hd = width // hpg
    groups = range(gps)
    expand = jnp.where(lax.broadcasted_iota(jnp.int32, (hpg, width), 1) // hd
                       == lax.broadcasted_iota(jnp.int32, (hpg, width), 0), 1.0, 0.0).astype(BF16)

    def widen(a):
        hi = a.astype(BF16)
        lo = (a - hi.astype(F32)).astype(BF16)
        return _dot(hi, expand) + _dot(lo, expand)

    def cols(gg, w):
        return slice(gg * w, (gg + 1) * w)

    @pl.when(i == 0)
    def _():
        sb_ref[...] = jnp.zeros_like(sb_ref)

    @pl.when(i == nb)
    def _():
        sf_ref[...] = jnp.zeros_like(sf_ref)

    @pl.when(i < nb)
    def _():
        for gg in groups:
            cub = cuc_b_ref[gg]
            tot = cub[0:1, :]
            wide = widen(jnp.concatenate([dtc_b_ref[gg] * jnp.exp(tot - cub), jnp.exp(cub[0:8, :])], axis=0))
            wgt, dec = wide[:q], wide[q:q + 1]
            xs = (x_ref[:, cols(gg, width)].astype(F32) * wgt).astype(BF16)
            cs = lax.dot_general(b_ref[:, cols(gg, n)], xs, (((0,), (0,)), ((), ())),
                                 preferred_element_type=F32)
            store_ref[i, gg] = sb_ref[gg].astype(BF16)
            sb_ref[gg] = sb_ref[gg] * dec + cs

    @pl.when(i >= nb)
    def _():
        i2 = i - nb
        slot = jnp.where(i2 < ncc, ncc - 1 - i2, nb - 1 - (i2 - ncc))
        li = lax.broadcasted_iota(jnp.int32, (q, q), 0)
        si = lax.broadcasted_iota(jnp.int32, (q, q), 1)
        causal = si <= li
        lane = lax.broadcasted_iota(jnp.int32, (q, 2 * hd), 1)
        cb = [lax.dot_general(c_ref[:, cols(gg, n)], b_ref[:, cols(gg, n)], (((1,), (1,)), ((), ())),
                              preferred_element_type=F32) for gg in groups]
        wide = []
        for gg in groups:
            cuf = cuc_f_ref[gg]
            wide.append(widen(jnp.concatenate(
                [jnp.exp(cuf), jnp.exp(cuc_b_ref[gg]), dtc_f_ref[gg] * jnp.exp(cuf[q - 1:q, :] - cuf)], axis=0)))
        y_off = []
        for gg in groups:
            ef, eb, wf = wide[gg][:q], wide[gg][q:2 * q], wide[gg][2 * q:]
            cm = c_ref[:, cols(gg, n)]
            y_off.append(_dot(cm, sf_ref[gg].astype(BF16)) * ef + _dot(cm, store_ref[slot, gg]) * eb)
            xs = (x_ref[:, cols(gg, width)].astype(F32) * wf).astype(BF16)
            cs = lax.dot_general(b_ref[:, cols(gg, n)], xs, (((0,), (0,)), ((), ())),
                                 preferred_element_type=F32)
            sf_ref[gg] = sf_ref[gg] * ef[q - 1:q, :] + cs

        y_diag = [[] for _ in groups]
        for p in range(hpg // 2):
            for gg in groups:
                xp = x_ref[:, gg * width + p * 2 * hd:gg * width + (p + 1) * 2 * hd]
                acc = None
                for half in range(2):
                    r = 2 * p + half
                    log_dec = jnp.where(causal, cuc_f_ref[gg, :, r:r + 1] - cur_f_ref[gg, r:r + 1, :],
                                        cuc_b_ref[gg, :, r:r + 1] - cur_b_ref[gg, r:r + 1, :])
                    dt_row = jnp.where(causal, dtr_f_ref[gg, r:r + 1, :], dtr_b_ref[gg, r:r + 1, :])
                    mix = (cb[gg] * (jnp.exp(log_dec) * dt_row)).astype(BF16)
                    keep = (lane < hd) if half == 0 else (lane >= hd)
                    part = _dot(mix, jnp.where(keep, xp, jnp.zeros_like(xp)))
                    acc = part if acc is None else acc + part
                y_diag[gg].append(acc)

        for gg in groups:
            cb_diag = jnp.sum(jnp.where(si == li, cb[gg], 0.0), axis=-1, keepdims=True)
            own_b = widen(dtc_b_ref[gg] * cb_diag)
            xf = x_ref[:, cols(gg, width)].astype(F32)
            y = (dskip_ref[:, cols(gg, width)] + own_b) * xf + y_off[gg] + jnp.concatenate(y_diag[gg], axis=1)
            gated = y * _silu(z_ref[:, cols(gg, width)].astype(F32))
            o_ref[:, cols(gg, width)] = (
                gated * lax.rsqrt(jnp.mean(gated * gated, axis=-1, keepdims=True) + EPS)
                * nw_ref[:, cols(gg, width)]).astype(o_ref.dtype)


def _ssd_scan(cfg, xbc, zx, dtc, cuc, dtr, cur, d_skip, norm_w):
    t, tr, batch = cfg["t"], ROW_TILE, cfg["batch"]
    g, n = SSD_GROUPS, SSD_STATE
    gps = SSD_GROUPS_PER_STEP if g % SSD_GROUPS_PER_STEP == 0 else 1
    inner = cfg["d"]
    width = inner // g
    hpg = width // SSD_HEAD_DIM
    assert hpg % 2 == 0 and tr == SSD_CHUNK and inner % (gps * n) == 0
    ncc, ncl = cfg["ctx_len"] // tr, cfg["seq"] // tr
    nb = ncc + ncl
    b_off, c_off = inner // (gps * n), (inner + g * n) // (gps * n)

    def blk(b, i):
        return jnp.where(i < nb, _bwd_block(cfg, b, i), _fwd_block(cfg, b, i - nb))

    def oblk(b, i):
        return _fwd_block(cfg, b, jnp.maximum(i - nb, 0))

    col = lambda d: pl.BlockSpec((gps, tr, hpg), lambda b, gi, i: (d * (g // gps) + gi, blk(b, i), 0))
    row = lambda d: pl.BlockSpec((gps, hpg, tr), lambda b, gi, i: (d * (g // gps) + gi, 0, blk(b, i)))
    kern = functools.partial(_ssd_kernel, nb=nb, ncc=ncc)
    return pl.pallas_call(
        kern,
        grid=(batch, g // gps, 2 * nb),
        in_specs=[
            pl.BlockSpec((tr, gps * width), lambda b, gi, i: (blk(b, i), gi)),
            pl.BlockSpec((tr, gps * n), lambda b, gi, i: (blk(b, i), b_off + gi)),
            pl.BlockSpec((tr, gps * n), lambda b, gi, i: (blk(b, i), c_off + gi)),
            pl.BlockSpec((tr, gps * width), lambda b, gi, i: (oblk(b, i), gi)),
            col(0), col(1), col(0), col(1), row(0), row(1), row(0), row(1),
            pl.BlockSpec((1, gps * width), lambda b, gi, i: (0, gi)),
            pl.BlockSpec((1, gps * width), lambda b, gi, i: (0, gi)),
        ],
        out_specs=pl.BlockSpec((tr, gps * width), lambda b, gi, i: (oblk(b, i), gi)),
        out_shape=jax.ShapeDtypeStruct((t, inner), BF16),
        scratch_shapes=[pltpu.VMEM((gps, n, width), F32), pltpu.VMEM((gps, n, width), F32),
                        pltpu.VMEM((nb, gps, n, width), BF16)],
        compiler_params=_cparams(("arbitrary", "arbitrary", "arbitrary")),
        name="ssd_scan",
    )(xbc, xbc, xbc, zx, dtc, dtc, cuc, cuc, dtr, dtr, cur, cur, d_skip, norm_w)


def _gdn_kernel(*refs, rev, finish):
    if finish:
        (q_ref, k_ref, v_ref, bc_ref, gc_ref, gr_ref, prev_ref, z_ref, nw_ref,
         o_ref, s_ref) = refs
    else:
        q_ref, k_ref, v_ref, bc_ref, gc_ref, gr_ref, o_ref, s_ref = refs
    c = GDN_CHUNK
    hw = GDN_HEAD
    n_heads = s_ref.shape[0]
    rep = n_heads // (k_ref.shape[1] // hw)

    @pl.when(pl.program_id(2) == 0)
    def _():
        s_ref[...] = jnp.zeros_like(s_ref)

    ii = lax.broadcasted_iota(jnp.int32, (c, c), 0)
    jj = lax.broadcasted_iota(jnp.int32, (c, c), 1)
    incl = (jj >= ii) if rev else (jj <= ii)
    strict = (jj > ii) if rev else (jj < ii)
    eye = jnp.where(ii == jj, 1.0, 0.0)
    n_chunks = q_ref.shape[0] // c
    order = range(n_chunks - 1, -1, -1) if rev else range(n_chunks)
    last = 0 if rev else c - 1

    heads = range(n_heads)
    chunk_rows = {ci: slice(ci * c, (ci + 1) * c) for ci in order}
    items = [(ci, h) for ci in order for h in heads]

    qk, kk = {}, {}
    for ci in order:
        rows = chunk_rows[ci]
        for kh in range(n_heads // rep):
            k_in = k_ref[rows, kh * hw:(kh + 1) * hw]
            both = lax.dot_general(jnp.concatenate([q_ref[rows, kh * hw:(kh + 1) * hw], k_in], axis=0),
                                   k_in, (((1,), (1,)), ((), ())), preferred_element_type=F32)
            qk[ci, kh], kk[ci, kh] = both[:c], both[c:]

    inv, pw, qkd = {}, {}, {}
    for ci, h in items:
        rows = chunk_rows[ci]
        gc1, gr1 = gc_ref[rows, h:h + 1], gr_ref[h:h + 1, rows]
        dec = jnp.where(incl, jnp.exp(jnp.where(incl, gc1 - gr1, 0.0)), 0.0)
        a = jnp.where(strict, bc_ref[rows, h:h + 1] * kk[ci, h // rep] * dec, 0.0)
        inv[ci, h] = eye - a
        pw[ci, h] = a.astype(BF16)
        qkd[ci, h] = (qk[ci, h // rep] * dec).astype(BF16)
    for it in items:
        pw[it] = _dot(pw[it], pw[it]).astype(BF16)
    steps = c.bit_length() - 1
    for s in range(1, steps):
        if s < steps - 1:
            both = {it: _dot(jnp.concatenate([inv[it].astype(BF16), pw[it]], axis=0), pw[it])
                    for it in items}
            for it in items:
                inv[it] = inv[it] + both[it][:c]
                pw[it] = both[it][c:].astype(BF16)
        else:
            upd = {it: _dot(inv[it].astype(BF16), pw[it]) for it in items}
            for it in items:
                inv[it] = inv[it] + upd[it]

    uw = {}
    for ci, h in items:
        rows = chunk_rows[ci]
        bc1, gc1 = bc_ref[rows, h:h + 1], gc_ref[rows, h:h + 1]
        kh = h // rep
        rhs = jnp.concatenate(
            [(v_ref[rows, h * hw:(h + 1) * hw].astype(F32) * bc1).astype(BF16),
             (k_ref[rows, kh * hw:(kh + 1) * hw].astype(F32) * (bc1 * jnp.exp(gc1))).astype(BF16)], axis=1)
        uw[ci, h] = _dot(inv[ci, h].astype(BF16), rhs)

    for ci in order:
        rows = chunk_rows[ci]
        st = {h: s_ref[h] for h in heads}
        ws_qs = {h: _dot(jnp.concatenate([uw[ci, h][:, hw:].astype(BF16),
                                          q_ref[rows, (h // rep) * hw:(h // rep + 1) * hw]], axis=0),
                         st[h].astype(BF16)) for h in heads}
        v_new = {h: (uw[ci, h][:, :hw] - ws_qs[h][:c]).astype(BF16) for h in heads}
        for h in heads:
            gc1 = gc_ref[rows, h:h + 1]
            gtot = gc1[last:last + 1, :]
            kh = h // rep
            k_dec = (k_ref[rows, kh * hw:(kh + 1) * hw].astype(F32) * jnp.exp(gtot - gc1)).astype(BF16)
            s_ref[h] = st[h] * jnp.exp(gtot) + lax.dot_general(
                k_dec, v_new[h], (((0,), (0,)), ((), ())), preferred_element_type=F32)
        for h in heads:
            out = jnp.exp(gc_ref[rows, h:h + 1]) * ws_qs[h][c:] + _dot(qkd[ci, h], v_new[h])
            cols = slice(h * hw, (h + 1) * hw)
            if finish:
                o = out + prev_ref[rows, cols]
                o = o * lax.rsqrt(jnp.mean(o * o, axis=-1, keepdims=True) + EPS) * nw_ref[...]
                o_ref[rows, cols] = (o * _silu(z_ref[rows, cols].astype(F32))).astype(o_ref.dtype)
            else:
                o_ref[rows, cols] = out


def _gdn_scan(cfg, qkv, bc, gc, gr, *, rev, prev=None, qkvz=None, norm_w=None):
    t, tr, batch = cfg["t"], ROW_TILE, cfg["batch"]
    hw = GDN_HEAD
    hv = cfg["d"] // hw
    hk = hv // 2
    hps = min(GDN_HEADS_PER_STEP, hv)
    kps = hps // 2
    nhb = hv // hps
    nb = (cfg["ctx_len"] + cfg["seq"]) // tr
    finish = prev is not None
    d = 1 if rev else 0
    order = _bwd_block if rev else _fwd_block

    def blk(b, i):
        return order(cfg, b, i)

    k_off = hk // kps
    v_off = 2 * hk // hps
    z_off = (2 * hk + hv) // hps
    col = pl.BlockSpec((None, tr, hps), lambda b, hb, i: (d * nhb + hb, blk(b, i), 0))
    row = pl.BlockSpec((None, hps, tr), lambda b, hb, i: (d * nhb + hb, 0, blk(b, i)))
    wide = lambda off: pl.BlockSpec((tr, hps * hw), lambda b, hb, i: (blk(b, i), off + hb))
    in_specs = [
        pl.BlockSpec((tr, kps * hw), lambda b, hb, i: (blk(b, i), hb)),
        pl.BlockSpec((tr, kps * hw), lambda b, hb, i: (blk(b, i), k_off + hb)),
        wide(v_off), col, col, row,
    ]
    ins = [qkv, qkv, qkv, bc, gc, gr]
    if finish:
        in_specs += [wide(0), wide(z_off), pl.BlockSpec((1, hw), lambda b, hb, i: (0, 0))]
        ins += [prev, qkvz, norm_w]
    return pl.pallas_call(
        functools.partial(_gdn_kernel, rev=rev, finish=finish),
        grid=(batch, nhb, nb),
        in_specs=in_specs,
        out_specs=wide(0),
        out_shape=jax.ShapeDtypeStruct((t, hv * hw), BF16 if finish else F32),
        scratch_shapes=[pltpu.VMEM((hps, hw, hw), F32)],
        compiler_params=_cparams(("arbitrary", "arbitrary", "arbitrary")),
        name="gdn_scan_bwd" if rev else "gdn_scan_fwd",
    )(*ins)


def _pad_lanes(v, width=128):
    v = v.reshape(1, -1).astype(F32)
    return jnp.pad(v, ((0, 0), (0, width - v.shape[1])))


def _pad_cols(w, width=128):
    return jnp.pad(w, ((0, 0), (0, width - w.shape[1])))


def _tail_weights(w, layer, main):
    if w.shape[-1] - main == LANES:
        return w, layer, main
    return _pad_cols(w[layer][:, main:])[None], 0, 0


def _ssd_mixer(cfg, h, layer, w_in, conv_w, conv_b, a_log, dt_bias, d_skip, norm_w, w_out):
    d = cfg["d"]
    heads = d // SSD_HEAD_DIM
    bc_dim = SSD_GROUPS * SSD_STATE
    main = 2 * d + 2 * bc_dim
    tm = cfg["tm"]
    zx = _matmul(h, w_in, layer, 0, main, BF16, tm, _pick(main, (512, 256, 128)), "ssd_in_proj")
    w_tail, tail_layer, tail_lo = _tail_weights(w_in, layer, main)
    tail = _matmul(h, w_tail, tail_layer, tail_lo, tail_lo + LANES, F32, tm, LANES, "ssd_in_proj_dt")
    xbc = _conv_silu(cfg, zx, d, d + 2 * bc_dim, conv_w, conv_b)
    a_neg = -jnp.exp(a_log.astype(F32))
    dtc, cuc, dtr, cur = _prep(cfg, tail, _pad_lanes(dt_bias), _pad_lanes(a_neg),
                               kind="ssd", chunk=SSD_CHUNK, n_heads=heads, hps=heads // SSD_GROUPS)
    y = _ssd_scan(cfg, xbc, zx, dtc, cuc, dtr, cur,
                  jnp.repeat(d_skip.astype(F32), SSD_HEAD_DIM).reshape(1, d), norm_w.reshape(1, d))
    return _matmul(y, w_out, layer, 0, d, BF16, tm, _pick(d, (512, 256, 128)), "ssd_out_proj")


def _gdn_mixer(cfg, h, layer, w_in, conv_w, a_log, dt_bias, norm_w, w_out):
    d = cfg["d"]
    hv = d // GDN_HEAD
    key_dim = (hv // 2) * GDN_HEAD
    conv_dim = 2 * key_dim + d
    main = conv_dim + d
    tm = cfg["tm"]
    qkvz = _matmul(h, w_in, layer, 0, main, BF16, tm, _pick(main, (512, 256, 128)), "gdn_in_proj")
    w_tail, tail_layer, tail_lo = _tail_weights(w_in, layer, main)
    tail = _matmul(h, w_tail, tail_layer, tail_lo, tail_lo + LANES, F32, tm, LANES, "gdn_in_proj_gates")
    qkv = _conv_silu(cfg, qkvz, 0, conv_dim, conv_w, jnp.zeros((conv_dim,), F32),
                     norm_ch=2 * key_dim, q_ch=key_dim, q_scale=GDN_HEAD ** -0.5)
    a_pos = jnp.exp(a_log.astype(F32)).reshape(-1)
    zeros = jnp.zeros((2 * hv,), F32)
    bias_full = _pad_lanes(jnp.concatenate([zeros, dt_bias.astype(F32).reshape(-1)]))
    amul_full = _pad_lanes(jnp.concatenate([zeros, -a_pos]))
    hps = min(GDN_HEADS_PER_STEP, hv)
    bc, gc, _, gr = _prep(cfg, tail, bias_full, amul_full, kind="gdn", chunk=GDN_CHUNK,
                           n_heads=hv, hps=hps)
    o_f = _gdn_scan(cfg, qkv, bc, gc, gr, rev=False)
    y = _gdn_scan(cfg, qkv, bc, gc, gr, rev=True, prev=o_f, qkvz=qkvz,
                  norm_w=norm_w.reshape(1, GDN_HEAD).astype(F32))
    return _matmul(y, w_out, layer, 0, d, BF16, tm, _pick(d, (512, 256, 128)), "gdn_out_proj")


def _ffn(cfg, h, layer, w_in, w_out_bf):
    d = cfg["d"]
    d_ff = w_in.shape[-1] // 2
    hid = _matmul_swiglu(h, w_in, layer, cfg["tm"], _pick(d_ff, (256, 128)))
    return _matmul(hid, w_out_bf, layer, 0, d, BF16, cfg["tm_out"], _pick(d, (512, 256, 128)), "ffn_out")


def kernel(x, c, ctx, c_ctx, ada_down, ada_up, ada_bias, norm_mix, norm_ffn, ffn_in, ffn_out,
           ssd_in, ssd_conv_w, ssd_conv_b, ssd_a_log, ssd_dt_bias, ssd_d, ssd_norm, ssd_out,
           gdn_in, gdn_conv_w, gdn_a_log, gdn_dt_bias, gdn_norm, gdn_out, final_norm):
    batch, seq, d = x.shape
    ctx_len = ctx.shape[1]
    depth = ada_down.shape[0]
    rows = seq // GRID_W
    assert rows == ROW_TILE and ctx_len % CONV_ROWS == 0 and seq % CONV_ROWS == 0
    t_lat, t_ctx = batch * seq, batch * ctx_len
    t = t_lat + t_ctx
    cfg = dict(d=d, batch=batch, seq=seq, ctx_len=ctx_len, cols=GRID_W, t_lat=t_lat, t_ctx=t_ctx, t=t,
               tm=_pick(t, (1536, 1024, 512, 256, 128)), tm_out=_pick(t, (512, 256, 128)))
    assert t_ctx % GRID_W == 0

    n_streams = batch + 1
    cv = jnp.concatenate([c.astype(F32), c_ctx.astype(F32)[None, :],
                          jnp.zeros((8 - n_streams % 8 if n_streams % 8 else 0, d), F32)], axis=0)
    mods = _ada_modulation(cv, ada_down, ada_up, ada_bias)

    def mod(layer, which):
        return mods[layer, :n_streams, which * d:(which + 1) * d].reshape(n_streams, 1, d)

    ffn_out_bf = ffn_out.astype(BF16)

    x_lat = x.reshape(t_lat, d)
    x_ctx = ctx.reshape(t_ctx, d)
    pending, pending_gate = None, None
    col_major = False
    for i in range(depth):
        want_col_major = (i // N_MIXERS) % 2 == 1
        assert want_col_major or not col_major
        j = i // N_MIXERS
        x_lat, x_ctx, h = _norm_modulate(cfg, x_lat, x_ctx, pending, pending_gate,
                                         norm_mix[i].reshape(1, d), mod(i, 0), mod(i, 1),
                                         to_col_major=want_col_major and not col_major)
        col_major = want_col_major
        if i % N_MIXERS == 0:
            y = _ssd_mixer(cfg, h, j, ssd_in, ssd_conv_w[j], ssd_conv_b[j], ssd_a_log[j],
                           ssd_dt_bias[j], ssd_d[j], ssd_norm[j], ssd_out)
        else:
            y = _gdn_mixer(cfg, h, j, gdn_in, gdn_conv_w[j], gdn_a_log[j], gdn_dt_bias[j],
                           gdn_norm[j], gdn_out)
        x_lat, x_ctx, h = _norm_modulate(cfg, x_lat, x_ctx, y, mod(i, 2),
                                         norm_ffn[i].reshape(1, d), mod(i, 3), mod(i, 4),
                                         to_col_major=False)
        pending, pending_gate = _ffn(cfg, h, i, ffn_in, ffn_out_bf), mod(i, 5)
    out = _final_norm(cfg, x_lat, pending, pending_gate, final_norm.reshape(1, d),
                      from_col_major=col_major)
    return out.reshape(batch, seq, d)
```

```python
import functools

import jax
import jax.numpy as jnp
from jax import lax
from jax.experimental import pallas as pl
from jax.experimental.pallas import tpu as pltpu

F32 = jnp.float32
BF16 = jnp.bfloat16
EPS = 1e-6

GRID_W = 64
N_MIXERS = 2
N_MOD = 6
SSD_HEAD_DIM = 64
SSD_GROUPS = 8
SSD_STATE = 128
SSD_CHUNK = 128
SSD_GROUPS_PER_STEP = 4
GDN_HEAD = 128
GDN_CHUNK = 64
GDN_HEADS_PER_STEP = 8

ROW_TILE = 128
CONV_ROWS = 256
HALO = 8
LANES = 128
CONV_PITCH = 36
CONV_SLAB_UNROLL = 2
V7X_VMEM_LIMIT = 56 * 1024 * 1024


def _cparams(sem, vmem=None):
    return pltpu.CompilerParams(dimension_semantics=sem, vmem_limit_bytes=vmem or V7X_VMEM_LIMIT)


def _sigmoid(x):
    return 1.0 / (1.0 + jnp.exp(-x))


def _silu(x):
    h = 0.5 * x
    return h + h * jnp.tanh(h)


def _softplus(x):
    return jnp.maximum(x, 0.0) + jnp.log1p(jnp.exp(-jnp.abs(x)))


def _pick(n, candidates):
    for c in candidates:
        if n % c == 0:
            return c
    raise ValueError(f"no tile for {n} among {candidates}")


def _split3(a):
    hi = a.astype(BF16).astype(F32)
    r1 = a - hi
    mid = r1.astype(BF16).astype(F32)
    lo = (r1 - mid).astype(BF16).astype(F32)
    return hi, mid, lo


def _dot(a, b):
    return jnp.dot(a, b, preferred_element_type=F32)


def _dot_exact_rhs(m, a):
    hi, mid, lo = _split3(a)
    mb = m.astype(BF16)
    return _dot(mb, hi.astype(BF16)) + _dot(mb, mid.astype(BF16)) + _dot(mb, lo.astype(BF16))


def _ada_kernel(cv_ref, down_ref, up_ref, bias_ref, o_ref):
    t = jnp.dot(_silu(cv_ref[...]), down_ref[...], preferred_element_type=F32,
                precision=lax.Precision.HIGHEST)
    o_ref[...] = jnp.dot(t, up_ref[...], preferred_element_type=F32,
                         precision=lax.Precision.HIGHEST) + bias_ref[...]


def _ada_modulation(cv, down, up, bias):
    depth, d, r = down.shape
    n = up.shape[-1]
    tn = _pick(n, (6144, 3072, 1536, 768, 512, 256, 128))
    rows = cv.shape[0]
    return pl.pallas_call(
        _ada_kernel,
        grid=(depth, n // tn),
        in_specs=[
            pl.BlockSpec((rows, d), lambda l, j: (0, 0)),
            pl.BlockSpec((None, d, r), lambda l, j: (l, 0, 0)),
            pl.BlockSpec((None, r, tn), lambda l, j: (l, 0, j)),
            pl.BlockSpec((None, 1, tn), lambda l, j: (l, 0, j)),
        ],
        out_specs=pl.BlockSpec((None, rows, tn), lambda l, j: (l, 0, j)),
        out_shape=jax.ShapeDtypeStruct((depth, rows, n), F32),
        compiler_params=_cparams(("arbitrary", "arbitrary")),
        name="ada_modulation",
    )(cv, down, up, bias.reshape(depth, 1, n))


def _norm_kernel(*refs, has_y, n_lat_tiles):
    if has_y:
        (xl_ref, xc_ref, yl_ref, yc_ref, g_ref, w_ref, sh_ref, sc_ref,
         xlo_ref, xco_ref, h_ref) = refs
    else:
        xl_ref, xc_ref, w_ref, sh_ref, sc_ref, h_ref = refs

    def run(x_ref, y_ref, xo_ref):
        x = x_ref[...]
        if has_y:
            x = x + g_ref[...] * y_ref[...]
            xo_ref[...] = x
        xn = x * lax.rsqrt(jnp.mean(x * x, axis=-1, keepdims=True) + EPS) * w_ref[...]
        h_ref[...] = (xn * (1.0 + sc_ref[...]) + sh_ref[...]).astype(h_ref.dtype)

    i = pl.program_id(0)

    @pl.when(i < n_lat_tiles)
    def _():
        run(xl_ref, yl_ref if has_y else None, xlo_ref if has_y else None)

    @pl.when(i >= n_lat_tiles)
    def _():
        run(xc_ref, yc_ref if has_y else None, xco_ref if has_y else None)


def _norm_modulate(cfg, x_lat, x_ctx, y, gate, w, shift, scale, *, to_col_major):
    d, tr = cfg["d"], ROW_TILE
    n_lat, n_ctx = cfg["t_lat"] // tr, cfg["t_ctx"] // tr
    per_batch = cfg["seq"] // tr
    cols = cfg["cols"]
    has_y = y is not None

    def lat_tile(i):
        return jnp.minimum(i, n_lat - 1)

    def ctx_tile(i):
        return jnp.maximum(i - n_lat, 0)

    def stream(i):
        return jnp.where(i < n_lat, i // per_batch, cfg["batch"])

    if to_col_major:
        def strided(i):
            il = lat_tile(i)
            return (il // cols, il % cols)
        xl_in = x_lat.reshape(cfg["t_lat"] // cols, cols * d)
        xl_spec = pl.BlockSpec((tr, d), strided)
    else:
        xl_in = x_lat
        xl_spec = pl.BlockSpec((tr, d), lambda i: (lat_tile(i), 0))
    xc_spec = pl.BlockSpec((tr, d), lambda i: (ctx_tile(i), 0))
    mod_spec = pl.BlockSpec((None, 1, d), lambda i: (stream(i), 0, 0))
    w_spec = pl.BlockSpec((1, d), lambda i: (0, 0))
    h_spec = pl.BlockSpec((tr, d), lambda i: (i, 0))
    h_shape = jax.ShapeDtypeStruct((cfg["t"], d), BF16)

    if has_y:
        if to_col_major:
            yl_in = y.reshape(cfg["t"] // cols, cols * d)
            yl_spec = pl.BlockSpec((tr, d), strided)
        else:
            yl_in = y
            yl_spec = pl.BlockSpec((tr, d), lambda i: (lat_tile(i), 0))
        yc_spec = pl.BlockSpec((tr, d), lambda i: (n_lat + ctx_tile(i), 0))
        ins = (xl_in, x_ctx, yl_in, y, gate, w, shift, scale)
        in_specs = [xl_spec, xc_spec, yl_spec, yc_spec, mod_spec, w_spec, mod_spec, mod_spec]
        out_specs = [pl.BlockSpec((tr, d), lambda i: (lat_tile(i), 0)), xc_spec, h_spec]
        out_shape = [jax.ShapeDtypeStruct((cfg["t_lat"], d), F32),
                     jax.ShapeDtypeStruct((cfg["t_ctx"], d), F32), h_shape]
    else:
        ins = (xl_in, x_ctx, w, shift, scale)
        in_specs = [xl_spec, xc_spec, w_spec, mod_spec, mod_spec]
        out_specs = h_spec
        out_shape = h_shape
    out = pl.pallas_call(
        functools.partial(_norm_kernel, has_y=has_y, n_lat_tiles=n_lat),
        grid=(n_lat + n_ctx,),
        in_specs=in_specs, out_specs=out_specs, out_shape=out_shape,
        compiler_params=_cparams(("arbitrary",)),
        name="norm_modulate",
    )(*ins)
    if has_y:
        return out
    return x_lat, x_ctx, out


def _final_kernel(x_ref, y_ref, g_ref, w_ref, o_ref):
    x = x_ref[...] + g_ref[...] * y_ref[...]
    o_ref[...] = x * lax.rsqrt(jnp.mean(x * x, axis=-1, keepdims=True) + EPS) * w_ref[...]


def _final_norm(cfg, x_lat, y, gate, w, *, from_col_major):
    d, tr = cfg["d"], ROW_TILE
    n_lat = cfg["t_lat"] // tr
    per_batch = cfg["seq"] // tr
    cols = cfg["cols"]
    plain = pl.BlockSpec((tr, d), lambda i: (i, 0))
    if from_col_major:
        out_spec = pl.BlockSpec((tr, d), lambda i: (i // cols, i % cols))
        out_shape = jax.ShapeDtypeStruct((cfg["t_lat"] // cols, cols * d), F32)
    else:
        out_spec, out_shape = plain, jax.ShapeDtypeStruct((cfg["t_lat"], d), F32)
    out = pl.pallas_call(
        _final_kernel,
        grid=(n_lat,),
        in_specs=[plain, plain,
                  pl.BlockSpec((None, 1, d), lambda i: (i // per_batch, 0, 0)),
                  pl.BlockSpec((1, d), lambda i: (0, 0))],
        out_specs=out_spec, out_shape=out_shape,
        compiler_params=_cparams(("arbitrary",)),
        name="final_norm",
    )(x_lat, y, gate, w)
    return out.reshape(cfg["t_lat"], d)


def _mm_kernel(a_ref, w_ref, o_ref):
    o_ref[...] = _dot(a_ref[...], w_ref[...].astype(BF16)).astype(o_ref.dtype)


def _matmul(a, w, layer, col_lo, col_hi, out_dtype, tm, tn, name):
    m, k = a.shape
    n = col_hi - col_lo
    assert m % tm == 0 and n % tn == 0 and col_lo % tn == 0
    off = col_lo // tn
    return pl.pallas_call(
        _mm_kernel,
        grid=(m // tm, n // tn),
        in_specs=[pl.BlockSpec((tm, k), lambda i, j: (i, 0)),
                  pl.BlockSpec((None, k, tn), lambda i, j: (layer, 0, off + j))],
        out_specs=pl.BlockSpec((tm, tn), lambda i, j: (i, j)),
        out_shape=jax.ShapeDtypeStruct((m, n), out_dtype),
        compiler_params=_cparams(("arbitrary", "arbitrary")),
        name=name,
    )(a, w)


def _swiglu_kernel(a_ref, wg_ref, wu_ref, o_ref):
    a = a_ref[...]
    g = _dot(a, wg_ref[...].astype(BF16))
    u = _dot(a, wu_ref[...].astype(BF16))
    o_ref[...] = (_silu(g) * u).astype(o_ref.dtype)


def _matmul_swiglu(a, w, layer, tm, tn):
    m, k = a.shape
    d_ff = w.shape[-1] // 2
    assert m % tm == 0 and d_ff % tn == 0
    nj = d_ff // tn
    return pl.pallas_call(
        _swiglu_kernel,
        grid=(m // tm, nj),
        in_specs=[pl.BlockSpec((tm, k), lambda i, j: (i, 0)),
                  pl.BlockSpec((None, k, tn), lambda i, j: (layer, 0, j)),
                  pl.BlockSpec((None, k, tn), lambda i, j: (layer, 0, nj + j))],
        out_specs=pl.BlockSpec((tm, tn), lambda i, j: (i, j)),
        out_shape=jax.ShapeDtypeStruct((m, d_ff), BF16),
        compiler_params=_cparams(("arbitrary", "arbitrary")),
        name="ffn_in_swiglu",
    )(a, w, w)


def _conv_kernel(prev_ref, cur_ref, next_ref, w_ref, b_ref, o_ref, scr_ref, out_ref, *,
                 tiles_per_lat_seq, n_lat_tiles, tiles_per_ctx_seq, n_norm_tiles, n_q_tiles, q_scale):
    i, j = pl.program_id(0), pl.program_id(1)
    ts = cur_ref.shape[0]
    ksz = w_ref.shape[0]
    is_lat = i < n_lat_tiles
    pos = jnp.where(is_lat, i % tiles_per_lat_seq, (i - n_lat_tiles) % tiles_per_ctx_seq)
    per_seq = jnp.where(is_lat, tiles_per_lat_seq, tiles_per_ctx_seq)
    keep_prev = jnp.where(pos == 0, 0.0, 1.0)
    keep_next = jnp.where(pos == per_seq - 1, 0.0, 1.0)
    if n_norm_tiles:
        normed = j < n_norm_tiles
        scale = jnp.where(j < n_q_tiles, q_scale, 1.0)
    first = HALO - ksz // 2
    n_slabs = cur_ref.shape[1] // LANES

    def one_slab(sl):
        lanes = pl.ds(pl.multiple_of(sl * LANES, LANES), LANES)
        scr_ref[sl, 0:HALO, :] = prev_ref[:, lanes].astype(F32) * keep_prev
        scr_ref[sl, HALO:HALO + ts, :] = cur_ref[:, lanes].astype(F32)
        scr_ref[sl, HALO + ts:2 * HALO + ts, :] = next_ref[:, lanes].astype(F32) * keep_next
        scr_ref[sl, 2 * HALO + ts:, :] = jnp.zeros((scr_ref.shape[1] - 2 * HALO - ts, LANES), F32)
        wk = [jnp.broadcast_to(w_ref[k:k + 1, lanes], (8, LANES)) for k in range(ksz)]
        bias = jnp.broadcast_to(b_ref[:, lanes], (8, LANES))
        taps = [scr_ref[sl, pl.ds(first + g, 8, stride=CONV_PITCH), :]
                for g in range(CONV_PITCH + ksz - 1)]
        for g in range(CONV_PITCH):
            acc = bias + wk[0] * taps[g]
            for k in range(1, ksz):
                acc = acc + wk[k] * taps[g + k]
            y = _silu(acc)
            if n_norm_tiles:
                r = lax.rsqrt(jnp.sum(y * y, axis=-1, keepdims=True) + EPS) * scale
                y = y * jnp.where(normed, r, 1.0)
            out_ref[sl, pl.ds(g, 8, stride=CONV_PITCH), :] = y
        o_ref[:, lanes] = out_ref[sl, 0:ts, :].astype(o_ref.dtype)

    @pl.loop(0, n_slabs // CONV_SLAB_UNROLL)
    def _(it):
        for u in range(CONV_SLAB_UNROLL):
            one_slab(it * CONV_SLAB_UNROLL + u)


def _conv_silu(cfg, src, col_lo, n_ch, w, b, *, norm_ch=0, q_ch=0, q_scale=1.0):
    t, ts = cfg["t"], CONV_ROWS
    tc = next(c for c in (2048, 1024, 512, 256)
              if all(v % c == 0 for v in (n_ch, col_lo, norm_ch, q_ch)))
    assert (tc // LANES) % CONV_SLAB_UNROLL == 0
    assert cfg["seq"] % ts == 0 and cfg["ctx_len"] % ts == 0 and 8 * CONV_PITCH >= ts
    ksz = w.shape[0]
    last_row = (HALO - ksz // 2) + (CONV_PITCH + ksz - 2) + 7 * CONV_PITCH
    in_rows = -(-(last_row + 1) // 8) * 8
    assert in_rows >= ts + 2 * HALO
    off = col_lo // tc
    hb = ts // HALO
    last_hb = t // HALO - 1
    kern = functools.partial(
        _conv_kernel, tiles_per_lat_seq=cfg["seq"] // ts, n_lat_tiles=cfg["t_lat"] // ts,
        tiles_per_ctx_seq=cfg["ctx_len"] // ts, n_norm_tiles=norm_ch // tc, n_q_tiles=q_ch // tc,
        q_scale=q_scale)
    return pl.pallas_call(
        kern,
        grid=(t // ts, n_ch // tc),
        in_specs=[
            pl.BlockSpec((HALO, tc), lambda i, j: (jnp.maximum(i * hb - 1, 0), off + j)),
            pl.BlockSpec((ts, tc), lambda i, j: (i, off + j)),
            pl.BlockSpec((HALO, tc), lambda i, j: (jnp.minimum((i + 1) * hb, last_hb), off + j)),
            pl.BlockSpec((w.shape[0], tc), lambda i, j: (0, j)),
            pl.BlockSpec((1, tc), lambda i, j: (0, j)),
        ],
        out_specs=pl.BlockSpec((ts, tc), lambda i, j: (i, j)),
        out_shape=jax.ShapeDtypeStruct((t, n_ch), BF16),
        scratch_shapes=[pltpu.VMEM((tc // LANES, in_rows, LANES), F32),
                        pltpu.VMEM((tc // LANES, 8 * CONV_PITCH, LANES), F32)],
        compiler_params=_cparams(("arbitrary", "arbitrary")),
        name="conv_silu",
    )(src, src, src, w, b.reshape(1, n_ch))


def _prep_kernel(tail_ref, bias_ref, amul_ref, p1c_ref, p2c_ref, p1r_ref, p2r_ref, *,
                 kind, chunk, p2_off, split_lane, hps):
    tr = ROW_TILE
    row = lax.broadcasted_iota(jnp.int32, (tr, tr), 0)
    col = lax.broadcasted_iota(jnp.int32, (tr, tr), 1)
    same = (row // chunk) == (col // chunk)
    lower = jnp.where(same & (col <= row), 1.0, 0.0)
    upper = jnp.where(same & (col >= row), 1.0, 0.0)
    lane = lax.broadcasted_iota(jnp.int32, (tr, LANES), 1)
    for sub in range(tail_ref.shape[0] // tr):
        rows = slice(sub * tr, (sub + 1) * tr)
        raw = tail_ref[rows, :]
        sp = _softplus(raw + bias_ref[...])
        p1 = sp if kind == "ssd" else _sigmoid(raw)
        a = sp * amul_ref[...]
        cum = jnp.where(lane < split_lane, _dot_exact_rhs(lower, a), _dot_exact_rhs(upper, a))
        p1t = p1.T
        cumt = cum.T
        for n in range(p1c_ref.shape[0]):
            p1c_ref[n, rows, :] = p1[:, n * hps:(n + 1) * hps]
            p2c_ref[n, rows, :] = cum[:, p2_off + n * hps:p2_off + (n + 1) * hps]
            p1r_ref[n, :, rows] = p1t[n * hps:(n + 1) * hps, :]
            p2r_ref[n, :, rows] = cumt[p2_off + n * hps:p2_off + (n + 1) * hps, :]


def _prep(cfg, tail, bias_full, amul_full, *, kind, chunk, n_heads, hps):
    t = cfg["t"]
    tr = _pick(t, (4 * ROW_TILE, 2 * ROW_TILE, ROW_TILE))
    nblk = 2 * n_heads // hps
    p2_off = 0 if kind == "ssd" else 2 * n_heads
    split_lane = p2_off + n_heads
    kern = functools.partial(_prep_kernel, kind=kind, chunk=chunk, p2_off=p2_off,
                             split_lane=split_lane, hps=hps)
    col_spec = pl.BlockSpec((nblk, tr, hps), lambda i: (0, i, 0))
    row_spec = pl.BlockSpec((nblk, hps, tr), lambda i: (0, 0, i))
    col_shape = jax.ShapeDtypeStruct((nblk, t, hps), F32)
    row_shape = jax.ShapeDtypeStruct((nblk, hps, t), F32)
    return pl.pallas_call(
        kern,
        grid=(t // tr,),
        in_specs=[pl.BlockSpec((tr, 128), lambda i: (i, 0)),
                  pl.BlockSpec((1, 128), lambda i: (0, 0)),
                  pl.BlockSpec((1, 128), lambda i: (0, 0))],
        out_specs=[col_spec, col_spec, row_spec, row_spec],
        out_shape=[col_shape, col_shape, row_shape, row_shape],
        compiler_params=_cparams(("arbitrary",)),
        name=f"prep_{kind}",
    )(tail, bias_full, amul_full)


def _fwd_block(cfg, b, i):
    ncc, ncl = cfg["ctx_len"] // ROW_TILE, cfg["seq"] // ROW_TILE
    ctx = cfg["batch"] * ncl + b * ncc + i
    lat = b * ncl + (i - ncc)
    return jnp.where(i < ncc, ctx, lat)


def _bwd_block(cfg, b, i):
    ncc, ncl = cfg["ctx_len"] // ROW_TILE, cfg["seq"] // ROW_TILE
    ctx = cfg["batch"] * ncl + b * ncc + (ncc - 1 - i)
    lat = b * ncl + (ncl - 1 - (i - ncc))
    return jnp.where(i < ncc, ctx, lat)


def _ssd_kernel(x_ref, b_ref, c_ref, z_ref, dtc_f_ref, dtc_b_ref, cuc_f_ref, cuc_b_ref,
                dtr_f_ref, dtr_b_ref, cur_f_ref, cur_b_ref, dskip_ref, nw_ref,
                o_ref, sf_ref, sb_ref, store_ref, *, nb, ncc):
    i = pl.program_id(2)
    q = x_ref.shape[0]
    gps = dtc_f_ref.shape[0]
    hpg = dtc_f_ref.shape[-1]
    width = x_ref.shape[1] // gps
    n = b_ref.shape[1] // gps
    hd = width // hpg
    groups = range(gps)
    expand = jnp.where(lax.broadcasted_iota(jnp.int32, (hpg, width), 1) // hd
                       == lax.broadcasted_iota(jnp.int32, (hpg, width), 0), 1.0, 0.0).astype(BF16)

    def widen(a):
        hi = a.astype(BF16)
        lo = (a - hi.astype(F32)).astype(BF16)
        return _dot(hi, expand) + _dot(lo, expand)

    def cols(gg, w):
        return slice(gg * w, (gg + 1) * w)

    @pl.when(i == 0)
    def _():
        sb_ref[...] = jnp.zeros_like(sb_ref)

    @pl.when(i == nb)
    def _():
        sf_ref[...] = jnp.zeros_like(sf_ref)

    @pl.when(i < nb)
    def _():
        for gg in groups:
            cub = cuc_b_ref[gg]
            tot = cub[0:1, :]
            wide = widen(jnp.concatenate([dtc_b_ref[gg] * jnp.exp(tot - cub), jnp.exp(cub[0:8, :])], axis=0))
            wgt, dec = wide[:q], wide[q:q + 1]
            xs = (x_ref[:, cols(gg, width)].astype(F32) * wgt).astype(BF16)
            cs = lax.dot_general(b_ref[:, cols(gg, n)], xs, (((0,), (0,)), ((), ())),
                                 preferred_element_type=F32)
            store_ref[i, gg] = sb_ref[gg].astype(BF16)
            sb_ref[gg] = sb_ref[gg] * dec + cs

    @pl.when(i >= nb)
    def _():
        i2 = i - nb
        slot = jnp.where(i2 < ncc, ncc - 1 - i2, nb - 1 - (i2 - ncc))
        li = lax.broadcasted_iota(jnp.int32, (q, q), 0)
        si = lax.broadcasted_iota(jnp.int32, (q, q), 1)
        causal = si <= li
        lane = lax.broadcasted_iota(jnp.int32, (q, 2 * hd), 1)
        cb = [lax.dot_general(c_ref[:, cols(gg, n)], b_ref[:, cols(gg, n)], (((1,), (1,)), ((), ())),
                              preferred_element_type=F32) for gg in groups]
        wide = []
        for gg in groups:
            cuf = cuc_f_ref[gg]
            wide.append(widen(jnp.concatenate(
                [jnp.exp(cuf), jnp.exp(cuc_b_ref[gg]), dtc_f_ref[gg] * jnp.exp(cuf[q - 1:q, :] - cuf)], axis=0)))
        y_off = []
        for gg in groups:
            ef, eb, wf = wide[gg][:q], wide[gg][q:2 * q], wide[gg][2 * q:]
            cm = c_ref[:, cols(gg, n)]
            y_off.append(_dot(cm, sf_ref[gg].astype(BF16)) * ef + _dot(cm, store_ref[slot, gg]) * eb)
            xs = (x_ref[:, cols(gg, width)].astype(F32) * wf).astype(BF16)
            cs = lax.dot_general(b_ref[:, cols(gg, n)], xs, (((0,), (0,)), ((), ())),
                                 preferred_element_type=F32)
            sf_ref[gg] = sf_ref[gg] * ef[q - 1:q, :] + cs

        y_diag = [[] for _ in groups]
        for p in range(hpg // 2):
            for gg in groups:
                xp = x_ref[:, gg * width + p * 2 * hd:gg * width + (p + 1) * 2 * hd]
                acc = None
                for half in range(2):
                    r = 2 * p + half
                    log_dec = jnp.where(causal, cuc_f_ref[gg, :, r:r + 1] - cur_f_ref[gg, r:r + 1, :],
                                        cuc_b_ref[gg, :, r:r + 1] - cur_b_ref[gg, r:r + 1, :])
                    dt_row = jnp.where(causal, dtr_f_ref[gg, r:r + 1, :], dtr_b_ref[gg, r:r + 1, :])
                    mix = (cb[gg] * (jnp.exp(log_dec) * dt_row)).astype(BF16)
                    keep = (lane < hd) if half == 0 else (lane >= hd)
                    part = _dot(mix, jnp.where(keep, xp, jnp.zeros_like(xp)))
                    acc = part if acc is None else acc + part
                y_diag[gg].append(acc)

        for gg in groups:
            cb_diag = jnp.sum(jnp.where(si == li, cb[gg], 0.0), axis=-1, keepdims=True)
            own_b = widen(dtc_b_ref[gg] * cb_diag)
            xf = x_ref[:, cols(gg, width)].astype(F32)
            y = (dskip_ref[:, cols(gg, width)] + own_b) * xf + y_off[gg] + jnp.concatenate(y_diag[gg], axis=1)
            gated = y * _silu(z_ref[:, cols(gg, width)].astype(F32))
            o_ref[:, cols(gg, width)] = (
                gated * lax.rsqrt(jnp.mean(gated * gated, axis=-1, keepdims=True) + EPS)
                * nw_ref[:, cols(gg, width)]).astype(o_ref.dtype)


def _ssd_scan(cfg, xbc, zx, dtc, cuc, dtr, cur, d_skip, norm_w):
    t, tr, batch = cfg["t"], ROW_TILE, cfg["batch"]
    g, n = SSD_GROUPS, SSD_STATE
    gps = SSD_GROUPS_PER_STEP if g % SSD_GROUPS_PER_STEP == 0 else 1
    inner = cfg["d"]
    width = inner // g
    hpg = width // SSD_HEAD_DIM
    assert hpg % 2 == 0 and tr == SSD_CHUNK and inner % (gps * n) == 0
    ncc, ncl = cfg["ctx_len"] // tr, cfg["seq"] // tr
    nb = ncc + ncl
    b_off, c_off = inner // (gps * n), (inner + g * n) // (gps * n)

    def blk(b, i):
        return jnp.where(i < nb, _bwd_block(cfg, b, i), _fwd_block(cfg, b, i - nb))

    def oblk(b, i):
        return _fwd_block(cfg, b, jnp.maximum(i - nb, 0))

    col = lambda d: pl.BlockSpec((gps, tr, hpg), lambda b, gi, i: (d * (g // gps) + gi, blk(b, i), 0))
    row = lambda d: pl.BlockSpec((gps, hpg, tr), lambda b, gi, i: (d * (g // gps) + gi, 0, blk(b, i)))
    kern = functools.partial(_ssd_kernel, nb=nb, ncc=ncc)
    return pl.pallas_call(
        kern,
        grid=(batch, g // gps, 2 * nb),
        in_specs=[
            pl.BlockSpec((tr, gps * width), lambda b, gi, i: (blk(b, i), gi)),
            pl.BlockSpec((tr, gps * n), lambda b, gi, i: (blk(b, i), b_off + gi)),
            pl.BlockSpec((tr, gps * n), lambda b, gi, i: (blk(b, i), c_off + gi)),
            pl.BlockSpec((tr, gps * width), lambda b, gi, i: (oblk(b, i), gi)),
            col(0), col(1), col(0), col(1), row(0), row(1), row(0), row(1),
            pl.BlockSpec((1, gps * width), lambda b, gi, i: (0, gi)),
            pl.BlockSpec((1, gps * width), lambda b, gi, i: (0, gi)),
        ],
        out_specs=pl.BlockSpec((tr, gps * width), lambda b, gi, i: (oblk(b, i), gi)),
        out_shape=jax.ShapeDtypeStruct((t, inner), BF16),
        scratch_shapes=[pltpu.VMEM((gps, n, width), F32), pltpu.VMEM((gps, n, width), F32),
                        pltpu.VMEM((nb, gps, n, width), BF16)],
        compiler_params=_cparams(("arbitrary", "arbitrary", "arbitrary")),
        name="ssd_scan",
    )(xbc, xbc, xbc, zx, dtc, dtc, cuc, cuc, dtr, dtr, cur, cur, d_skip, norm_w)


def _gdn_kernel(*refs, rev, finish):
    if finish:
        (q_ref, k_ref, v_ref, bc_ref, gc_ref, gr_ref, prev_ref, z_ref, nw_ref,
         o_ref, s_ref) = refs
    else:
        q_ref, k_ref, v_ref, bc_ref, gc_ref, gr_ref, o_ref, s_ref = refs
    c = GDN_CHUNK
    hw = GDN_HEAD
    n_heads = s_ref.shape[0]
    rep = n_heads // (k_ref.shape[1] // hw)

    @pl.when(pl.program_id(2) == 0)
    def _():
        s_ref[...] = jnp.zeros_like(s_ref)

    ii = lax.broadcasted_iota(jnp.int32, (c, c), 0)
    jj = lax.broadcasted_iota(jnp.int32, (c, c), 1)
    incl = (jj >= ii) if rev else (jj <= ii)
    strict = (jj > ii) if rev else (jj < ii)
    eye = jnp.where(ii == jj, 1.0, 0.0)
    n_chunks = q_ref.shape[0] // c
    order = range(n_chunks - 1, -1, -1) if rev else range(n_chunks)
    last = 0 if rev else c - 1

    heads = range(n_heads)
    steps = c.bit_length() - 1
    qk, kk, inv, pw, qkd, res, uw = {}, {}, {}, {}, {}, {}, {}

    def rows_of(ci):
        return slice(ci * c, (ci + 1) * c)

    def gram(ci):
        for kh in range(n_heads // rep):
            k_in = k_ref[rows_of(ci), kh * hw:(kh + 1) * hw]
            both = lax.dot_general(jnp.concatenate([q_ref[rows_of(ci), kh * hw:(kh + 1) * hw], k_in], axis=0),
                                   k_in, (((1,), (1,)), ((), ())), preferred_element_type=F32)
            qk[ci, kh], kk[ci, kh] = both[:c], both[c:]

    def construct(ci, h):
        rows = rows_of(ci)
        gc1, gr1 = gc_ref[rows, h:h + 1], gr_ref[h:h + 1, rows]
        dec = jnp.where(incl, jnp.exp(jnp.where(incl, gc1 - gr1, 0.0)), 0.0)
        a = jnp.where(strict, bc_ref[rows, h:h + 1] * kk[ci, h // rep] * dec, 0.0)
        inv[ci, h] = eye - a
        qkd[ci, h] = (qk[ci, h // rep] * dec).astype(BF16)
        a16 = a.astype(BF16)
        res[ci, h] = _dot(a16, a16)

    def level(s, it):
        if s == 1:
            pw[it] = res[it].astype(BF16)
        else:
            inv[it] = inv[it] + res[it][:c]
            pw[it] = res[it][c:].astype(BF16)
        if s < steps - 1:
            res[it] = _dot(jnp.concatenate([inv[it].astype(BF16), pw[it]], axis=0), pw[it])
        else:
            res[it] = _dot(inv[it].astype(BF16), pw[it])

    def solve(ci, h):
        rows = rows_of(ci)
        bc1, gc1 = bc_ref[rows, h:h + 1], gc_ref[rows, h:h + 1]
        kh = h // rep
        rhs = jnp.concatenate(
            [(v_ref[rows, h * hw:(h + 1) * hw].astype(F32) * bc1).astype(BF16),
             (k_ref[rows, kh * hw:(kh + 1) * hw].astype(F32) * (bc1 * jnp.exp(gc1))).astype(BF16)], axis=1)
        uw[ci, h] = _dot((inv[ci, h] + res[ci, h]).astype(BF16), rhs)

    st, ws_qs, kv, intra, gtot = {}, {}, {}, {}, {}

    def state_read(ci):
        for h in heads:
            st[h] = s_ref[h]
            ws_qs[h] = _dot(jnp.concatenate([uw[ci, h][:, hw:].astype(BF16),
                                             q_ref[rows_of(ci), (h // rep) * hw:(h // rep + 1) * hw]], axis=0),
                            st[h].astype(BF16))

    def state_update(ci):
        rows = rows_of(ci)
        for h in heads:
            v_new = (uw[ci, h][:, :hw] - ws_qs[h][:c]).astype(BF16)
            gc1 = gc_ref[rows, h:h + 1]
            gtot[h] = gc1[last:last + 1, :]
            kh = h // rep
            k_dec = (k_ref[rows, kh * hw:(kh + 1) * hw].astype(F32) * jnp.exp(gtot[h] - gc1)).astype(BF16)
            kv[h] = lax.dot_general(k_dec, v_new, (((0,), (0,)), ((), ())), preferred_element_type=F32)
            intra[h] = _dot(qkd[ci, h], v_new)

    def state_write(ci):
        rows = rows_of(ci)
        for h in heads:
            s_ref[h] = st[h] * jnp.exp(gtot[h]) + kv[h]
        for h in heads:
            out = jnp.exp(gc_ref[rows, h:h + 1]) * ws_qs[h][c:] + intra[h]
            cols = slice(h * hw, (h + 1) * hw)
            if finish:
                o = out + prev_ref[rows, cols]
                o = o * lax.rsqrt(jnp.mean(o * o, axis=-1, keepdims=True) + EPS) * nw_ref[...]
                o_ref[rows, cols] = (o * _silu(z_ref[rows, cols].astype(F32))).astype(o_ref.dtype)
            else:
                o_ref[rows, cols] = out

    pairs = [(ci, h) for ci in order for h in heads]
    for ci in order:
        gram(ci)
    for pair in pairs:
        construct(*pair)
    for s in range(1, steps):
        for pair in pairs:
            level(s, pair)
    for pair in pairs:
        solve(*pair)
    for ci in order:
        state_read(ci)
        state_update(ci)
        state_write(ci)


def _gdn_scan(cfg, qkv, bc, gc, gr, *, rev, prev=None, qkvz=None, norm_w=None):
    t, tr, batch = cfg["t"], ROW_TILE, cfg["batch"]
    hw = GDN_HEAD
    hv = cfg["d"] // hw
    hk = hv // 2
    hps = min(GDN_HEADS_PER_STEP, hv)
    kps = hps // 2
    nhb = hv // hps
    nb = (cfg["ctx_len"] + cfg["seq"]) // tr
    finish = prev is not None
    d = 1 if rev else 0
    order = _bwd_block if rev else _fwd_block

    def blk(b, i):
        return order(cfg, b, i)

    k_off = hk // kps
    v_off = 2 * hk // hps
    z_off = (2 * hk + hv) // hps
    col = pl.BlockSpec((None, tr, hps), lambda b, hb, i: (d * nhb + hb, blk(b, i), 0))
    row = pl.BlockSpec((None, hps, tr), lambda b, hb, i: (d * nhb + hb, 0, blk(b, i)))
    wide = lambda off: pl.BlockSpec((tr, hps * hw), lambda b, hb, i: (blk(b, i), off + hb))
    in_specs = [
        pl.BlockSpec((tr, kps * hw), lambda b, hb, i: (blk(b, i), hb)),
        pl.BlockSpec((tr, kps * hw), lambda b, hb, i: (blk(b, i), k_off + hb)),
        wide(v_off), col, col, row,
    ]
    ins = [qkv, qkv, qkv, bc, gc, gr]
    if finish:
        in_specs += [wide(0), wide(z_off), pl.BlockSpec((1, hw), lambda b, hb, i: (0, 0))]
        ins += [prev, qkvz, norm_w]
    return pl.pallas_call(
        functools.partial(_gdn_kernel, rev=rev, finish=finish),
        grid=(batch, nhb, nb),
        in_specs=in_specs,
        out_specs=wide(0),
        out_shape=jax.ShapeDtypeStruct((t, hv * hw), BF16 if finish else F32),
        scratch_shapes=[pltpu.VMEM((hps, hw, hw), F32)],
        compiler_params=_cparams(("arbitrary", "arbitrary", "arbitrary")),
        name="gdn_scan_bwd" if rev else "gdn_scan_fwd",
    )(*ins)


def _pad_lanes(v, width=128):
    v = v.reshape(1, -1).astype(F32)
    return jnp.pad(v, ((0, 0), (0, width - v.shape[1])))


def _pad_cols(w, width=128):
    return jnp.pad(w, ((0, 0), (0, width - w.shape[1])))


def _tail_weights(w, layer, main):
    if w.shape[-1] - main == LANES:
        return w, layer, main
    return _pad_cols(w[layer][:, main:])[None], 0, 0


def _ssd_mixer(cfg, h, layer, w_in, conv_w, conv_b, a_log, dt_bias, d_skip, norm_w, w_out):
    d = cfg["d"]
    heads = d // SSD_HEAD_DIM
    bc_dim = SSD_GROUPS * SSD_STATE
    main = 2 * d + 2 * bc_dim
    tm = cfg["tm"]
    zx = _matmul(h, w_in, layer, 0, main, BF16, tm, _pick(main, (512, 256, 128)), "ssd_in_proj")
    w_tail, tail_layer, tail_lo = _tail_weights(w_in, layer, main)
    tail = _matmul(h, w_tail, tail_layer, tail_lo, tail_lo + LANES, F32, tm, LANES, "ssd_in_proj_dt")
    xbc = _conv_silu(cfg, zx, d, d + 2 * bc_dim, conv_w, conv_b)
    a_neg = -jnp.exp(a_log.astype(F32))
    dtc, cuc, dtr, cur = _prep(cfg, tail, _pad_lanes(dt_bias), _pad_lanes(a_neg),
                               kind="ssd", chunk=SSD_CHUNK, n_heads=heads, hps=heads // SSD_GROUPS)
    y = _ssd_scan(cfg, xbc, zx, dtc, cuc, dtr, cur,
                  jnp.repeat(d_skip.astype(F32), SSD_HEAD_DIM).reshape(1, d), norm_w.reshape(1, d))
    return _matmul(y, w_out, layer, 0, d, BF16, tm, _pick(d, (512, 256, 128)), "ssd_out_proj")


def _gdn_mixer(cfg, h, layer, w_in, conv_w, a_log, dt_bias, norm_w, w_out):
    d = cfg["d"]
    hv = d // GDN_HEAD
    key_dim = (hv // 2) * GDN_HEAD
    conv_dim = 2 * key_dim + d
    main = conv_dim + d
    tm = cfg["tm"]
    qkvz = _matmul(h, w_in, layer, 0, main, BF16, tm, _pick(main, (512, 256, 128)), "gdn_in_proj")
    w_tail, tail_layer, tail_lo = _tail_weights(w_in, layer, main)
    tail = _matmul(h, w_tail, tail_layer, tail_lo, tail_lo + LANES, F32, tm, LANES, "gdn_in_proj_gates")
    qkv = _conv_silu(cfg, qkvz, 0, conv_dim, conv_w, jnp.zeros((conv_dim,), F32),
                     norm_ch=2 * key_dim, q_ch=key_dim, q_scale=GDN_HEAD ** -0.5)
    a_pos = jnp.exp(a_log.astype(F32)).reshape(-1)
    zeros = jnp.zeros((2 * hv,), F32)
    bias_full = _pad_lanes(jnp.concatenate([zeros, dt_bias.astype(F32).reshape(-1)]))
    amul_full = _pad_lanes(jnp.concatenate([zeros, -a_pos]))
    hps = min(GDN_HEADS_PER_STEP, hv)
    bc, gc, _, gr = _prep(cfg, tail, bias_full, amul_full, kind="gdn", chunk=GDN_CHUNK,
                           n_heads=hv, hps=hps)
    o_f = _gdn_scan(cfg, qkv, bc, gc, gr, rev=False)
    y = _gdn_scan(cfg, qkv, bc, gc, gr, rev=True, prev=o_f, qkvz=qkvz,
                  norm_w=norm_w.reshape(1, GDN_HEAD).astype(F32))
    return _matmul(y, w_out, layer, 0, d, BF16, tm, _pick(d, (512, 256, 128)), "gdn_out_proj")


def _ffn(cfg, h, layer, w_in, w_out_bf):
    d = cfg["d"]
    d_ff = w_in.shape[-1] // 2
    hid = _matmul_swiglu(h, w_in, layer, cfg["tm"], _pick(d_ff, (256, 128)))
    return _matmul(hid, w_out_bf, layer, 0, d, BF16, cfg["tm_out"], _pick(d, (512, 256, 128)), "ffn_out")


def kernel(x, c, ctx, c_ctx, ada_down, ada_up, ada_bias, norm_mix, norm_ffn, ffn_in, ffn_out,
           ssd_in, ssd_conv_w, ssd_conv_b, ssd_a_log, ssd_dt_bias, ssd_d, ssd_norm, ssd_out,
           gdn_in, gdn_conv_w, gdn_a_log, gdn_dt_bias, gdn_norm, gdn_out, final_norm):
    batch, seq, d = x.shape
    ctx_len = ctx.shape[1]
    depth = ada_down.shape[0]
    rows = seq // GRID_W
    assert rows == ROW_TILE and ctx_len % CONV_ROWS == 0 and seq % CONV_ROWS == 0
    t_lat, t_ctx = batch * seq, batch * ctx_len
    t = t_lat + t_ctx
    cfg = dict(d=d, batch=batch, seq=seq, ctx_len=ctx_len, cols=GRID_W, t_lat=t_lat, t_ctx=t_ctx, t=t,
               tm=_pick(t, (1536, 1024, 512, 256, 128)), tm_out=_pick(t, (512, 256, 128)))
    assert t_ctx % GRID_W == 0

    n_streams = batch + 1
    cv = jnp.concatenate([c.astype(F32), c_ctx.astype(F32)[None, :],
                          jnp.zeros((8 - n_streams % 8 if n_streams % 8 else 0, d), F32)], axis=0)
    mods = _ada_modulation(cv, ada_down, ada_up, ada_bias)

    def mod(layer, which):
        return mods[layer, :n_streams, which * d:(which + 1) * d].reshape(n_streams, 1, d)

    ffn_out_bf = ffn_out.astype(BF16)

    x_lat = x.reshape(t_lat, d)
    x_ctx = ctx.reshape(t_ctx, d)
    pending, pending_gate = None, None
    col_major = False
    for i in range(depth):
        want_col_major = (i // N_MIXERS) % 2 == 1
        assert want_col_major or not col_major
        j = i // N_MIXERS
        x_lat, x_ctx, h = _norm_modulate(cfg, x_lat, x_ctx, pending, pending_gate,
                                         norm_mix[i].reshape(1, d), mod(i, 0), mod(i, 1),
                                         to_col_major=want_col_major and not col_major)
        col_major = want_col_major
        if i % N_MIXERS == 0:
            y = _ssd_mixer(cfg, h, j, ssd_in, ssd_conv_w[j], ssd_conv_b[j], ssd_a_log[j],
                           ssd_dt_bias[j], ssd_d[j], ssd_norm[j], ssd_out)
        else:
            y = _gdn_mixer(cfg, h, j, gdn_in, gdn_conv_w[j], gdn_a_log[j], gdn_dt_bias[j],
                           gdn_norm[j], gdn_out)
        x_lat, x_ctx, h = _norm_modulate(cfg, x_lat, x_ctx, y, mod(i, 2),
                                         norm_ffn[i].reshape(1, d), mod(i, 3), mod(i, 4),
                                         to_col_major=False)
        pending, pending_gate = _ffn(cfg, h, i, ffn_in, ffn_out_bf), mod(i, 5)
    out = _final_norm(cfg, x_lat, pending, pending_gate, final_norm.reshape(1, d),
                      from_col_major=col_major)
    return out.reshape(batch, seq, d)
```

```python
import functools

import jax
import jax.numpy as jnp
from jax import lax
from jax.experimental import pallas as pl
from jax.experimental.pallas import tpu as pltpu

F32 = jnp.float32
BF16 = jnp.bfloat16
EPS = 1e-6

GRID_W = 64
N_MIXERS = 2
N_MOD = 6
SSD_HEAD_DIM = 64
SSD_GROUPS = 8
SSD_STATE = 128
SSD_CHUNK = 128
SSD_GROUPS_PER_STEP = 4
GDN_HEAD = 128
GDN_CHUNK = 64
GDN_HEADS_PER_STEP = 8

ROW_TILE = 128
NORM_ROWS = 256
CONV_ROWS = 256
HALO = 8
LANES = 128
CONV_PITCH = 36
CONV_SLAB_UNROLL = 2
V7X_VMEM_LIMIT = 56 * 1024 * 1024


def _cparams(sem, vmem=None):
    return pltpu.CompilerParams(dimension_semantics=sem, vmem_limit_bytes=vmem or V7X_VMEM_LIMIT)


def _sigmoid(x):
    return 1.0 / (1.0 + jnp.exp(-x))


def _silu(x):
    h = 0.5 * x
    return h + h * jnp.tanh(h)


def _softplus(x):
    return jnp.maximum(x, 0.0) + jnp.log1p(jnp.exp(-jnp.abs(x)))


def _pick(n, candidates):
    for c in candidates:
        if n % c == 0:
            return c
    raise ValueError(f"no tile for {n} among {candidates}")


def _split3(a):
    hi = a.astype(BF16).astype(F32)
    r1 = a - hi
    mid = r1.astype(BF16).astype(F32)
    lo = (r1 - mid).astype(BF16).astype(F32)
    return hi, mid, lo


def _dot(a, b):
    return jnp.dot(a, b, preferred_element_type=F32)


def _dot_exact_rhs(m, a):
    hi, mid, lo = _split3(a)
    mb = m.astype(BF16)
    return _dot(mb, hi.astype(BF16)) + _dot(mb, mid.astype(BF16)) + _dot(mb, lo.astype(BF16))


def _ada_kernel(cv_ref, down_ref, up_ref, bias_ref, o_ref):
    t = jnp.dot(_silu(cv_ref[...]), down_ref[...], preferred_element_type=F32,
                precision=lax.Precision.HIGHEST)
    o_ref[...] = jnp.dot(t, up_ref[...], preferred_element_type=F32,
                         precision=lax.Precision.HIGHEST) + bias_ref[...]


def _ada_modulation(cv, down, up, bias):
    depth, d, r = down.shape
    n = up.shape[-1]
    tn = _pick(n, (6144, 3072, 1536, 768, 512, 256, 128))
    rows = cv.shape[0]
    return pl.pallas_call(
        _ada_kernel,
        grid=(depth, n // tn),
        in_specs=[
            pl.BlockSpec((rows, d), lambda l, j: (0, 0)),
            pl.BlockSpec((None, d, r), lambda l, j: (l, 0, 0)),
            pl.BlockSpec((None, r, tn), lambda l, j: (l, 0, j)),
            pl.BlockSpec((None, 1, tn), lambda l, j: (l, 0, j)),
        ],
        out_specs=pl.BlockSpec((None, rows, tn), lambda l, j: (l, 0, j)),
        out_shape=jax.ShapeDtypeStruct((depth, rows, n), F32),
        compiler_params=_cparams(("arbitrary", "arbitrary")),
        name="ada_modulation",
    )(cv, down, up, bias.reshape(depth, 1, n))


def _norm_kernel(*refs, has_y, n_lat_tiles):
    if has_y:
        (xl_ref, xc_ref, yl_ref, yc_ref, g_ref, w_ref, sh_ref, sc_ref,
         xlo_ref, xco_ref, h_ref) = refs
    else:
        xl_ref, xc_ref, w_ref, sh_ref, sc_ref, h_ref = refs

    def run(x_ref, y_ref, xo_ref):
        x = x_ref[...]
        if has_y:
            x = x + g_ref[...] * y_ref[...]
            xo_ref[...] = x
        xn = x * lax.rsqrt(jnp.mean(x * x, axis=-1, keepdims=True) + EPS) * w_ref[...]
        h_ref[...] = (xn * (1.0 + sc_ref[...]) + sh_ref[...]).astype(h_ref.dtype)

    i = pl.program_id(0)

    @pl.when(i < n_lat_tiles)
    def _():
        run(xl_ref, yl_ref if has_y else None, xlo_ref if has_y else None)

    @pl.when(i >= n_lat_tiles)
    def _():
        run(xc_ref, yc_ref if has_y else None, xco_ref if has_y else None)


def _norm_modulate(cfg, x_lat, x_ctx, y, gate, w, shift, scale, *, to_col_major):
    d = cfg["d"]
    tr = ROW_TILE if to_col_major else NORM_ROWS
    assert cfg["seq"] % tr == 0 and cfg["t_ctx"] % tr == 0
    n_lat, n_ctx = cfg["t_lat"] // tr, cfg["t_ctx"] // tr
    per_batch = cfg["seq"] // tr
    cols = cfg["cols"]
    has_y = y is not None

    def lat_tile(i):
        return jnp.minimum(i, n_lat - 1)

    def ctx_tile(i):
        return jnp.maximum(i - n_lat, 0)

    def stream(i):
        return jnp.where(i < n_lat, i // per_batch, cfg["batch"])

    if to_col_major:
        def strided(i):
            il = lat_tile(i)
            return (il // cols, il % cols)
        xl_in = x_lat.reshape(cfg["t_lat"] // cols, cols * d)
        xl_spec = pl.BlockSpec((tr, d), strided)
    else:
        xl_in = x_lat
        xl_spec = pl.BlockSpec((tr, d), lambda i: (lat_tile(i), 0))
    xc_spec = pl.BlockSpec((tr, d), lambda i: (ctx_tile(i), 0))
    mod_spec = pl.BlockSpec((None, 1, d), lambda i: (stream(i), 0, 0))
    w_spec = pl.BlockSpec((1, d), lambda i: (0, 0))
    h_spec = pl.BlockSpec((tr, d), lambda i: (i, 0))
    h_shape = jax.ShapeDtypeStruct((cfg["t"], d), BF16)

    if has_y:
        if to_col_major:
            yl_in = y.reshape(cfg["t"] // cols, cols * d)
            yl_spec = pl.BlockSpec((tr, d), strided)
        else:
            yl_in = y
            yl_spec = pl.BlockSpec((tr, d), lambda i: (lat_tile(i), 0))
        yc_spec = pl.BlockSpec((tr, d), lambda i: (n_lat + ctx_tile(i), 0))
        ins = (xl_in, x_ctx, yl_in, y, gate, w, shift, scale)
        in_specs = [xl_spec, xc_spec, yl_spec, yc_spec, mod_spec, w_spec, mod_spec, mod_spec]
        out_specs = [pl.BlockSpec((tr, d), lambda i: (lat_tile(i), 0)), xc_spec, h_spec]
        out_shape = [jax.ShapeDtypeStruct((cfg["t_lat"], d), F32),
                     jax.ShapeDtypeStruct((cfg["t_ctx"], d), F32), h_shape]
    else:
        ins = (xl_in, x_ctx, w, shift, scale)
        in_specs = [xl_spec, xc_spec, w_spec, mod_spec, mod_spec]
        out_specs = h_spec
        out_shape = h_shape
    out = pl.pallas_call(
        functools.partial(_norm_kernel, has_y=has_y, n_lat_tiles=n_lat),
        grid=(n_lat + n_ctx,),
        in_specs=in_specs, out_specs=out_specs, out_shape=out_shape,
        compiler_params=_cparams(("arbitrary",)),
        name="norm_modulate",
    )(*ins)
    if has_y:
        return out
    return x_lat, x_ctx, out


def _final_kernel(x_ref, y_ref, g_ref, w_ref, o_ref):
    x = x_ref[...] + g_ref[...] * y_ref[...]
    o_ref[...] = x * lax.rsqrt(jnp.mean(x * x, axis=-1, keepdims=True) + EPS) * w_ref[...]


def _final_norm(cfg, x_lat, y, gate, w, *, from_col_major):
    d, tr = cfg["d"], ROW_TILE
    n_lat = cfg["t_lat"] // tr
    per_batch = cfg["seq"] // tr
    cols = cfg["cols"]
    plain = pl.BlockSpec((tr, d), lambda i: (i, 0))
    if from_col_major:
        out_spec = pl.BlockSpec((tr, d), lambda i: (i // cols, i % cols))
        out_shape = jax.ShapeDtypeStruct((cfg["t_lat"] // cols, cols * d), F32)
    else:
        out_spec, out_shape = plain, jax.ShapeDtypeStruct((cfg["t_lat"], d), F32)
    out = pl.pallas_call(
        _final_kernel,
        grid=(n_lat,),
        in_specs=[plain, plain,
                  pl.BlockSpec((None, 1, d), lambda i: (i // per_batch, 0, 0)),
                  pl.BlockSpec((1, d), lambda i: (0, 0))],
        out_specs=out_spec, out_shape=out_shape,
        compiler_params=_cparams(("arbitrary",)),
        name="final_norm",
    )(x_lat, y, gate, w)
    return out.reshape(cfg["t_lat"], d)


def _mm_kernel(a_ref, w_ref, o_ref):
    o_ref[...] = _dot(a_ref[...], w_ref[...].astype(BF16)).astype(o_ref.dtype)


def _matmul(a, w, layer, col_lo, col_hi, out_dtype, tm, tn, name):
    m, k = a.shape
    n = col_hi - col_lo
    assert m % tm == 0 and n % tn == 0 and col_lo % tn == 0
    off = col_lo // tn
    return pl.pallas_call(
        _mm_kernel,
        grid=(m // tm, n // tn),
        in_specs=[pl.BlockSpec((tm, k), lambda i, j: (i, 0)),
                  pl.BlockSpec((None, k, tn), lambda i, j: (layer, 0, off + j))],
        out_specs=pl.BlockSpec((tm, tn), lambda i, j: (i, j)),
        out_shape=jax.ShapeDtypeStruct((m, n), out_dtype),
        compiler_params=_cparams(("arbitrary", "arbitrary")),
        name=name,
    )(a, w)


def _swiglu_kernel(a_ref, wg_ref, wu_ref, o_ref):
    a = a_ref[...]
    g = _dot(a, wg_ref[...].astype(BF16))
    u = _dot(a, wu_ref[...].astype(BF16))
    o_ref[...] = (_silu(g) * u).astype(o_ref.dtype)


def _matmul_swiglu(a, w, layer, tm, tn):
    m, k = a.shape
    d_ff = w.shape[-1] // 2
    assert m % tm == 0 and d_ff % tn == 0
    nj = d_ff // tn
    return pl.pallas_call(
        _swiglu_kernel,
        grid=(m // tm, nj),
        in_specs=[pl.BlockSpec((tm, k), lambda i, j: (i, 0)),
                  pl.BlockSpec((None, k, tn), lambda i, j: (layer, 0, j)),
                  pl.BlockSpec((None, k, tn), lambda i, j: (layer, 0, nj + j))],
        out_specs=pl.BlockSpec((tm, tn), lambda i, j: (i, j)),
        out_shape=jax.ShapeDtypeStruct((m, d_ff), BF16),
        compiler_params=_cparams(("arbitrary", "arbitrary")),
        name="ffn_in_swiglu",
    )(a, w, w)


def _conv_kernel(prev_ref, cur_ref, next_ref, w_ref, b_ref, o_ref, scr_ref, out_ref, *,
                 tiles_per_lat_seq, n_lat_tiles, tiles_per_ctx_seq, n_norm_tiles, n_q_tiles, q_scale):
    i, j = pl.program_id(0), pl.program_id(1)
    ts = cur_ref.shape[0]
    ksz = w_ref.shape[0]
    is_lat = i < n_lat_tiles
    pos = jnp.where(is_lat, i % tiles_per_lat_seq, (i - n_lat_tiles) % tiles_per_ctx_seq)
    per_seq = jnp.where(is_lat, tiles_per_lat_seq, tiles_per_ctx_seq)
    keep_prev = jnp.where(pos == 0, 0.0, 1.0)
    keep_next = jnp.where(pos == per_seq - 1, 0.0, 1.0)
    if n_norm_tiles:
        normed = j < n_norm_tiles
        scale = jnp.where(j < n_q_tiles, q_scale, 1.0)
    first = HALO - ksz // 2
    n_slabs = cur_ref.shape[1] // LANES

    def one_slab(sl):
        lanes = pl.ds(pl.multiple_of(sl * LANES, LANES), LANES)
        scr_ref[sl, 0:HALO, :] = prev_ref[:, lanes].astype(F32) * keep_prev
        scr_ref[sl, HALO:HALO + ts, :] = cur_ref[:, lanes].astype(F32)
        scr_ref[sl, HALO + ts:2 * HALO + ts, :] = next_ref[:, lanes].astype(F32) * keep_next
        scr_ref[sl, 2 * HALO + ts:, :] = jnp.zeros((scr_ref.shape[1] - 2 * HALO - ts, LANES), F32)
        wk = [jnp.broadcast_to(w_ref[k:k + 1, lanes], (8, LANES)) for k in range(ksz)]
        bias = jnp.broadcast_to(b_ref[:, lanes], (8, LANES))
        taps = [scr_ref[sl, pl.ds(first + g, 8, stride=CONV_PITCH), :]
                for g in range(CONV_PITCH + ksz - 1)]
        for g in range(CONV_PITCH):
            acc = bias + wk[0] * taps[g]
            for k in range(1, ksz):
                acc = acc + wk[k] * taps[g + k]
            y = _silu(acc)
            if n_norm_tiles:
                r = lax.rsqrt(jnp.sum(y * y, axis=-1, keepdims=True) + EPS) * scale
                y = y * jnp.where(normed, r, 1.0)
            out_ref[sl, pl.ds(g, 8, stride=CONV_PITCH), :] = y
        o_ref[:, lanes] = out_ref[sl, 0:ts, :].astype(o_ref.dtype)

    @pl.loop(0, n_slabs // CONV_SLAB_UNROLL)
    def _(it):
        for u in range(CONV_SLAB_UNROLL):
            one_slab(it * CONV_SLAB_UNROLL + u)


def _conv_silu(cfg, src, col_lo, n_ch, w, b, *, norm_ch=0, q_ch=0, q_scale=1.0):
    t, ts = cfg["t"], CONV_ROWS
    tc = next(c for c in (2048, 1024, 512, 256)
              if all(v % c == 0 for v in (n_ch, col_lo, norm_ch, q_ch)))
    assert (tc // LANES) % CONV_SLAB_UNROLL == 0
    assert cfg["seq"] % ts == 0 and cfg["ctx_len"] % ts == 0 and 8 * CONV_PITCH >= ts
    ksz = w.shape[0]
    last_row = (HALO - ksz // 2) + (CONV_PITCH + ksz - 2) + 7 * CONV_PITCH
    in_rows = -(-(last_row + 1) // 8) * 8
    assert in_rows >= ts + 2 * HALO
    off = col_lo // tc
    hb = ts // HALO
    last_hb = t // HALO - 1
    kern = functools.partial(
        _conv_kernel, tiles_per_lat_seq=cfg["seq"] // ts, n_lat_tiles=cfg["t_lat"] // ts,
        tiles_per_ctx_seq=cfg["ctx_len"] // ts, n_norm_tiles=norm_ch // tc, n_q_tiles=q_ch // tc,
        q_scale=q_scale)
    return pl.pallas_call(
        kern,
        grid=(t // ts, n_ch // tc),
        in_specs=[
            pl.BlockSpec((HALO, tc), lambda i, j: (jnp.maximum(i * hb - 1, 0), off + j)),
            pl.BlockSpec((ts, tc), lambda i, j: (i, off + j)),
            pl.BlockSpec((HALO, tc), lambda i, j: (jnp.minimum((i + 1) * hb, last_hb), off + j)),
            pl.BlockSpec((w.shape[0], tc), lambda i, j: (0, j)),
            pl.BlockSpec((1, tc), lambda i, j: (0, j)),
        ],
        out_specs=pl.BlockSpec((ts, tc), lambda i, j: (i, j)),
        out_shape=jax.ShapeDtypeStruct((t, n_ch), BF16),
        scratch_shapes=[pltpu.VMEM((tc // LANES, in_rows, LANES), F32),
                        pltpu.VMEM((tc // LANES, 8 * CONV_PITCH, LANES), F32)],
        compiler_params=_cparams(("arbitrary", "arbitrary")),
        name="conv_silu",
    )(src, src, src, w, b.reshape(1, n_ch))


def _prep_kernel(tail_ref, bias_ref, amul_ref, p1c_ref, p2c_ref, p1r_ref, p2r_ref, *,
                 kind, chunk, p2_off, split_lane, hps):
    tr = ROW_TILE
    row = lax.broadcasted_iota(jnp.int32, (tr, tr), 0)
    col = lax.broadcasted_iota(jnp.int32, (tr, tr), 1)
    same = (row // chunk) == (col // chunk)
    lower = jnp.where(same & (col <= row), 1.0, 0.0)
    upper = jnp.where(same & (col >= row), 1.0, 0.0)
    lane = lax.broadcasted_iota(jnp.int32, (tr, LANES), 1)
    for sub in range(tail_ref.shape[0] // tr):
        rows = slice(sub * tr, (sub + 1) * tr)
        raw = tail_ref[rows, :]
        sp = _softplus(raw + bias_ref[...])
        p1 = sp if kind == "ssd" else _sigmoid(raw)
        a = sp * amul_ref[...]
        cum = jnp.where(lane < split_lane, _dot_exact_rhs(lower, a), _dot_exact_rhs(upper, a))
        p1t = p1.T
        cumt = cum.T
        for n in range(p1c_ref.shape[0]):
            p1c_ref[n, rows, :] = p1[:, n * hps:(n + 1) * hps]
            p2c_ref[n, rows, :] = cum[:, p2_off + n * hps:p2_off + (n + 1) * hps]
            p1r_ref[n, :, rows] = p1t[n * hps:(n + 1) * hps, :]
            p2r_ref[n, :, rows] = cumt[p2_off + n * hps:p2_off + (n + 1) * hps, :]


def _prep(cfg, tail, bias_full, amul_full, *, kind, chunk, n_heads, hps):
    t = cfg["t"]
    tr = _pick(t, (4 * ROW_TILE, 2 * ROW_TILE, ROW_TILE))
    nblk = 2 * n_heads // hps
    p2_off = 0 if kind == "ssd" else 2 * n_heads
    split_lane = p2_off + n_heads
    kern = functools.partial(_prep_kernel, kind=kind, chunk=chunk, p2_off=p2_off,
                             split_lane=split_lane, hps=hps)
    col_spec = pl.BlockSpec((nblk, tr, hps), lambda i: (0, i, 0))
    row_spec = pl.BlockSpec((nblk, hps, tr), lambda i: (0, 0, i))
    col_shape = jax.ShapeDtypeStruct((nblk, t, hps), F32)
    row_shape = jax.ShapeDtypeStruct((nblk, hps, t), F32)
    return pl.pallas_call(
        kern,
        grid=(t // tr,),
        in_specs=[pl.BlockSpec((tr, 128), lambda i: (i, 0)),
                  pl.BlockSpec((1, 128), lambda i: (0, 0)),
                  pl.BlockSpec((1, 128), lambda i: (0, 0))],
        out_specs=[col_spec, col_spec, row_spec, row_spec],
        out_shape=[col_shape, col_shape, row_shape, row_shape],
        compiler_params=_cparams(("arbitrary",)),
        name=f"prep_{kind}",
    )(tail, bias_full, amul_full)


def _fwd_block(cfg, b, i):
    ncc, ncl = cfg["ctx_len"] // ROW_TILE, cfg["seq"] // ROW_TILE
    ctx = cfg["batch"] * ncl + b * ncc + i
    lat = b * ncl + (i - ncc)
    return jnp.where(i < ncc, ctx, lat)


def _bwd_block(cfg, b, i):
    ncc, ncl = cfg["ctx_len"] // ROW_TILE, cfg["seq"] // ROW_TILE
    ctx = cfg["batch"] * ncl + b * ncc + (ncc - 1 - i)
    lat = b * ncl + (ncl - 1 - (i - ncc))
    return jnp.where(i < ncc, ctx, lat)


def _ssd_kernel(x_ref, b_ref, c_ref, z_ref, dtc_f_ref, dtc_b_ref, cuc_f_ref, cuc_b_ref,
                dtr_f_ref, dtr_b_ref, cur_f_ref, cur_b_ref, dskip_ref, nw_ref,
                o_ref, sf_ref, sb_ref, store_ref, *, nb, ncc):
    i = pl.program_id(2)
    q = x_ref.shape[0]
    gps = dtc_f_ref.shape[0]
    hpg = dtc_f_ref.shape[-1]
    width = x_ref.shape[1] // gps
    n = b_ref.shape[1] // gps
    hd = width // hpg
    groups = range(gps)
    expand = jnp.where(lax.broadcasted_iota(jnp.int32, (hpg, width), 1) // hd
                       == lax.broadcasted_iota(jnp.int32, (hpg, width), 0), 1.0, 0.0).astype(BF16)

    def widen(a):
        hi = a.astype(BF16)
        lo = (a - hi.astype(F32)).astype(BF16)
        return _dot(hi, expand) + _dot(lo, expand)

    def cols(gg, w):
        return slice(gg * w, (gg + 1) * w)

    @pl.when(i == 0)
    def _():
        sb_ref[...] = jnp.zeros_like(sb_ref)

    @pl.when(i == nb)
    def _():
        sf_ref[...] = jnp.zeros_like(sf_ref)

    @pl.when(i < nb)
    def _():
        for gg in groups:
            cub = cuc_b_ref[gg]
            tot = cub[0:1, :]
            wide = widen(jnp.concatenate([dtc_b_ref[gg] * jnp.exp(tot - cub), jnp.exp(cub[0:8, :])], axis=0))
            wgt, dec = wide[:q], wide[q:q + 1]
            xs = (x_ref[:, cols(gg, width)].astype(F32) * wgt).astype(BF16)
            cs = lax.dot_general(b_ref[:, cols(gg, n)], xs, (((0,), (0,)), ((), ())),
                                 preferred_element_type=F32)
            store_ref[i, gg] = sb_ref[gg].astype(BF16)
            sb_ref[gg] = sb_ref[gg] * dec + cs

    @pl.when(i >= nb)
    def _():
        i2 = i - nb
        slot = jnp.where(i2 < ncc, ncc - 1 - i2, nb - 1 - (i2 - ncc))
        li = lax.broadcasted_iota(jnp.int32, (q, q), 0)
        si = lax.broadcasted_iota(jnp.int32, (q, q), 1)
        causal = si <= li
        lane = lax.broadcasted_iota(jnp.int32, (q, 2 * hd), 1)
        cb = [lax.dot_general(c_ref[:, cols(gg, n)], b_ref[:, cols(gg, n)], (((1,), (1,)), ((), ())),
                              preferred_element_type=F32) for gg in groups]
        wide = []
        for gg in groups:
            cuf = cuc_f_ref[gg]
            wide.append(widen(jnp.concatenate(
                [jnp.exp(cuf), jnp.exp(cuc_b_ref[gg]), dtc_f_ref[gg] * jnp.exp(cuf[q - 1:q, :] - cuf)], axis=0)))
        y_off = []
        for gg in groups:
            ef, eb, wf = wide[gg][:q], wide[gg][q:2 * q], wide[gg][2 * q:]
            cm = c_ref[:, cols(gg, n)]
            y_off.append(_dot(cm, sf_ref[gg].astype(BF16)) * ef + _dot(cm, store_ref[slot, gg]) * eb)
            xs = (x_ref[:, cols(gg, width)].astype(F32) * wf).astype(BF16)
            cs = lax.dot_general(b_ref[:, cols(gg, n)], xs, (((0,), (0,)), ((), ())),
                                 preferred_element_type=F32)
            sf_ref[gg] = sf_ref[gg] * ef[q - 1:q, :] + cs

        y_diag = [[] for _ in groups]
        for p in range(hpg // 2):
            for gg in groups:
                xp = x_ref[:, gg * width + p * 2 * hd:gg * width + (p + 1) * 2 * hd]
                acc = None
                for half in range(2):
                    r = 2 * p + half
                    log_dec = jnp.where(causal, cuc_f_ref[gg, :, r:r + 1] - cur_f_ref[gg, r:r + 1, :],
                                        cuc_b_ref[gg, :, r:r + 1] - cur_b_ref[gg, r:r + 1, :])
                    dt_row = jnp.where(causal, dtr_f_ref[gg, r:r + 1, :], dtr_b_ref[gg, r:r + 1, :])
                    mix = (cb[gg] * (jnp.exp(log_dec) * dt_row)).astype(BF16)
                    keep = (lane < hd) if half == 0 else (lane >= hd)
                    part = _dot(mix, jnp.where(keep, xp, jnp.zeros_like(xp)))
                    acc = part if acc is None else acc + part
                y_diag[gg].append(acc)

        for gg in groups:
            cb_diag = jnp.sum(jnp.where(si == li, cb[gg], 0.0), axis=-1, keepdims=True)
            own_b = widen(dtc_b_ref[gg] * cb_diag)
            xf = x_ref[:, cols(gg, width)].astype(F32)
            y = (dskip_ref[:, cols(gg, width)] + own_b) * xf + y_off[gg] + jnp.concatenate(y_diag[gg], axis=1)
            gated = y * _silu(z_ref[:, cols(gg, width)].astype(F32))
            o_ref[:, cols(gg, width)] = (
                gated * lax.rsqrt(jnp.mean(gated * gated, axis=-1, keepdims=True) + EPS)
                * nw_ref[:, cols(gg, width)]).astype(o_ref.dtype)


def _ssd_scan(cfg, xbc, zx, dtc, cuc, dtr, cur, d_skip, norm_w):
    t, tr, batch = cfg["t"], ROW_TILE, cfg["batch"]
    g, n = SSD_GROUPS, SSD_STATE
    gps = SSD_GROUPS_PER_STEP if g % SSD_GROUPS_PER_STEP == 0 else 1
    inner = cfg["d"]
    width = inner // g
    hpg = width // SSD_HEAD_DIM
    assert hpg % 2 == 0 and tr == SSD_CHUNK and inner % (gps * n) == 0
    ncc, ncl = cfg["ctx_len"] // tr, cfg["seq"] // tr
    nb = ncc + ncl
    b_off, c_off = inner // (gps * n), (inner + g * n) // (gps * n)

    def blk(b, i):
        return jnp.where(i < nb, _bwd_block(cfg, b, i), _fwd_block(cfg, b, i - nb))

    def oblk(b, i):
        return _fwd_block(cfg, b, jnp.maximum(i - nb, 0))

    col = lambda d: pl.BlockSpec((gps, tr, hpg), lambda b, gi, i: (d * (g // gps) + gi, blk(b, i), 0))
    row = lambda d: pl.BlockSpec((gps, hpg, tr), lambda b, gi, i: (d * (g // gps) + gi, 0, blk(b, i)))
    kern = functools.partial(_ssd_kernel, nb=nb, ncc=ncc)
    return pl.pallas_call(
        kern,
        grid=(batch, g // gps, 2 * nb),
        in_specs=[
            pl.BlockSpec((tr, gps * width), lambda b, gi, i: (blk(b, i), gi)),
            pl.BlockSpec((tr, gps * n), lambda b, gi, i: (blk(b, i), b_off + gi)),
            pl.BlockSpec((tr, gps * n), lambda b, gi, i: (blk(b, i), c_off + gi)),
            pl.BlockSpec((tr, gps * width), lambda b, gi, i: (oblk(b, i), gi)),
            col(0), col(1), col(0), col(1), row(0), row(1), row(0), row(1),
            pl.BlockSpec((1, gps * width), lambda b, gi, i: (0, gi)),
            pl.BlockSpec((1, gps * width), lambda b, gi, i: (0, gi)),
        ],
        out_specs=pl.BlockSpec((tr, gps * width), lambda b, gi, i: (oblk(b, i), gi)),
        out_shape=jax.ShapeDtypeStruct((t, inner), BF16),
        scratch_shapes=[pltpu.VMEM((gps, n, width), F32), pltpu.VMEM((gps, n, width), F32),
                        pltpu.VMEM((nb, gps, n, width), BF16)],
        compiler_params=_cparams(("arbitrary", "arbitrary", "arbitrary")),
        name="ssd_scan",
    )(xbc, xbc, xbc, zx, dtc, dtc, cuc, cuc, dtr, dtr, cur, cur, d_skip, norm_w)


def _gdn_kernel(*refs, rev, finish):
    if finish:
        (q_ref, k_ref, v_ref, bc_ref, gc_ref, gr_ref, prev_ref, z_ref, nw_ref,
         o_ref, s_ref) = refs
    else:
        q_ref, k_ref, v_ref, bc_ref, gc_ref, gr_ref, o_ref, s_ref = refs
    c = GDN_CHUNK
    hw = GDN_HEAD
    n_heads = s_ref.shape[0]
    rep = n_heads // (k_ref.shape[1] // hw)

    @pl.when(pl.program_id(2) == 0)
    def _():
        s_ref[...] = jnp.zeros_like(s_ref)

    ii = lax.broadcasted_iota(jnp.int32, (c, c), 0)
    jj = lax.broadcasted_iota(jnp.int32, (c, c), 1)
    incl = (jj >= ii) if rev else (jj <= ii)
    strict = (jj > ii) if rev else (jj < ii)
    eye = jnp.where(ii == jj, 1.0, 0.0)
    n_chunks = q_ref.shape[0] // c
    order = range(n_chunks - 1, -1, -1) if rev else range(n_chunks)
    last = 0 if rev else c - 1

    heads = range(n_heads)
    steps = c.bit_length() - 1
    qk, kk, inv, pw, qkd, res, uw = {}, {}, {}, {}, {}, {}, {}

    def rows_of(ci):
        return slice(ci * c, (ci + 1) * c)

    def gram(ci):
        for kh in range(n_heads // rep):
            k_in = k_ref[rows_of(ci), kh * hw:(kh + 1) * hw]
            both = lax.dot_general(jnp.concatenate([q_ref[rows_of(ci), kh * hw:(kh + 1) * hw], k_in], axis=0),
                                   k_in, (((1,), (1,)), ((), ())), preferred_element_type=F32)
            qk[ci, kh], kk[ci, kh] = both[:c], both[c:]

    def construct(ci, h):
        rows = rows_of(ci)
        gc1, gr1 = gc_ref[rows, h:h + 1], gr_ref[h:h + 1, rows]
        dec = jnp.where(incl, jnp.exp(jnp.where(incl, gc1 - gr1, 0.0)), 0.0)
        a = jnp.where(strict, bc_ref[rows, h:h + 1] * kk[ci, h // rep] * dec, 0.0)
        inv[ci, h] = eye - a
        qkd[ci, h] = (qk[ci, h // rep] * dec).astype(BF16)
        a16 = a.astype(BF16)
        res[ci, h] = _dot(a16, a16)

    def level(s, it):
        if s == 1:
            pw[it] = res[it].astype(BF16)
        else:
            inv[it] = inv[it] + res[it][:c]
            pw[it] = res[it][c:].astype(BF16)
        if s < steps - 1:
            res[it] = _dot(jnp.concatenate([inv[it].astype(BF16), pw[it]], axis=0), pw[it])
        else:
            res[it] = _dot(inv[it].astype(BF16), pw[it])

    def solve(ci, h):
        rows = rows_of(ci)
        bc1, gc1 = bc_ref[rows, h:h + 1], gc_ref[rows, h:h + 1]
        kh = h // rep
        rhs = jnp.concatenate(
            [(v_ref[rows, h * hw:(h + 1) * hw].astype(F32) * bc1).astype(BF16),
             (k_ref[rows, kh * hw:(kh + 1) * hw].astype(F32) * (bc1 * jnp.exp(gc1))).astype(BF16)], axis=1)
        uw[ci, h] = _dot((inv[ci, h] + res[ci, h]).astype(BF16), rhs)

    st, ws_qs, kv, intra, gtot = {}, {}, {}, {}, {}

    def state_read(ci):
        for h in heads:
            st[h] = s_ref[h]
            ws_qs[h] = _dot(jnp.concatenate([uw[ci, h][:, hw:].astype(BF16),
                                             q_ref[rows_of(ci), (h // rep) * hw:(h // rep + 1) * hw]], axis=0),
                            st[h].astype(BF16))

    def state_update(ci):
        rows = rows_of(ci)
        for h in heads:
            v_new = (uw[ci, h][:, :hw] - ws_qs[h][:c]).astype(BF16)
            gc1 = gc_ref[rows, h:h + 1]
            gtot[h] = gc1[last:last + 1, :]
            kh = h // rep
            k_dec = (k_ref[rows, kh * hw:(kh + 1) * hw].astype(F32) * jnp.exp(gtot[h] - gc1)).astype(BF16)
            kv[h] = lax.dot_general(k_dec, v_new, (((0,), (0,)), ((), ())), preferred_element_type=F32)
            intra[h] = _dot(qkd[ci, h], v_new)

    def state_write(ci):
        rows = rows_of(ci)
        for h in heads:
            s_ref[h] = st[h] * jnp.exp(gtot[h]) + kv[h]
        for h in heads:
            out = jnp.exp(gc_ref[rows, h:h + 1]) * ws_qs[h][c:] + intra[h]
            cols = slice(h * hw, (h + 1) * hw)
            if finish:
                o = out + prev_ref[rows, cols]
                o = o * lax.rsqrt(jnp.mean(o * o, axis=-1, keepdims=True) + EPS) * nw_ref[...]
                o_ref[rows, cols] = (o * _silu(z_ref[rows, cols].astype(F32))).astype(o_ref.dtype)
            else:
                o_ref[rows, cols] = out.astype(o_ref.dtype)

    pairs = [(ci, h) for ci in order for h in heads]
    for ci in order:
        gram(ci)
    for pair in pairs:
        construct(*pair)
    for s in range(1, steps):
        for pair in pairs:
            level(s, pair)
    for pair in pairs:
        solve(*pair)
    for ci in order:
        state_read(ci)
        state_update(ci)
        state_write(ci)


def _gdn_scan(cfg, qk, v, bc, gc, gr, *, rev, prev=None, qkvz=None, norm_w=None):
    t, tr, batch = cfg["t"], ROW_TILE, cfg["batch"]
    hw = GDN_HEAD
    hv = cfg["d"] // hw
    hk = hv // 2
    hps = min(GDN_HEADS_PER_STEP, hv)
    kps = hps // 2
    nhb = hv // hps
    nb = (cfg["ctx_len"] + cfg["seq"]) // tr
    finish = prev is not None
    d = 1 if rev else 0
    order = _bwd_block if rev else _fwd_block

    def blk(b, i):
        return order(cfg, b, i)

    k_off = hk // kps
    z_off = (2 * hk + hv) // hps
    col = pl.BlockSpec((None, tr, hps), lambda b, hb, i: (d * nhb + hb, blk(b, i), 0))
    row = pl.BlockSpec((None, hps, tr), lambda b, hb, i: (d * nhb + hb, 0, blk(b, i)))
    wide = lambda off: pl.BlockSpec((tr, hps * hw), lambda b, hb, i: (blk(b, i), off + hb))
    in_specs = [
        pl.BlockSpec((tr, kps * hw), lambda b, hb, i: (blk(b, i), hb)),
        pl.BlockSpec((tr, kps * hw), lambda b, hb, i: (blk(b, i), k_off + hb)),
        wide(0), col, col, row,
    ]
    ins = [qk, qk, v, bc, gc, gr]
    if finish:
        in_specs += [wide(0), wide(z_off), pl.BlockSpec((1, hw), lambda b, hb, i: (0, 0))]
        ins += [prev, qkvz, norm_w]
    return pl.pallas_call(
        functools.partial(_gdn_kernel, rev=rev, finish=finish),
        grid=(batch, nhb, nb),
        in_specs=in_specs,
        out_specs=wide(0),
        out_shape=jax.ShapeDtypeStruct((t, hv * hw), BF16),
        scratch_shapes=[pltpu.VMEM((hps, hw, hw), F32)],
        compiler_params=_cparams(("arbitrary", "arbitrary", "arbitrary")),
        name="gdn_scan_bwd" if rev else "gdn_scan_fwd",
    )(*ins)


def _pad_lanes(v, width=128):
    v = v.reshape(1, -1).astype(F32)
    return jnp.pad(v, ((0, 0), (0, width - v.shape[1])))


def _pad_cols(w, width=128):
    return jnp.pad(w, ((0, 0), (0, width - w.shape[1])))


def _tail_weights(w, layer, main):
    if w.shape[-1] - main == LANES:
        return w, layer, main
    return _pad_cols(w[layer][:, main:])[None], 0, 0


def _ssd_mixer(cfg, h, layer, w_in, conv_w, conv_b, a_log, dt_bias, d_skip, norm_w, w_out):
    d = cfg["d"]
    heads = d // SSD_HEAD_DIM
    bc_dim = SSD_GROUPS * SSD_STATE
    main = 2 * d + 2 * bc_dim
    tm = cfg["tm"]
    zx = _matmul(h, w_in, layer, 0, main, BF16, tm, _pick(main, (512, 256, 128)), "ssd_in_proj")
    w_tail, tail_layer, tail_lo = _tail_weights(w_in, layer, main)
    tail = _matmul(h, w_tail, tail_layer, tail_lo, tail_lo + LANES, F32, tm, LANES, "ssd_in_proj_dt")
    xbc = _conv_silu(cfg, zx, d, d + 2 * bc_dim, conv_w, conv_b)
    a_neg = -jnp.exp(a_log.astype(F32))
    dtc, cuc, dtr, cur = _prep(cfg, tail, _pad_lanes(dt_bias), _pad_lanes(a_neg),
                               kind="ssd", chunk=SSD_CHUNK, n_heads=heads, hps=heads // SSD_GROUPS)
    y = _ssd_scan(cfg, xbc, zx, dtc, cuc, dtr, cur,
                  jnp.repeat(d_skip.astype(F32), SSD_HEAD_DIM).reshape(1, d), norm_w.reshape(1, d))
    return _matmul(y, w_out, layer, 0, d, BF16, tm, _pick(d, (512, 256, 128)), "ssd_out_proj")


def _gdn_mixer(cfg, h, layer, w_in, conv_w, a_log, dt_bias, norm_w, w_out):
    d = cfg["d"]
    hv = d // GDN_HEAD
    key_dim = (hv // 2) * GDN_HEAD
    conv_dim = 2 * key_dim + d
    main = conv_dim + d
    tm = cfg["tm"]
    qkvz = _matmul(h, w_in, layer, 0, main, BF16, tm, _pick(main, (512, 256, 128)), "gdn_in_proj")
    w_tail, tail_layer, tail_lo = _tail_weights(w_in, layer, main)
    tail = _matmul(h, w_tail, tail_layer, tail_lo, tail_lo + LANES, F32, tm, LANES, "gdn_in_proj_gates")
    qk = _conv_silu(cfg, qkvz, 0, 2 * key_dim, conv_w[:, :2 * key_dim], jnp.zeros((2 * key_dim,), F32),
                    norm_ch=2 * key_dim, q_ch=key_dim, q_scale=GDN_HEAD ** -0.5)
    v = _conv_silu(cfg, qkvz, 2 * key_dim, d, conv_w[:, 2 * key_dim:], jnp.zeros((d,), F32))
    a_pos = jnp.exp(a_log.astype(F32)).reshape(-1)
    zeros = jnp.zeros((2 * hv,), F32)
    bias_full = _pad_lanes(jnp.concatenate([zeros, dt_bias.astype(F32).reshape(-1)]))
    amul_full = _pad_lanes(jnp.concatenate([zeros, -a_pos]))
    hps = min(GDN_HEADS_PER_STEP, hv)
    bc, gc, _, gr = _prep(cfg, tail, bias_full, amul_full, kind="gdn", chunk=GDN_CHUNK,
                           n_heads=hv, hps=hps)
    o_f = _gdn_scan(cfg, qk, v, bc, gc, gr, rev=False)
    y = _gdn_scan(cfg, qk, v, bc, gc, gr, rev=True, prev=o_f, qkvz=qkvz,
                  norm_w=norm_w.reshape(1, GDN_HEAD).astype(F32))
    return _matmul(y, w_out, layer, 0, d, BF16, tm, _pick(d, (512, 256, 128)), "gdn_out_proj")


def _ffn(cfg, h, layer, w_in, w_out_bf):
    d = cfg["d"]
    d_ff = w_in.shape[-1] // 2
    hid = _matmul_swiglu(h, w_in, layer, cfg["tm"], _pick(d_ff, (256, 128)))
    return _matmul(hid, w_out_bf, layer, 0, d, BF16, cfg["tm_out"], _pick(d, (512, 256, 128)), "ffn_out")


def kernel(x, c, ctx, c_ctx, ada_down, ada_up, ada_bias, norm_mix, norm_ffn, ffn_in, ffn_out,
           ssd_in, ssd_conv_w, ssd_conv_b, ssd_a_log, ssd_dt_bias, ssd_d, ssd_norm, ssd_out,
           gdn_in, gdn_conv_w, gdn_a_log, gdn_dt_bias, gdn_norm, gdn_out, final_norm):
    batch, seq, d = x.shape
    ctx_len = ctx.shape[1]
    depth = ada_down.shape[0]
    rows = seq // GRID_W
    assert rows == ROW_TILE and ctx_len % CONV_ROWS == 0 and seq % CONV_ROWS == 0
    t_lat, t_ctx = batch * seq, batch * ctx_len
    t = t_lat + t_ctx
    cfg = dict(d=d, batch=batch, seq=seq, ctx_len=ctx_len, cols=GRID_W, t_lat=t_lat, t_ctx=t_ctx, t=t,
               tm=_pick(t, (1536, 1024, 512, 256, 128)), tm_out=_pick(t, (512, 256, 128)))
    assert t_ctx % GRID_W == 0

    n_streams = batch + 1
    cv = jnp.concatenate([c.astype(F32), c_ctx.astype(F32)[None, :],
                          jnp.zeros((8 - n_streams % 8 if n_streams % 8 else 0, d), F32)], axis=0)
    mods = _ada_modulation(cv, ada_down, ada_up, ada_bias)

    def mod(layer, which):
        return mods[layer, :n_streams, which * d:(which + 1) * d].reshape(n_streams, 1, d)

    ffn_out_bf = ffn_out.astype(BF16)

    x_lat = x.reshape(t_lat, d)
    x_ctx = ctx.reshape(t_ctx, d)
    pending, pending_gate = None, None
    col_major = False
    for i in range(depth):
        want_col_major = (i // N_MIXERS) % 2 == 1
        assert want_col_major or not col_major
        j = i // N_MIXERS
        x_lat, x_ctx, h = _norm_modulate(cfg, x_lat, x_ctx, pending, pending_gate,
                                         norm_mix[i].reshape(1, d), mod(i, 0), mod(i, 1),
                                         to_col_major=want_col_major and not col_major)
        col_major = want_col_major
        if i % N_MIXERS == 0:
            y = _ssd_mixer(cfg, h, j, ssd_in, ssd_conv_w[j], ssd_conv_b[j], ssd_a_log[j],
                           ssd_dt_bias[j], ssd_d[j], ssd_norm[j], ssd_out)
        else:
            y = _gdn_mixer(cfg, h, j, gdn_in, gdn_conv_w[j], gdn_a_log[j], gdn_dt_bias[j],
                           gdn_norm[j], gdn_out)
        x_lat, x_ctx, h = _norm_modulate(cfg, x_lat, x_ctx, y, mod(i, 2),
                                         norm_ffn[i].reshape(1, d), mod(i, 3), mod(i, 4),
                                         to_col_major=False)
        pending, pending_gate = _ffn(cfg, h, i, ffn_in, ffn_out_bf), mod(i, 5)
    out = _final_norm(cfg, x_lat, pending, pending_gate, final_norm.reshape(1, d),
                      from_col_major=col_major)
    return out.reshape(batch, seq, d)
```

```python
import functools

import jax
import jax.numpy as jnp
from jax import lax
from jax.experimental import pallas as pl
from jax.experimental.pallas import tpu as pltpu

F32 = jnp.float32
BF16 = jnp.bfloat16
EPS = 1e-6

GRID_W = 64
N_MIXERS = 2
N_MOD = 6
SSD_HEAD_DIM = 64
SSD_GROUPS = 8
SSD_STATE = 128
SSD_CHUNK = 128
SSD_GROUPS_PER_STEP = 4
GDN_HEAD = 128
GDN_CHUNK = 64
GDN_HEADS_PER_STEP = 8

ROW_TILE = 128
NORM_ROWS = 256
CONV_ROWS = 256
HALO = 8
LANES = 128
CONV_PITCH = 36
CONV_SLAB_UNROLL = 2
V7X_VMEM_LIMIT = 56 * 1024 * 1024


def _cparams(sem, vmem=None):
    return pltpu.CompilerParams(dimension_semantics=sem, vmem_limit_bytes=vmem or V7X_VMEM_LIMIT)


def _sigmoid(x):
    return 1.0 / (1.0 + jnp.exp(-x))


def _silu(x):
    h = 0.5 * x
    return h + h * jnp.tanh(h)


def _softplus(x):
    return jnp.maximum(x, 0.0) + jnp.log1p(jnp.exp(-jnp.abs(x)))


def _pick(n, candidates):
    for c in candidates:
        if n % c == 0:
            return c
    raise ValueError(f"no tile for {n} among {candidates}")


def _split3(a):
    hi = a.astype(BF16).astype(F32)
    r1 = a - hi
    mid = r1.astype(BF16).astype(F32)
    lo = (r1 - mid).astype(BF16).astype(F32)
    return hi, mid, lo


def _dot(a, b):
    return jnp.dot(a, b, preferred_element_type=F32)


def _dot_exact_rhs(m, a):
    hi, mid, lo = _split3(a)
    mb = m.astype(BF16)
    return _dot(mb, hi.astype(BF16)) + _dot(mb, mid.astype(BF16)) + _dot(mb, lo.astype(BF16))


def _ada_kernel(cv_ref, down_ref, up_ref, bias_ref, o_ref):
    t = jnp.dot(_silu(cv_ref[...]), down_ref[...], preferred_element_type=F32,
                precision=lax.Precision.HIGHEST)
    o_ref[...] = jnp.dot(t, up_ref[...], preferred_element_type=F32,
                         precision=lax.Precision.HIGHEST) + bias_ref[...]


def _ada_modulation(cv, down, up, bias):
    depth, d, r = down.shape
    n = up.shape[-1]
    tn = _pick(n, (6144, 3072, 1536, 768, 512, 256, 128))
    rows = cv.shape[0]
    return pl.pallas_call(
        _ada_kernel,
        grid=(depth, n // tn),
        in_specs=[
            pl.BlockSpec((rows, d), lambda l, j: (0, 0)),
            pl.BlockSpec((None, d, r), lambda l, j: (l, 0, 0)),
            pl.BlockSpec((None, r, tn), lambda l, j: (l, 0, j)),
            pl.BlockSpec((None, 1, tn), lambda l, j: (l, 0, j)),
        ],
        out_specs=pl.BlockSpec((None, rows, tn), lambda l, j: (l, 0, j)),
        out_shape=jax.ShapeDtypeStruct((depth, rows, n), F32),
        compiler_params=_cparams(("arbitrary", "arbitrary")),
        name="ada_modulation",
    )(cv, down, up, bias.reshape(depth, 1, n))


def _norm_kernel(*refs, has_y, n_lat_tiles):
    if has_y:
        (xl_ref, xc_ref, yl_ref, yc_ref, g_ref, w_ref, sh_ref, sc_ref,
         xlo_ref, xco_ref, h_ref) = refs
    else:
        xl_ref, xc_ref, w_ref, sh_ref, sc_ref, h_ref = refs

    def run(x_ref, y_ref, xo_ref):
        x = x_ref[...]
        if has_y:
            x = x + g_ref[...] * y_ref[...]
            xo_ref[...] = x
        xn = x * lax.rsqrt(jnp.mean(x * x, axis=-1, keepdims=True) + EPS) * w_ref[...]
        h_ref[...] = (xn * (1.0 + sc_ref[...]) + sh_ref[...]).astype(h_ref.dtype)

    i = pl.program_id(0)

    @pl.when(i < n_lat_tiles)
    def _():
        run(xl_ref, yl_ref if has_y else None, xlo_ref if has_y else None)

    @pl.when(i >= n_lat_tiles)
    def _():
        run(xc_ref, yc_ref if has_y else None, xco_ref if has_y else None)


def _norm_modulate(cfg, x_lat, x_ctx, y, gate, w, shift, scale, *, to_col_major):
    d = cfg["d"]
    tr = ROW_TILE if to_col_major else NORM_ROWS
    assert cfg["seq"] % tr == 0 and cfg["t_ctx"] % tr == 0
    n_lat, n_ctx = cfg["t_lat"] // tr, cfg["t_ctx"] // tr
    per_batch = cfg["seq"] // tr
    cols = cfg["cols"]
    has_y = y is not None

    def lat_tile(i):
        return jnp.minimum(i, n_lat - 1)

    def ctx_tile(i):
        return jnp.maximum(i - n_lat, 0)

    def stream(i):
        return jnp.where(i < n_lat, i // per_batch, cfg["batch"])

    if to_col_major:
        def strided(i):
            il = lat_tile(i)
            return (il // cols, il % cols)
        xl_in = x_lat.reshape(cfg["t_lat"] // cols, cols * d)
        xl_spec = pl.BlockSpec((tr, d), strided)
    else:
        xl_in = x_lat
        xl_spec = pl.BlockSpec((tr, d), lambda i: (lat_tile(i), 0))
    xc_spec = pl.BlockSpec((tr, d), lambda i: (ctx_tile(i), 0))
    mod_spec = pl.BlockSpec((None, 1, d), lambda i: (stream(i), 0, 0))
    w_spec = pl.BlockSpec((1, d), lambda i: (0, 0))
    h_spec = pl.BlockSpec((tr, d), lambda i: (i, 0))
    h_shape = jax.ShapeDtypeStruct((cfg["t"], d), BF16)

    if has_y:
        if to_col_major:
            yl_in = y.reshape(cfg["t"] // cols, cols * d)
            yl_spec = pl.BlockSpec((tr, d), strided)
        else:
            yl_in = y
            yl_spec = pl.BlockSpec((tr, d), lambda i: (lat_tile(i), 0))
        yc_spec = pl.BlockSpec((tr, d), lambda i: (n_lat + ctx_tile(i), 0))
        ins = (xl_in, x_ctx, yl_in, y, gate, w, shift, scale)
        in_specs = [xl_spec, xc_spec, yl_spec, yc_spec, mod_spec, w_spec, mod_spec, mod_spec]
        out_specs = [pl.BlockSpec((tr, d), lambda i: (lat_tile(i), 0)), xc_spec, h_spec]
        out_shape = [jax.ShapeDtypeStruct((cfg["t_lat"], d), F32),
                     jax.ShapeDtypeStruct((cfg["t_ctx"], d), F32), h_shape]
    else:
        ins = (xl_in, x_ctx, w, shift, scale)
        in_specs = [xl_spec, xc_spec, w_spec, mod_spec, mod_spec]
        out_specs = h_spec
        out_shape = h_shape
    out = pl.pallas_call(
        functools.partial(_norm_kernel, has_y=has_y, n_lat_tiles=n_lat),
        grid=(n_lat + n_ctx,),
        in_specs=in_specs, out_specs=out_specs, out_shape=out_shape,
        compiler_params=_cparams(("arbitrary",)),
        name="norm_modulate",
    )(*ins)
    if has_y:
        return out
    return x_lat, x_ctx, out


def _final_kernel(x_ref, y_ref, g_ref, w_ref, o_ref):
    x = x_ref[...] + g_ref[...] * y_ref[...]
    o_ref[...] = x * lax.rsqrt(jnp.mean(x * x, axis=-1, keepdims=True) + EPS) * w_ref[...]


def _final_norm(cfg, x_lat, y, gate, w, *, from_col_major):
    d, tr = cfg["d"], ROW_TILE
    n_lat = cfg["t_lat"] // tr
    per_batch = cfg["seq"] // tr
    cols = cfg["cols"]
    plain = pl.BlockSpec((tr, d), lambda i: (i, 0))
    if from_col_major:
        out_spec = pl.BlockSpec((tr, d), lambda i: (i // cols, i % cols))
        out_shape = jax.ShapeDtypeStruct((cfg["t_lat"] // cols, cols * d), F32)
    else:
        out_spec, out_shape = plain, jax.ShapeDtypeStruct((cfg["t_lat"], d), F32)
    out = pl.pallas_call(
        _final_kernel,
        grid=(n_lat,),
        in_specs=[plain, plain,
                  pl.BlockSpec((None, 1, d), lambda i: (i // per_batch, 0, 0)),
                  pl.BlockSpec((1, d), lambda i: (0, 0))],
        out_specs=out_spec, out_shape=out_shape,
        compiler_params=_cparams(("arbitrary",)),
        name="final_norm",
    )(x_lat, y, gate, w)
    return out.reshape(cfg["t_lat"], d)


def _mm_kernel(a_ref, w_ref, o_ref):
    o_ref[...] = _dot(a_ref[...], w_ref[...].astype(BF16)).astype(o_ref.dtype)


def _matmul(a, w, layer, col_lo, col_hi, out_dtype, tm, tn, name):
    m, k = a.shape
    n = col_hi - col_lo
    assert m % tm == 0 and n % tn == 0 and col_lo % tn == 0
    off = col_lo // tn
    return pl.pallas_call(
        _mm_kernel,
        grid=(m // tm, n // tn),
        in_specs=[pl.BlockSpec((tm, k), lambda i, j: (i, 0)),
                  pl.BlockSpec((None, k, tn), lambda i, j: (layer, 0, off + j))],
        out_specs=pl.BlockSpec((tm, tn), lambda i, j: (i, j)),
        out_shape=jax.ShapeDtypeStruct((m, n), out_dtype),
        compiler_params=_cparams(("arbitrary", "arbitrary")),
        name=name,
    )(a, w)


def _swiglu_kernel(a_ref, wg_ref, wu_ref, o_ref):
    a = a_ref[...]
    g = _dot(a, wg_ref[...].astype(BF16))
    u = _dot(a, wu_ref[...].astype(BF16))
    o_ref[...] = (_silu(g) * u).astype(o_ref.dtype)


def _matmul_swiglu(a, w, layer, tm, tn):
    m, k = a.shape
    d_ff = w.shape[-1] // 2
    assert m % tm == 0 and d_ff % tn == 0
    nj = d_ff // tn
    return pl.pallas_call(
        _swiglu_kernel,
        grid=(m // tm, nj),
        in_specs=[pl.BlockSpec((tm, k), lambda i, j: (i, 0)),
                  pl.BlockSpec((None, k, tn), lambda i, j: (layer, 0, j)),
                  pl.BlockSpec((None, k, tn), lambda i, j: (layer, 0, nj + j))],
        out_specs=pl.BlockSpec((tm, tn), lambda i, j: (i, j)),
        out_shape=jax.ShapeDtypeStruct((m, d_ff), BF16),
        compiler_params=_cparams(("arbitrary", "arbitrary")),
        name="ffn_in_swiglu",
    )(a, w, w)


def _conv_kernel(prev_ref, cur_ref, next_ref, w_ref, b_ref, o_ref, scr_ref, out_ref, *,
                 tiles_per_lat_seq, n_lat_tiles, tiles_per_ctx_seq, n_norm_tiles, n_q_tiles, q_scale):
    i, j = pl.program_id(0), pl.program_id(1)
    ts = cur_ref.shape[0]
    ksz = w_ref.shape[0]
    is_lat = i < n_lat_tiles
    pos = jnp.where(is_lat, i % tiles_per_lat_seq, (i - n_lat_tiles) % tiles_per_ctx_seq)
    per_seq = jnp.where(is_lat, tiles_per_lat_seq, tiles_per_ctx_seq)
    keep_prev = jnp.where(pos == 0, 0.0, 1.0)
    keep_next = jnp.where(pos == per_seq - 1, 0.0, 1.0)
    if n_norm_tiles:
        normed = j < n_norm_tiles
        scale = jnp.where(j < n_q_tiles, q_scale, 1.0)
    first = HALO - ksz // 2
    n_slabs = cur_ref.shape[1] // LANES

    def one_slab(sl):
        lanes = pl.ds(pl.multiple_of(sl * LANES, LANES), LANES)
        scr_ref[sl, 0:HALO, :] = prev_ref[:, lanes].astype(F32) * keep_prev
        scr_ref[sl, HALO:HALO + ts, :] = cur_ref[:, lanes].astype(F32)
        scr_ref[sl, HALO + ts:2 * HALO + ts, :] = next_ref[:, lanes].astype(F32) * keep_next
        scr_ref[sl, 2 * HALO + ts:, :] = jnp.zeros((scr_ref.shape[1] - 2 * HALO - ts, LANES), F32)
        wk = [jnp.broadcast_to(w_ref[k:k + 1, lanes], (8, LANES)) for k in range(ksz)]
        bias = jnp.broadcast_to(b_ref[:, lanes], (8, LANES))
        taps = [scr_ref[sl, pl.ds(first + g, 8, stride=CONV_PITCH), :]
                for g in range(CONV_PITCH + ksz - 1)]
        for g in range(CONV_PITCH):
            acc = bias + wk[0] * taps[g]
            for k in range(1, ksz):
                acc = acc + wk[k] * taps[g + k]
            y = _silu(acc)
            if n_norm_tiles:
                r = lax.rsqrt(jnp.sum(y * y, axis=-1, keepdims=True) + EPS) * scale
                y = y * jnp.where(normed, r, 1.0)
            out_ref[sl, pl.ds(g, 8, stride=CONV_PITCH), :] = y
        o_ref[:, lanes] = out_ref[sl, 0:ts, :].astype(o_ref.dtype)

    @pl.loop(0, n_slabs // CONV_SLAB_UNROLL)
    def _(it):
        for u in range(CONV_SLAB_UNROLL):
            one_slab(it * CONV_SLAB_UNROLL + u)


def _conv_silu(cfg, src, col_lo, n_ch, w, b, *, norm_ch=0, q_ch=0, q_scale=1.0):
    t, ts = cfg["t"], CONV_ROWS
    tc = next(c for c in (2048, 1024, 512, 256)
              if all(v % c == 0 for v in (n_ch, col_lo, norm_ch, q_ch)))
    assert (tc // LANES) % CONV_SLAB_UNROLL == 0
    assert cfg["seq"] % ts == 0 and cfg["ctx_len"] % ts == 0 and 8 * CONV_PITCH >= ts
    ksz = w.shape[0]
    last_row = (HALO - ksz // 2) + (CONV_PITCH + ksz - 2) + 7 * CONV_PITCH
    in_rows = -(-(last_row + 1) // 8) * 8
    assert in_rows >= ts + 2 * HALO
    off = col_lo // tc
    hb = ts // HALO
    last_hb = t // HALO - 1
    kern = functools.partial(
        _conv_kernel, tiles_per_lat_seq=cfg["seq"] // ts, n_lat_tiles=cfg["t_lat"] // ts,
        tiles_per_ctx_seq=cfg["ctx_len"] // ts, n_norm_tiles=norm_ch // tc, n_q_tiles=q_ch // tc,
        q_scale=q_scale)
    return pl.pallas_call(
        kern,
        grid=(t // ts, n_ch // tc),
        in_specs=[
            pl.BlockSpec((HALO, tc), lambda i, j: (jnp.maximum(i * hb - 1, 0), off + j)),
            pl.BlockSpec((ts, tc), lambda i, j: (i, off + j)),
            pl.BlockSpec((HALO, tc), lambda i, j: (jnp.minimum((i + 1) * hb, last_hb), off + j)),
            pl.BlockSpec((w.shape[0], tc), lambda i, j: (0, j)),
            pl.BlockSpec((1, tc), lambda i, j: (0, j)),
        ],
        out_specs=pl.BlockSpec((ts, tc), lambda i, j: (i, j)),
        out_shape=jax.ShapeDtypeStruct((t, n_ch), BF16),
        scratch_shapes=[pltpu.VMEM((tc // LANES, in_rows, LANES), F32),
                        pltpu.VMEM((tc // LANES, 8 * CONV_PITCH, LANES), F32)],
        compiler_params=_cparams(("arbitrary", "arbitrary")),
        name="conv_silu",
    )(src, src, src, w, b.reshape(1, n_ch))


def _prep_kernel(tail_ref, bias_ref, amul_ref, p1c_ref, p2c_ref, p1r_ref, p2r_ref, *,
                 kind, chunk, p2_off, split_lane, hps):
    tr = ROW_TILE
    row = lax.broadcasted_iota(jnp.int32, (tr, tr), 0)
    col = lax.broadcasted_iota(jnp.int32, (tr, tr), 1)
    same = (row // chunk) == (col // chunk)
    lower = jnp.where(same & (col <= row), 1.0, 0.0)
    upper = jnp.where(same & (col >= row), 1.0, 0.0)
    lane = lax.broadcasted_iota(jnp.int32, (tr, LANES), 1)
    for sub in range(tail_ref.shape[0] // tr):
        rows = slice(sub * tr, (sub + 1) * tr)
        raw = tail_ref[rows, :]
        sp = _softplus(raw + bias_ref[...])
        p1 = sp if kind == "ssd" else _sigmoid(raw)
        a = sp * amul_ref[...]
        cum = jnp.where(lane < split_lane, _dot_exact_rhs(lower, a), _dot_exact_rhs(upper, a))
        p1t = p1.T
        cumt = cum.T
        for n in range(p1c_ref.shape[0]):
            p1c_ref[n, rows, :] = p1[:, n * hps:(n + 1) * hps]
            p2c_ref[n, rows, :] = cum[:, p2_off + n * hps:p2_off + (n + 1) * hps]
            p1r_ref[n, :, rows] = p1t[n * hps:(n + 1) * hps, :]
            p2r_ref[n, :, rows] = cumt[p2_off + n * hps:p2_off + (n + 1) * hps, :]


def _prep(cfg, tail, bias_full, amul_full, *, kind, chunk, n_heads, hps):
    t = cfg["t"]
    tr = _pick(t, (4 * ROW_TILE, 2 * ROW_TILE, ROW_TILE))
    nblk = 2 * n_heads // hps
    p2_off = 0 if kind == "ssd" else 2 * n_heads
    split_lane = p2_off + n_heads
    kern = functools.partial(_prep_kernel, kind=kind, chunk=chunk, p2_off=p2_off,
                             split_lane=split_lane, hps=hps)
    col_spec = pl.BlockSpec((nblk, tr, hps), lambda i: (0, i, 0))
    row_spec = pl.BlockSpec((nblk, hps, tr), lambda i: (0, 0, i))
    col_shape = jax.ShapeDtypeStruct((nblk, t, hps), F32)
    row_shape = jax.ShapeDtypeStruct((nblk, hps, t), F32)
    return pl.pallas_call(
        kern,
        grid=(t // tr,),
        in_specs=[pl.BlockSpec((tr, 128), lambda i: (i, 0)),
                  pl.BlockSpec((1, 128), lambda i: (0, 0)),
                  pl.BlockSpec((1, 128), lambda i: (0, 0))],
        out_specs=[col_spec, col_spec, row_spec, row_spec],
        out_shape=[col_shape, col_shape, row_shape, row_shape],
        compiler_params=_cparams(("arbitrary",)),
        name=f"prep_{kind}",
    )(tail, bias_full, amul_full)


def _fwd_block(cfg, b, i):
    ncc, ncl = cfg["ctx_len"] // ROW_TILE, cfg["seq"] // ROW_TILE
    ctx = cfg["batch"] * ncl + b * ncc + i
    lat = b * ncl + (i - ncc)
    return jnp.where(i < ncc, ctx, lat)


def _bwd_block(cfg, b, i):
    ncc, ncl = cfg["ctx_len"] // ROW_TILE, cfg["seq"] // ROW_TILE
    ctx = cfg["batch"] * ncl + b * ncc + (ncc - 1 - i)
    lat = b * ncl + (ncl - 1 - (i - ncc))
    return jnp.where(i < ncc, ctx, lat)


def _ssd_kernel(x_ref, b_ref, c_ref, z_ref, dtc_f_ref, dtc_b_ref, cuc_f_ref, cuc_b_ref,
                dtr_f_ref, dtr_b_ref, cur_f_ref, cur_b_ref, dskip_ref, nw_ref,
                o_ref, sf_ref, sb_ref, store_ref, *, nb, ncc):
    i = pl.program_id(2)
    q = x_ref.shape[0]
    gps = dtc_f_ref.shape[0]
    hpg = dtc_f_ref.shape[-1]
    width = x_ref.shape[1] // gps
    n = b_ref.shape[1] // gps
    hd = width // hpg
    groups = range(gps)
    expand = jnp.where(lax.broadcasted_iota(jnp.int32, (hpg, width), 1) // hd
                       == lax.broadcasted_iota(jnp.int32, (hpg, width), 0), 1.0, 0.0).astype(BF16)

    def widen(a):
        hi = a.astype(BF16)
        lo = (a - hi.astype(F32)).astype(BF16)
        return _dot(hi, expand) + _dot(lo, expand)

    def cols(gg, w):
        return slice(gg * w, (gg + 1) * w)

    @pl.when(i == 0)
    def _():
        sb_ref[...] = jnp.zeros_like(sb_ref)

    @pl.when(i == nb)
    def _():
        sf_ref[...] = jnp.zeros_like(sf_ref)

    @pl.when(i < nb)
    def _():
        for gg in groups:
            cub = cuc_b_ref[gg]
            tot = cub[0:1, :]
            wide = widen(jnp.concatenate([dtc_b_ref[gg] * jnp.exp(tot - cub), jnp.exp(cub[0:8, :])], axis=0))
            wgt, dec = wide[:q], wide[q:q + 1]
            xs = (x_ref[:, cols(gg, width)].astype(F32) * wgt).astype(BF16)
            cs = lax.dot_general(b_ref[:, cols(gg, n)], xs, (((0,), (0,)), ((), ())),
                                 preferred_element_type=F32)
            store_ref[i, gg] = sb_ref[gg].astype(BF16)
            sb_ref[gg] = sb_ref[gg] * dec + cs

    @pl.when(i >= nb)
    def _():
        i2 = i - nb
        slot = jnp.where(i2 < ncc, ncc - 1 - i2, nb - 1 - (i2 - ncc))
        li = lax.broadcasted_iota(jnp.int32, (q, q), 0)
        si = lax.broadcasted_iota(jnp.int32, (q, q), 1)
        causal = si <= li
        lane = lax.broadcasted_iota(jnp.int32, (q, 2 * hd), 1)
        cb = [lax.dot_general(c_ref[:, cols(gg, n)], b_ref[:, cols(gg, n)], (((1,), (1,)), ((), ())),
                              preferred_element_type=F32) for gg in groups]
        wide = []
        for gg in groups:
            cuf = cuc_f_ref[gg]
            wide.append(widen(jnp.concatenate(
                [jnp.exp(cuf), jnp.exp(cuc_b_ref[gg]), dtc_f_ref[gg] * jnp.exp(cuf[q - 1:q, :] - cuf)], axis=0)))
        y_off = []
        for gg in groups:
            ef, eb, wf = wide[gg][:q], wide[gg][q:2 * q], wide[gg][2 * q:]
            cm = c_ref[:, cols(gg, n)]
            y_off.append(_dot(cm, sf_ref[gg].astype(BF16)) * ef + _dot(cm, store_ref[slot, gg]) * eb)
            xs = (x_ref[:, cols(gg, width)].astype(F32) * wf).astype(BF16)
            cs = lax.dot_general(b_ref[:, cols(gg, n)], xs, (((0,), (0,)), ((), ())),
                                 preferred_element_type=F32)
            sf_ref[gg] = sf_ref[gg] * ef[q - 1:q, :] + cs

        y_diag = [[] for _ in groups]
        for p in range(hpg // 2):
            for gg in groups:
                xp = x_ref[:, gg * width + p * 2 * hd:gg * width + (p + 1) * 2 * hd]
                acc = None
                for half in range(2):
                    r = 2 * p + half
                    log_dec = jnp.where(causal, cuc_f_ref[gg, :, r:r + 1] - cur_f_ref[gg, r:r + 1, :],
                                        cuc_b_ref[gg, :, r:r + 1] - cur_b_ref[gg, r:r + 1, :])
                    dt_row = jnp.where(causal, dtr_f_ref[gg, r:r + 1, :], dtr_b_ref[gg, r:r + 1, :])
                    mix = (cb[gg] * (jnp.exp(log_dec) * dt_row)).astype(BF16)
                    keep = (lane < hd) if half == 0 else (lane >= hd)
                    part = _dot(mix, jnp.where(keep, xp, jnp.zeros_like(xp)))
                    acc = part if acc is None else acc + part
                y_diag[gg].append(acc)

        for gg in groups:
            cb_diag = jnp.sum(jnp.where(si == li, cb[gg], 0.0), axis=-1, keepdims=True)
            own_b = widen(dtc_b_ref[gg] * cb_diag)
            xf = x_ref[:, cols(gg, width)].astype(F32)
            y = (dskip_ref[:, cols(gg, width)] + own_b) * xf + y_off[gg] + jnp.concatenate(y_diag[gg], axis=1)
            gated = y * _silu(z_ref[:, cols(gg, width)].astype(F32))
            o_ref[:, cols(gg, width)] = (
                gated * lax.rsqrt(jnp.mean(gated * gated, axis=-1, keepdims=True) + EPS)
                * nw_ref[:, cols(gg, width)]).astype(o_ref.dtype)


def _ssd_scan(cfg, xbc, zx, dtc, cuc, dtr, cur, d_skip, norm_w):
    t, tr, batch = cfg["t"], ROW_TILE, cfg["batch"]
    g, n = SSD_GROUPS, SSD_STATE
    gps = SSD_GROUPS_PER_STEP if g % SSD_GROUPS_PER_STEP == 0 else 1
    inner = cfg["d"]
    width = inner // g
    hpg = width // SSD_HEAD_DIM
    assert hpg % 2 == 0 and tr == SSD_CHUNK and inner % (gps * n) == 0
    ncc, ncl = cfg["ctx_len"] // tr, cfg["seq"] // tr
    nb = ncc + ncl
    b_off, c_off = inner // (gps * n), (inner + g * n) // (gps * n)

    def blk(b, i):
        return jnp.where(i < nb, _bwd_block(cfg, b, i), _fwd_block(cfg, b, i - nb))

    def oblk(b, i):
        return _fwd_block(cfg, b, jnp.maximum(i - nb, 0))

    col = lambda d: pl.BlockSpec((gps, tr, hpg), lambda b, gi, i: (d * (g // gps) + gi, blk(b, i), 0))
    row = lambda d: pl.BlockSpec((gps, hpg, tr), lambda b, gi, i: (d * (g // gps) + gi, 0, blk(b, i)))
    kern = functools.partial(_ssd_kernel, nb=nb, ncc=ncc)
    return pl.pallas_call(
        kern,
        grid=(batch, g // gps, 2 * nb),
        in_specs=[
            pl.BlockSpec((tr, gps * width), lambda b, gi, i: (blk(b, i), gi)),
            pl.BlockSpec((tr, gps * n), lambda b, gi, i: (blk(b, i), b_off + gi)),
            pl.BlockSpec((tr, gps * n), lambda b, gi, i: (blk(b, i), c_off + gi)),
            pl.BlockSpec((tr, gps * width), lambda b, gi, i: (oblk(b, i), gi)),
            col(0), col(1), col(0), col(1), row(0), row(1), row(0), row(1),
            pl.BlockSpec((1, gps * width), lambda b, gi, i: (0, gi)),
            pl.BlockSpec((1, gps * width), lambda b, gi, i: (0, gi)),
        ],
        out_specs=pl.BlockSpec((tr, gps * width), lambda b, gi, i: (oblk(b, i), gi)),
        out_shape=jax.ShapeDtypeStruct((t, inner), BF16),
        scratch_shapes=[pltpu.VMEM((gps, n, width), F32), pltpu.VMEM((gps, n, width), F32),
                        pltpu.VMEM((nb, gps, n, width), BF16)],
        compiler_params=_cparams(("arbitrary", "arbitrary", "arbitrary")),
        name="ssd_scan",
    )(xbc, xbc, xbc, zx, dtc, dtc, cuc, cuc, dtr, dtr, cur, cur, d_skip, norm_w)


def _gdn_kernel(*refs, rev, finish):
    if finish:
        (q_ref, k_ref, v_ref, bc_ref, gc_ref, gr_ref, prev_ref, z_ref, nw_ref,
         o_ref, s_ref) = refs
    else:
        q_ref, k_ref, v_ref, bc_ref, gc_ref, gr_ref, o_ref, s_ref = refs
    c = GDN_CHUNK
    hw = GDN_HEAD
    n_heads = s_ref.shape[0]
    rep = n_heads // (k_ref.shape[1] // hw)

    @pl.when(pl.program_id(2) == 0)
    def _():
        s_ref[...] = jnp.zeros_like(s_ref)

    ii = lax.broadcasted_iota(jnp.int32, (c, c), 0)
    jj = lax.broadcasted_iota(jnp.int32, (c, c), 1)
    incl = (jj >= ii) if rev else (jj <= ii)
    strict = (jj > ii) if rev else (jj < ii)
    eye = jnp.where(ii == jj, 1.0, 0.0)
    n_chunks = q_ref.shape[0] // c
    order = range(n_chunks - 1, -1, -1) if rev else range(n_chunks)
    last = 0 if rev else c - 1

    heads = range(n_heads)
    steps = c.bit_length() - 1
    qk, kk, inv, pw, qkd, res, uw = {}, {}, {}, {}, {}, {}, {}

    def rows_of(ci):
        return slice(ci * c, (ci + 1) * c)

    def gram(ci):
        for kh in range(n_heads // rep):
            k_in = k_ref[rows_of(ci), kh * hw:(kh + 1) * hw]
            both = lax.dot_general(jnp.concatenate([q_ref[rows_of(ci), kh * hw:(kh + 1) * hw], k_in], axis=0),
                                   k_in, (((1,), (1,)), ((), ())), preferred_element_type=F32)
            qk[ci, kh], kk[ci, kh] = both[:c], both[c:]

    sizes = [2 << k for k in range(steps)]
    joins = {b: ((ii // b) == (jj // b)) & ((ii // (b // 2)) != (jj // (b // 2))) for b in sizes}

    def construct(ci, h):
        rows = rows_of(ci)
        gc1, gr1 = gc_ref[rows, h:h + 1], gr_ref[h:h + 1, rows]
        dec = jnp.where(incl, jnp.exp(jnp.where(incl, gc1 - gr1, 0.0)), 0.0)
        a = jnp.where(strict, bc_ref[rows, h:h + 1] * kk[ci, h // rep] * dec, 0.0)
        pw[ci, h] = a
        inv[ci, h] = eye - jnp.where(joins[sizes[0]], a, 0.0)
        qkd[ci, h] = (qk[ci, h // rep] * dec).astype(BF16)
        res[ci, h] = None

    def level(s, it):
        b = sizes[(s + 1) // 2]
        if s % 2 == 1:
            if res[it] is not None:
                inv[it] = inv[it] - res[it]
            res[it] = _dot(inv[it].astype(BF16), jnp.where(joins[b], pw[it], 0.0).astype(BF16))
        else:
            res[it] = _dot(res[it].astype(BF16), inv[it].astype(BF16))

    def solve(ci, h):
        rows = rows_of(ci)
        bc1, gc1 = bc_ref[rows, h:h + 1], gc_ref[rows, h:h + 1]
        kh = h // rep
        rhs = jnp.concatenate(
            [(v_ref[rows, h * hw:(h + 1) * hw].astype(F32) * bc1).astype(BF16),
             (k_ref[rows, kh * hw:(kh + 1) * hw].astype(F32) * (bc1 * jnp.exp(gc1))).astype(BF16)], axis=1)
        uw[ci, h] = _dot((inv[ci, h] - res[ci, h]).astype(BF16), rhs)

    st, ws_qs, kv, intra, gtot = {}, {}, {}, {}, {}

    def state_read(ci):
        for h in heads:
            st[h] = s_ref[h]
            ws_qs[h] = _dot(jnp.concatenate([uw[ci, h][:, hw:].astype(BF16),
                                             q_ref[rows_of(ci), (h // rep) * hw:(h // rep + 1) * hw]], axis=0),
                            st[h].astype(BF16))

    def state_update(ci):
        rows = rows_of(ci)
        for h in heads:
            v_new = (uw[ci, h][:, :hw] - ws_qs[h][:c]).astype(BF16)
            gc1 = gc_ref[rows, h:h + 1]
            gtot[h] = gc1[last:last + 1, :]
            kh = h // rep
            k_dec = (k_ref[rows, kh * hw:(kh + 1) * hw].astype(F32) * jnp.exp(gtot[h] - gc1)).astype(BF16)
            kv[h] = lax.dot_general(k_dec, v_new, (((0,), (0,)), ((), ())), preferred_element_type=F32)
            intra[h] = _dot(qkd[ci, h], v_new)

    def state_write(ci):
        rows = rows_of(ci)
        for h in heads:
            s_ref[h] = st[h] * jnp.exp(gtot[h]) + kv[h]
        for h in heads:
            out = jnp.exp(gc_ref[rows, h:h + 1]) * ws_qs[h][c:] + intra[h]
            cols = slice(h * hw, (h + 1) * hw)
            if finish:
                o = out + prev_ref[rows, cols]
                o = o * lax.rsqrt(jnp.mean(o * o, axis=-1, keepdims=True) + EPS) * nw_ref[...]
                o_ref[rows, cols] = (o * _silu(z_ref[rows, cols].astype(F32))).astype(o_ref.dtype)
            else:
                o_ref[rows, cols] = out.astype(o_ref.dtype)

    pairs = [(ci, h) for ci in order for h in heads]
    for ci in order:
        gram(ci)
    for pair in pairs:
        construct(*pair)
    for s in range(1, 2 * (steps - 1) + 1):
        for pair in pairs:
            level(s, pair)
    for pair in pairs:
        solve(*pair)
    for ci in order:
        state_read(ci)
        state_update(ci)
        state_write(ci)


def _gdn_scan(cfg, qk, v, bc, gc, gr, *, rev, prev=None, qkvz=None, norm_w=None):
    t, tr, batch = cfg["t"], ROW_TILE, cfg["batch"]
    hw = GDN_HEAD
    hv = cfg["d"] // hw
    hk = hv // 2
    hps = min(GDN_HEADS_PER_STEP, hv)
    kps = hps // 2
    nhb = hv // hps
    nb = (cfg["ctx_len"] + cfg["seq"]) // tr
    finish = prev is not None
    d = 1 if rev else 0
    order = _bwd_block if rev else _fwd_block

    def blk(b, i):
        return order(cfg, b, i)

    k_off = hk // kps
    z_off = (2 * hk + hv) // hps
    col = pl.BlockSpec((None, tr, hps), lambda b, hb, i: (d * nhb + hb, blk(b, i), 0))
    row = pl.BlockSpec((None, hps, tr), lambda b, hb, i: (d * nhb + hb, 0, blk(b, i)))
    wide = lambda off: pl.BlockSpec((tr, hps * hw), lambda b, hb, i: (blk(b, i), off + hb))
    in_specs = [
        pl.BlockSpec((tr, kps * hw), lambda b, hb, i: (blk(b, i), hb)),
        pl.BlockSpec((tr, kps * hw), lambda b, hb, i: (blk(b, i), k_off + hb)),
        wide(0), col, col, row,
    ]
    ins = [qk, qk, v, bc, gc, gr]
    if finish:
        in_specs += [wide(0), wide(z_off), pl.BlockSpec((1, hw), lambda b, hb, i: (0, 0))]
        ins += [prev, qkvz, norm_w]
    return pl.pallas_call(
        functools.partial(_gdn_kernel, rev=rev, finish=finish),
        grid=(batch, nhb, nb),
        in_specs=in_specs,
        out_specs=wide(0),
        out_shape=jax.ShapeDtypeStruct((t, hv * hw), BF16),
        scratch_shapes=[pltpu.VMEM((hps, hw, hw), F32)],
        compiler_params=_cparams(("arbitrary", "arbitrary", "arbitrary")),
        name="gdn_scan_bwd" if rev else "gdn_scan_fwd",
    )(*ins)


def _pad_lanes(v, width=128):
    v = v.reshape(1, -1).astype(F32)
    return jnp.pad(v, ((0, 0), (0, width - v.shape[1])))


def _pad_cols(w, width=128):
    return jnp.pad(w, ((0, 0), (0, width - w.shape[1])))


def _tail_weights(w, layer, main):
    if w.shape[-1] - main == LANES:
        return w, layer, main
    return _pad_cols(w[layer][:, main:])[None], 0, 0


def _ssd_mixer(cfg, h, layer, w_in, conv_w, conv_b, a_log, dt_bias, d_skip, norm_w, w_out):
    d = cfg["d"]
    heads = d // SSD_HEAD_DIM
    bc_dim = SSD_GROUPS * SSD_STATE
    main = 2 * d + 2 * bc_dim
    tm = cfg["tm"]
    zx = _matmul(h, w_in, layer, 0, main, BF16, tm, _pick(main, (512, 256, 128)), "ssd_in_proj")
    w_tail, tail_layer, tail_lo = _tail_weights(w_in, layer, main)
    tail = _matmul(h, w_tail, tail_layer, tail_lo, tail_lo + LANES, F32, tm, LANES, "ssd_in_proj_dt")
    xbc = _conv_silu(cfg, zx, d, d + 2 * bc_dim, conv_w, conv_b)
    a_neg = -jnp.exp(a_log.astype(F32))
    dtc, cuc, dtr, cur = _prep(cfg, tail, _pad_lanes(dt_bias), _pad_lanes(a_neg),
                               kind="ssd", chunk=SSD_CHUNK, n_heads=heads, hps=heads // SSD_GROUPS)
    y = _ssd_scan(cfg, xbc, zx, dtc, cuc, dtr, cur,
                  jnp.repeat(d_skip.astype(F32), SSD_HEAD_DIM).reshape(1, d), norm_w.reshape(1, d))
    return _matmul(y, w_out, layer, 0, d, BF16, tm, _pick(d, (512, 256, 128)), "ssd_out_proj")


def _gdn_mixer(cfg, h, layer, w_in, conv_w, a_log, dt_bias, norm_w, w_out):
    d = cfg["d"]
    hv = d // GDN_HEAD
    key_dim = (hv // 2) * GDN_HEAD
    conv_dim = 2 * key_dim + d
    main = conv_dim + d
    tm = cfg["tm"]
    qkvz = _matmul(h, w_in, layer, 0, main, BF16, tm, _pick(main, (512, 256, 128)), "gdn_in_proj")
    w_tail, tail_layer, tail_lo = _tail_weights(w_in, layer, main)
    tail = _matmul(h, w_tail, tail_layer, tail_lo, tail_lo + LANES, F32, tm, LANES, "gdn_in_proj_gates")
    qk = _conv_silu(cfg, qkvz, 0, 2 * key_dim, conv_w[:, :2 * key_dim], jnp.zeros((2 * key_dim,), F32),
                    norm_ch=2 * key_dim, q_ch=key_dim, q_scale=GDN_HEAD ** -0.5)
    v = _conv_silu(cfg, qkvz, 2 * key_dim, d, conv_w[:, 2 * key_dim:], jnp.zeros((d,), F32))
    a_pos = jnp.exp(a_log.astype(F32)).reshape(-1)
    zeros = jnp.zeros((2 * hv,), F32)
    bias_full = _pad_lanes(jnp.concatenate([zeros, dt_bias.astype(F32).reshape(-1)]))
    amul_full = _pad_lanes(jnp.concatenate([zeros, -a_pos]))
    hps = min(GDN_HEADS_PER_STEP, hv)
    bc, gc, _, gr = _prep(cfg, tail, bias_full, amul_full, kind="gdn", chunk=GDN_CHUNK,
                           n_heads=hv, hps=hps)
    o_f = _gdn_scan(cfg, qk, v, bc, gc, gr, rev=False)
    y = _gdn_scan(cfg, qk, v, bc, gc, gr, rev=True, prev=o_f, qkvz=qkvz,
                  norm_w=norm_w.reshape(1, GDN_HEAD).astype(F32))
    return _matmul(y, w_out, layer, 0, d, BF16, tm, _pick(d, (512, 256, 128)), "gdn_out_proj")


def _ffn(cfg, h, layer, w_in, w_out_bf):
    d = cfg["d"]
    d_ff = w_in.shape[-1] // 2
    hid = _matmul_swiglu(h, w_in, layer, cfg["tm"], _pick(d_ff, (256, 128)))
    return _matmul(hid, w_out_bf, layer, 0, d, BF16, cfg["tm_out"], _pick(d, (512, 256, 128)), "ffn_out")


def kernel(x, c, ctx, c_ctx, ada_down, ada_up, ada_bias, norm_mix, norm_ffn, ffn_in, ffn_out,
           ssd_in, ssd_conv_w, ssd_conv_b, ssd_a_log, ssd_dt_bias, ssd_d, ssd_norm, ssd_out,
           gdn_in, gdn_conv_w, gdn_a_log, gdn_dt_bias, gdn_norm, gdn_out, final_norm):
    batch, seq, d = x.shape
    ctx_len = ctx.shape[1]
    depth = ada_down.shape[0]
    rows = seq // GRID_W
    assert rows == ROW_TILE and ctx_len % CONV_ROWS == 0 and seq % CONV_ROWS == 0
    t_lat, t_ctx = batch * seq, batch * ctx_len
    t = t_lat + t_ctx
    cfg = dict(d=d, batch=batch, seq=seq, ctx_len=ctx_len, cols=GRID_W, t_lat=t_lat, t_ctx=t_ctx, t=t,
               tm=_pick(t, (1536, 1024, 512, 256, 128)), tm_out=_pick(t, (512, 256, 128)))
    assert t_ctx % GRID_W == 0

    n_streams = batch + 1
    cv = jnp.concatenate([c.astype(F32), c_ctx.astype(F32)[None, :],
                          jnp.zeros((8 - n_streams % 8 if n_streams % 8 else 0, d), F32)], axis=0)
    mods = _ada_modulation(cv, ada_down, ada_up, ada_bias)

    def mod(layer, which):
        return mods[layer, :n_streams, which * d:(which + 1) * d].reshape(n_streams, 1, d)

    ffn_out_bf = ffn_out.astype(BF16)

    x_lat = x.reshape(t_lat, d)
    x_ctx = ctx.reshape(t_ctx, d)
    pending, pending_gate = None, None
    col_major = False
    for i in range(depth):
        want_col_major = (i // N_MIXERS) % 2 == 1
        assert want_col_major or not col_major
        j = i // N_MIXERS
        x_lat, x_ctx, h = _norm_modulate(cfg, x_lat, x_ctx, pending, pending_gate,
                                         norm_mix[i].reshape(1, d), mod(i, 0), mod(i, 1),
                                         to_col_major=want_col_major and not col_major)
        col_major = want_col_major
        if i % N_MIXERS == 0:
            y = _ssd_mixer(cfg, h, j, ssd_in, ssd_conv_w[j], ssd_conv_b[j], ssd_a_log[j],
                           ssd_dt_bias[j], ssd_d[j], ssd_norm[j], ssd_out)
        else:
            y = _gdn_mixer(cfg, h, j, gdn_in, gdn_conv_w[j], gdn_a_log[j], gdn_dt_bias[j],
                           gdn_norm[j], gdn_out)
        x_lat, x_ctx, h = _norm_modulate(cfg, x_lat, x_ctx, y, mod(i, 2),
                                         norm_ffn[i].reshape(1, d), mod(i, 3), mod(i, 4),
                                         to_col_major=False)
        pending, pending_gate = _ffn(cfg, h, i, ffn_in, ffn_out_bf), mod(i, 5)
    out = _final_norm(cfg, x_lat, pending, pending_gate, final_norm.reshape(1, d),
                      from_col_major=col_major)
    return out.reshape(batch, seq, d)
```

```python
import functools

import jax
import jax.numpy as jnp
from jax import lax
from jax.experimental import pallas as pl
from jax.experimental.pallas import tpu as pltpu

F32 = jnp.float32
BF16 = jnp.bfloat16
EPS = 1e-6

GRID_W = 64
N_MIXERS = 2
N_MOD = 6
SSD_HEAD_DIM = 64
SSD_GROUPS = 8
SSD_STATE = 128
SSD_CHUNK = 128
SSD_GROUPS_PER_STEP = 4
GDN_HEAD = 128
GDN_CHUNK = 64
GDN_HEADS_PER_STEP = 16

ROW_TILE = 128
NORM_ROWS = 256
CONV_ROWS = 256
HALO = 8
LANES = 128
CONV_PITCH = 36
CONV_SLAB_UNROLL = 2
V7X_VMEM_LIMIT = 56 * 1024 * 1024


def _cparams(sem, vmem=None):
    return pltpu.CompilerParams(dimension_semantics=sem, vmem_limit_bytes=vmem or V7X_VMEM_LIMIT)


def _sigmoid(x):
    return 1.0 / (1.0 + jnp.exp(-x))


def _silu(x):
    h = 0.5 * x
    return h + h * jnp.tanh(h)


def _softplus(x):
    return jnp.maximum(x, 0.0) + jnp.log1p(jnp.exp(-jnp.abs(x)))


def _pick(n, candidates):
    for c in candidates:
        if n % c == 0:
            return c
    raise ValueError(f"no tile for {n} among {candidates}")


def _split3(a):
    hi = a.astype(BF16).astype(F32)
    r1 = a - hi
    mid = r1.astype(BF16).astype(F32)
    lo = (r1 - mid).astype(BF16).astype(F32)
    return hi, mid, lo


def _dot(a, b):
    return jnp.dot(a, b, preferred_element_type=F32)


def _dot_exact_rhs(m, a):
    hi, mid, lo = _split3(a)
    mb = m.astype(BF16)
    return _dot(mb, hi.astype(BF16)) + _dot(mb, mid.astype(BF16)) + _dot(mb, lo.astype(BF16))


def _ada_kernel(cv_ref, down_ref, up_ref, bias_ref, o_ref):
    t = jnp.dot(_silu(cv_ref[...]), down_ref[...], preferred_element_type=F32,
                precision=lax.Precision.HIGHEST)
    o_ref[...] = jnp.dot(t, up_ref[...], preferred_element_type=F32,
                         precision=lax.Precision.HIGHEST) + bias_ref[...]


def _ada_modulation(cv, down, up, bias):
    depth, d, r = down.shape
    n = up.shape[-1]
    tn = _pick(n, (6144, 3072, 1536, 768, 512, 256, 128))
    rows = cv.shape[0]
    return pl.pallas_call(
        _ada_kernel,
        grid=(depth, n // tn),
        in_specs=[
            pl.BlockSpec((rows, d), lambda l, j: (0, 0)),
            pl.BlockSpec((None, d, r), lambda l, j: (l, 0, 0)),
            pl.BlockSpec((None, r, tn), lambda l, j: (l, 0, j)),
            pl.BlockSpec((None, 1, tn), lambda l, j: (l, 0, j)),
        ],
        out_specs=pl.BlockSpec((None, rows, tn), lambda l, j: (l, 0, j)),
        out_shape=jax.ShapeDtypeStruct((depth, rows, n), F32),
        compiler_params=_cparams(("arbitrary", "arbitrary")),
        name="ada_modulation",
    )(cv, down, up, bias.reshape(depth, 1, n))


def _norm_kernel(*refs, has_y, n_lat_tiles):
    if has_y:
        (xl_ref, xc_ref, yl_ref, yc_ref, g_ref, w_ref, sh_ref, sc_ref,
         xlo_ref, xco_ref, h_ref) = refs
    else:
        xl_ref, xc_ref, w_ref, sh_ref, sc_ref, h_ref = refs

    def run(x_ref, y_ref, xo_ref):
        x = x_ref[...]
        if has_y:
            x = x + g_ref[...] * y_ref[...]
            xo_ref[...] = x
        xn = x * lax.rsqrt(jnp.mean(x * x, axis=-1, keepdims=True) + EPS) * w_ref[...]
        h_ref[...] = (xn * (1.0 + sc_ref[...]) + sh_ref[...]).astype(h_ref.dtype)

    i = pl.program_id(0)

    @pl.when(i < n_lat_tiles)
    def _():
        run(xl_ref, yl_ref if has_y else None, xlo_ref if has_y else None)

    @pl.when(i >= n_lat_tiles)
    def _():
        run(xc_ref, yc_ref if has_y else None, xco_ref if has_y else None)


def _norm_modulate(cfg, x_lat, x_ctx, y, gate, w, shift, scale, *, to_col_major):
    d = cfg["d"]
    tr = ROW_TILE if to_col_major else NORM_ROWS
    assert cfg["seq"] % tr == 0 and cfg["t_ctx"] % tr == 0
    n_lat, n_ctx = cfg["t_lat"] // tr, cfg["t_ctx"] // tr
    per_batch = cfg["seq"] // tr
    cols = cfg["cols"]
    has_y = y is not None

    def lat_tile(i):
        return jnp.minimum(i, n_lat - 1)

    def ctx_tile(i):
        return jnp.maximum(i - n_lat, 0)

    def stream(i):
        return jnp.where(i < n_lat, i // per_batch, cfg["batch"])

    if to_col_major:
        def strided(i):
            il = lat_tile(i)
            return (il // cols, il % cols)
        xl_in = x_lat.reshape(cfg["t_lat"] // cols, cols * d)
        xl_spec = pl.BlockSpec((tr, d), strided)
    else:
        xl_in = x_lat
        xl_spec = pl.BlockSpec((tr, d), lambda i: (lat_tile(i), 0))
    xc_spec = pl.BlockSpec((tr, d), lambda i: (ctx_tile(i), 0))
    mod_spec = pl.BlockSpec((None, 1, d), lambda i: (stream(i), 0, 0))
    w_spec = pl.BlockSpec((1, d), lambda i: (0, 0))
    h_spec = pl.BlockSpec((tr, d), lambda i: (i, 0))
    h_shape = jax.ShapeDtypeStruct((cfg["t"], d), BF16)

    if has_y:
        if to_col_major:
            yl_in = y.reshape(cfg["t"] // cols, cols * d)
            yl_spec = pl.BlockSpec((tr, d), strided)
        else:
            yl_in = y
            yl_spec = pl.BlockSpec((tr, d), lambda i: (lat_tile(i), 0))
        yc_spec = pl.BlockSpec((tr, d), lambda i: (n_lat + ctx_tile(i), 0))
        ins = (xl_in, x_ctx, yl_in, y, gate, w, shift, scale)
        in_specs = [xl_spec, xc_spec, yl_spec, yc_spec, mod_spec, w_spec, mod_spec, mod_spec]
        out_specs = [pl.BlockSpec((tr, d), lambda i: (lat_tile(i), 0)), xc_spec, h_spec]
        out_shape = [jax.ShapeDtypeStruct((cfg["t_lat"], d), F32),
                     jax.ShapeDtypeStruct((cfg["t_ctx"], d), F32), h_shape]
    else:
        ins = (xl_in, x_ctx, w, shift, scale)
        in_specs = [xl_spec, xc_spec, w_spec, mod_spec, mod_spec]
        out_specs = h_spec
        out_shape = h_shape
    out = pl.pallas_call(
        functools.partial(_norm_kernel, has_y=has_y, n_lat_tiles=n_lat),
        grid=(n_lat + n_ctx,),
        in_specs=in_specs, out_specs=out_specs, out_shape=out_shape,
        compiler_params=_cparams(("arbitrary",)),
        name="norm_modulate",
    )(*ins)
    if has_y:
        return out
    return x_lat, x_ctx, out


def _final_kernel(x_ref, y_ref, g_ref, w_ref, o_ref):
    x = x_ref[...] + g_ref[...] * y_ref[...]
    o_ref[...] = x * lax.rsqrt(jnp.mean(x * x, axis=-1, keepdims=True) + EPS) * w_ref[...]


def _final_norm(cfg, x_lat, y, gate, w, *, from_col_major):
    d, tr = cfg["d"], ROW_TILE
    n_lat = cfg["t_lat"] // tr
    per_batch = cfg["seq"] // tr
    cols = cfg["cols"]
    plain = pl.BlockSpec((tr, d), lambda i: (i, 0))
    if from_col_major:
        out_spec = pl.BlockSpec((tr, d), lambda i: (i // cols, i % cols))
        out_shape = jax.ShapeDtypeStruct((cfg["t_lat"] // cols, cols * d), F32)
    else:
        out_spec, out_shape = plain, jax.ShapeDtypeStruct((cfg["t_lat"], d), F32)
    out = pl.pallas_call(
        _final_kernel,
        grid=(n_lat,),
        in_specs=[plain, plain,
                  pl.BlockSpec((None, 1, d), lambda i: (i // per_batch, 0, 0)),
                  pl.BlockSpec((1, d), lambda i: (0, 0))],
        out_specs=out_spec, out_shape=out_shape,
        compiler_params=_cparams(("arbitrary",)),
        name="final_norm",
    )(x_lat, y, gate, w)
    return out.reshape(cfg["t_lat"], d)


def _mm_kernel(a_ref, w_ref, o_ref):
    o_ref[...] = _dot(a_ref[...], w_ref[...].astype(BF16)).astype(o_ref.dtype)


def _matmul(a, w, layer, col_lo, col_hi, out_dtype, tm, tn, name):
    m, k = a.shape
    n = col_hi - col_lo
    assert m % tm == 0 and n % tn == 0 and col_lo % tn == 0
    off = col_lo // tn
    return pl.pallas_call(
        _mm_kernel,
        grid=(m // tm, n // tn),
        in_specs=[pl.BlockSpec((tm, k), lambda i, j: (i, 0)),
                  pl.BlockSpec((None, k, tn), lambda i, j: (layer, 0, off + j))],
        out_specs=pl.BlockSpec((tm, tn), lambda i, j: (i, j)),
        out_shape=jax.ShapeDtypeStruct((m, n), out_dtype),
        compiler_params=_cparams(("arbitrary", "arbitrary")),
        name=name,
    )(a, w)


def _swiglu_kernel(a_ref, wg_ref, wu_ref, o_ref):
    a = a_ref[...]
    g = _dot(a, wg_ref[...].astype(BF16))
    u = _dot(a, wu_ref[...].astype(BF16))
    o_ref[...] = (_silu(g) * u).astype(o_ref.dtype)


def _matmul_swiglu(a, w, layer, tm, tn):
    m, k = a.shape
    d_ff = w.shape[-1] // 2
    assert m % tm == 0 and d_ff % tn == 0
    nj = d_ff // tn
    return pl.pallas_call(
        _swiglu_kernel,
        grid=(m // tm, nj),
        in_specs=[pl.BlockSpec((tm, k), lambda i, j: (i, 0)),
                  pl.BlockSpec((None, k, tn), lambda i, j: (layer, 0, j)),
                  pl.BlockSpec((None, k, tn), lambda i, j: (layer, 0, nj + j))],
        out_specs=pl.BlockSpec((tm, tn), lambda i, j: (i, j)),
        out_shape=jax.ShapeDtypeStruct((m, d_ff), BF16),
        compiler_params=_cparams(("arbitrary", "arbitrary")),
        name="ffn_in_swiglu",
    )(a, w, w)


def _conv_kernel(prev_ref, cur_ref, next_ref, w_ref, b_ref, o_ref, scr_ref, out_ref, *,
                 tiles_per_lat_seq, n_lat_tiles, tiles_per_ctx_seq, n_norm_tiles, n_q_tiles, q_scale):
    i, j = pl.program_id(0), pl.program_id(1)
    ts = cur_ref.shape[0]
    ksz = w_ref.shape[0]
    is_lat = i < n_lat_tiles
    pos = jnp.where(is_lat, i % tiles_per_lat_seq, (i - n_lat_tiles) % tiles_per_ctx_seq)
    per_seq = jnp.where(is_lat, tiles_per_lat_seq, tiles_per_ctx_seq)
    keep_prev = jnp.where(pos == 0, 0.0, 1.0)
    keep_next = jnp.where(pos == per_seq - 1, 0.0, 1.0)
    if n_norm_tiles:
        normed = j < n_norm_tiles
        scale = jnp.where(j < n_q_tiles, q_scale, 1.0)
    first = HALO - ksz // 2
    n_slabs = cur_ref.shape[1] // LANES

    def one_slab(sl):
        lanes = pl.ds(pl.multiple_of(sl * LANES, LANES), LANES)
        scr_ref[sl, 0:HALO, :] = prev_ref[:, lanes].astype(F32) * keep_prev
        scr_ref[sl, HALO:HALO + ts, :] = cur_ref[:, lanes].astype(F32)
        scr_ref[sl, HALO + ts:2 * HALO + ts, :] = next_ref[:, lanes].astype(F32) * keep_next
        scr_ref[sl, 2 * HALO + ts:, :] = jnp.zeros((scr_ref.shape[1] - 2 * HALO - ts, LANES), F32)
        wk = [jnp.broadcast_to(w_ref[k:k + 1, lanes], (8, LANES)) for k in range(ksz)]
        bias = jnp.broadcast_to(b_ref[:, lanes], (8, LANES))
        taps = [scr_ref[sl, pl.ds(first + g, 8, stride=CONV_PITCH), :]
                for g in range(CONV_PITCH + ksz - 1)]
        for g in range(CONV_PITCH):
            acc = bias + wk[0] * taps[g]
            for k in range(1, ksz):
                acc = acc + wk[k] * taps[g + k]
            y = _silu(acc)
            if n_norm_tiles:
                r = lax.rsqrt(jnp.sum(y * y, axis=-1, keepdims=True) + EPS) * scale
                y = y * jnp.where(normed, r, 1.0)
            out_ref[sl, pl.ds(g, 8, stride=CONV_PITCH), :] = y
        o_ref[:, lanes] = out_ref[sl, 0:ts, :].astype(o_ref.dtype)

    @pl.loop(0, n_slabs // CONV_SLAB_UNROLL)
    def _(it):
        for u in range(CONV_SLAB_UNROLL):
            one_slab(it * CONV_SLAB_UNROLL + u)


def _conv_silu(cfg, src, col_lo, n_ch, w, b, *, norm_ch=0, q_ch=0, q_scale=1.0):
    t, ts = cfg["t"], CONV_ROWS
    tc = next(c for c in (2048, 1024, 512, 256)
              if all(v % c == 0 for v in (n_ch, col_lo, norm_ch, q_ch)))
    assert (tc // LANES) % CONV_SLAB_UNROLL == 0
    assert cfg["seq"] % ts == 0 and cfg["ctx_len"] % ts == 0 and 8 * CONV_PITCH >= ts
    ksz = w.shape[0]
    last_row = (HALO - ksz // 2) + (CONV_PITCH + ksz - 2) + 7 * CONV_PITCH
    in_rows = -(-(last_row + 1) // 8) * 8
    assert in_rows >= ts + 2 * HALO
    off = col_lo // tc
    hb = ts // HALO
    last_hb = t // HALO - 1
    kern = functools.partial(
        _conv_kernel, tiles_per_lat_seq=cfg["seq"] // ts, n_lat_tiles=cfg["t_lat"] // ts,
        tiles_per_ctx_seq=cfg["ctx_len"] // ts, n_norm_tiles=norm_ch // tc, n_q_tiles=q_ch // tc,
        q_scale=q_scale)
    return pl.pallas_call(
        kern,
        grid=(t // ts, n_ch // tc),
        in_specs=[
            pl.BlockSpec((HALO, tc), lambda i, j: (jnp.maximum(i * hb - 1, 0), off + j)),
            pl.BlockSpec((ts, tc), lambda i, j: (i, off + j)),
            pl.BlockSpec((HALO, tc), lambda i, j: (jnp.minimum((i + 1) * hb, last_hb), off + j)),
            pl.BlockSpec((w.shape[0], tc), lambda i, j: (0, j)),
            pl.BlockSpec((1, tc), lambda i, j: (0, j)),
        ],
        out_specs=pl.BlockSpec((ts, tc), lambda i, j: (i, j)),
        out_shape=jax.ShapeDtypeStruct((t, n_ch), BF16),
        scratch_shapes=[pltpu.VMEM((tc // LANES, in_rows, LANES), F32),
                        pltpu.VMEM((tc // LANES, 8 * CONV_PITCH, LANES), F32)],
        compiler_params=_cparams(("arbitrary", "arbitrary")),
        name="conv_silu",
    )(src, src, src, w, b.reshape(1, n_ch))


def _prep_kernel(tail_ref, bias_ref, amul_ref, p1c_ref, p2c_ref, p1r_ref, p2r_ref, *,
                 kind, chunk, p2_off, split_lane, hps):
    tr = ROW_TILE
    row = lax.broadcasted_iota(jnp.int32, (tr, tr), 0)
    col = lax.broadcasted_iota(jnp.int32, (tr, tr), 1)
    same = (row // chunk) == (col // chunk)
    lower = jnp.where(same & (col <= row), 1.0, 0.0)
    upper = jnp.where(same & (col >= row), 1.0, 0.0)
    lane = lax.broadcasted_iota(jnp.int32, (tr, LANES), 1)
    for sub in range(tail_ref.shape[0] // tr):
        rows = slice(sub * tr, (sub + 1) * tr)
        raw = tail_ref[rows, :]
        sp = _softplus(raw + bias_ref[...])
        p1 = sp if kind == "ssd" else _sigmoid(raw)
        a = sp * amul_ref[...]
        cum = jnp.where(lane < split_lane, _dot_exact_rhs(lower, a), _dot_exact_rhs(upper, a))
        p1t = p1.T
        cumt = cum.T
        for n in range(p1c_ref.shape[0]):
            p1c_ref[n, rows, :] = p1[:, n * hps:(n + 1) * hps]
            p2c_ref[n, rows, :] = cum[:, p2_off + n * hps:p2_off + (n + 1) * hps]
            p1r_ref[n, :, rows] = p1t[n * hps:(n + 1) * hps, :]
            p2r_ref[n, :, rows] = cumt[p2_off + n * hps:p2_off + (n + 1) * hps, :]


def _prep(cfg, tail, bias_full, amul_full, *, kind, chunk, n_heads, hps):
    t = cfg["t"]
    tr = _pick(t, (4 * ROW_TILE, 2 * ROW_TILE, ROW_TILE))
    nblk = 2 * n_heads // hps
    p2_off = 0 if kind == "ssd" else 2 * n_heads
    split_lane = p2_off + n_heads
    kern = functools.partial(_prep_kernel, kind=kind, chunk=chunk, p2_off=p2_off,
                             split_lane=split_lane, hps=hps)
    col_spec = pl.BlockSpec((nblk, tr, hps), lambda i: (0, i, 0))
    row_spec = pl.BlockSpec((nblk, hps, tr), lambda i: (0, 0, i))
    col_shape = jax.ShapeDtypeStruct((nblk, t, hps), F32)
    row_shape = jax.ShapeDtypeStruct((nblk, hps, t), F32)
    return pl.pallas_call(
        kern,
        grid=(t // tr,),
        in_specs=[pl.BlockSpec((tr, 128), lambda i: (i, 0)),
                  pl.BlockSpec((1, 128), lambda i: (0, 0)),
                  pl.BlockSpec((1, 128), lambda i: (0, 0))],
        out_specs=[col_spec, col_spec, row_spec, row_spec],
        out_shape=[col_shape, col_shape, row_shape, row_shape],
        compiler_params=_cparams(("arbitrary",)),
        name=f"prep_{kind}",
    )(tail, bias_full, amul_full)


def _fwd_block(cfg, b, i):
    ncc, ncl = cfg["ctx_len"] // ROW_TILE, cfg["seq"] // ROW_TILE
    ctx = cfg["batch"] * ncl + b * ncc + i
    lat = b * ncl + (i - ncc)
    return jnp.where(i < ncc, ctx, lat)


def _bwd_block(cfg, b, i):
    ncc, ncl = cfg["ctx_len"] // ROW_TILE, cfg["seq"] // ROW_TILE
    ctx = cfg["batch"] * ncl + b * ncc + (ncc - 1 - i)
    lat = b * ncl + (ncl - 1 - (i - ncc))
    return jnp.where(i < ncc, ctx, lat)


def _ssd_kernel(x_ref, b_ref, c_ref, z_ref, dtc_f_ref, dtc_b_ref, cuc_f_ref, cuc_b_ref,
                dtr_f_ref, dtr_b_ref, cur_f_ref, cur_b_ref, dskip_ref, nw_ref,
                o_ref, sf_ref, sb_ref, store_ref, *, nb, ncc):
    i = pl.program_id(2)
    q = x_ref.shape[0]
    gps = dtc_f_ref.shape[0]
    hpg = dtc_f_ref.shape[-1]
    width = x_ref.shape[1] // gps
    n = b_ref.shape[1] // gps
    hd = width // hpg
    groups = range(gps)
    expand = jnp.where(lax.broadcasted_iota(jnp.int32, (hpg, width), 1) // hd
                       == lax.broadcasted_iota(jnp.int32, (hpg, width), 0), 1.0, 0.0).astype(BF16)

    def widen(a):
        hi = a.astype(BF16)
        lo = (a - hi.astype(F32)).astype(BF16)
        return _dot(hi, expand) + _dot(lo, expand)

    def cols(gg, w):
        return slice(gg * w, (gg + 1) * w)

    @pl.when(i == 0)
    def _():
        sb_ref[...] = jnp.zeros_like(sb_ref)

    @pl.when(i == nb)
    def _():
        sf_ref[...] = jnp.zeros_like(sf_ref)

    @pl.when(i < nb)
    def _():
        for gg in groups:
            cub = cuc_b_ref[gg]
            tot = cub[0:1, :]
            wide = widen(jnp.concatenate([dtc_b_ref[gg] * jnp.exp(tot - cub), jnp.exp(cub[0:8, :])], axis=0))
            wgt, dec = wide[:q], wide[q:q + 1]
            xs = (x_ref[:, cols(gg, width)].astype(F32) * wgt).astype(BF16)
            cs = lax.dot_general(b_ref[:, cols(gg, n)], xs, (((0,), (0,)), ((), ())),
                                 preferred_element_type=F32)
            store_ref[i, gg] = sb_ref[gg].astype(BF16)
            sb_ref[gg] = sb_ref[gg] * dec + cs

    @pl.when(i >= nb)
    def _():
        i2 = i - nb
        slot = jnp.where(i2 < ncc, ncc - 1 - i2, nb - 1 - (i2 - ncc))
        li = lax.broadcasted_iota(jnp.int32, (q, q), 0)
        si = lax.broadcasted_iota(jnp.int32, (q, q), 1)
        causal = si <= li
        lane = lax.broadcasted_iota(jnp.int32, (q, 2 * hd), 1)
        cb = [lax.dot_general(c_ref[:, cols(gg, n)], b_ref[:, cols(gg, n)], (((1,), (1,)), ((), ())),
                              preferred_element_type=F32) for gg in groups]
        wide = []
        for gg in groups:
            cuf = cuc_f_ref[gg]
            wide.append(widen(jnp.concatenate(
                [jnp.exp(cuf), jnp.exp(cuc_b_ref[gg]), dtc_f_ref[gg] * jnp.exp(cuf[q - 1:q, :] - cuf)], axis=0)))
        y_off = []
        for gg in groups:
            ef, eb, wf = wide[gg][:q], wide[gg][q:2 * q], wide[gg][2 * q:]
            cm = c_ref[:, cols(gg, n)]
            y_off.append(_dot(cm, sf_ref[gg].astype(BF16)) * ef + _dot(cm, store_ref[slot, gg]) * eb)
            xs = (x_ref[:, cols(gg, width)].astype(F32) * wf).astype(BF16)
            cs = lax.dot_general(b_ref[:, cols(gg, n)], xs, (((0,), (0,)), ((), ())),
                                 preferred_element_type=F32)
            sf_ref[gg] = sf_ref[gg] * ef[q - 1:q, :] + cs

        y_diag = [[] for _ in groups]
        for p in range(hpg // 2):
            for gg in groups:
                xp = x_ref[:, gg * width + p * 2 * hd:gg * width + (p + 1) * 2 * hd]
                acc = None
                for half in range(2):
                    r = 2 * p + half
                    log_dec = jnp.where(causal, cuc_f_ref[gg, :, r:r + 1] - cur_f_ref[gg, r:r + 1, :],
                                        cuc_b_ref[gg, :, r:r + 1] - cur_b_ref[gg, r:r + 1, :])
                    dt_row = jnp.where(causal, dtr_f_ref[gg, r:r + 1, :], dtr_b_ref[gg, r:r + 1, :])
                    mix = (cb[gg] * (jnp.exp(log_dec) * dt_row)).astype(BF16)
                    keep = (lane < hd) if half == 0 else (lane >= hd)
                    part = _dot(mix, jnp.where(keep, xp, jnp.zeros_like(xp)))
                    acc = part if acc is None else acc + part
                y_diag[gg].append(acc)

        for gg in groups:
            cb_diag = jnp.sum(jnp.where(si == li, cb[gg], 0.0), axis=-1, keepdims=True)
            own_b = widen(dtc_b_ref[gg] * cb_diag)
            xf = x_ref[:, cols(gg, width)].astype(F32)
            y = (dskip_ref[:, cols(gg, width)] + own_b) * xf + y_off[gg] + jnp.concatenate(y_diag[gg], axis=1)
            gated = y * _silu(z_ref[:, cols(gg, width)].astype(F32))
            o_ref[:, cols(gg, width)] = (
                gated * lax.rsqrt(jnp.mean(gated * gated, axis=-1, keepdims=True) + EPS)
                * nw_ref[:, cols(gg, width)]).astype(o_ref.dtype)


def _ssd_scan(cfg, xbc, zx, dtc, cuc, dtr, cur, d_skip, norm_w):
    t, tr, batch = cfg["t"], ROW_TILE, cfg["batch"]
    g, n = SSD_GROUPS, SSD_STATE
    gps = SSD_GROUPS_PER_STEP if g % SSD_GROUPS_PER_STEP == 0 else 1
    inner = cfg["d"]
    width = inner // g
    hpg = width // SSD_HEAD_DIM
    assert hpg % 2 == 0 and tr == SSD_CHUNK and inner % (gps * n) == 0
    ncc, ncl = cfg["ctx_len"] // tr, cfg["seq"] // tr
    nb = ncc + ncl
    b_off, c_off = inner // (gps * n), (inner + g * n) // (gps * n)

    def blk(b, i):
        return jnp.where(i < nb, _bwd_block(cfg, b, i), _fwd_block(cfg, b, i - nb))

    def oblk(b, i):
        return _fwd_block(cfg, b, jnp.maximum(i - nb, 0))

    col = lambda d: pl.BlockSpec((gps, tr, hpg), lambda b, gi, i: (d * (g // gps) + gi, blk(b, i), 0))
    row = lambda d: pl.BlockSpec((gps, hpg, tr), lambda b, gi, i: (d * (g // gps) + gi, 0, blk(b, i)))
    kern = functools.partial(_ssd_kernel, nb=nb, ncc=ncc)
    return pl.pallas_call(
        kern,
        grid=(batch, g // gps, 2 * nb),
        in_specs=[
            pl.BlockSpec((tr, gps * width), lambda b, gi, i: (blk(b, i), gi)),
            pl.BlockSpec((tr, gps * n), lambda b, gi, i: (blk(b, i), b_off + gi)),
            pl.BlockSpec((tr, gps * n), lambda b, gi, i: (blk(b, i), c_off + gi)),
            pl.BlockSpec((tr, gps * width), lambda b, gi, i: (oblk(b, i), gi)),
            col(0), col(1), col(0), col(1), row(0), row(1), row(0), row(1),
            pl.BlockSpec((1, gps * width), lambda b, gi, i: (0, gi)),
            pl.BlockSpec((1, gps * width), lambda b, gi, i: (0, gi)),
        ],
        out_specs=pl.BlockSpec((tr, gps * width), lambda b, gi, i: (oblk(b, i), gi)),
        out_shape=jax.ShapeDtypeStruct((t, inner), BF16),
        scratch_shapes=[pltpu.VMEM((gps, n, width), F32), pltpu.VMEM((gps, n, width), F32),
                        pltpu.VMEM((nb, gps, n, width), BF16)],
        compiler_params=_cparams(("arbitrary", "arbitrary", "arbitrary")),
        name="ssd_scan",
    )(xbc, xbc, xbc, zx, dtc, dtc, cuc, cuc, dtr, dtr, cur, cur, d_skip, norm_w)


def _gdn_kernel(*refs, rev, finish):
    if finish:
        (q_ref, k_ref, v_ref, bc_ref, gc_ref, gr_ref, prev_ref, z_ref, nw_ref,
         o_ref, s_ref) = refs
    else:
        q_ref, k_ref, v_ref, bc_ref, gc_ref, gr_ref, o_ref, s_ref = refs
    c = GDN_CHUNK
    hw = GDN_HEAD
    n_heads = s_ref.shape[0]
    rep = n_heads // (k_ref.shape[1] // hw)

    @pl.when(pl.program_id(2) == 0)
    def _():
        s_ref[...] = jnp.zeros_like(s_ref)

    ii = lax.broadcasted_iota(jnp.int32, (c, c), 0)
    jj = lax.broadcasted_iota(jnp.int32, (c, c), 1)
    incl = (jj >= ii) if rev else (jj <= ii)
    strict = (jj > ii) if rev else (jj < ii)
    eye = jnp.where(ii == jj, 1.0, 0.0)
    n_chunks = q_ref.shape[0] // c
    order = range(n_chunks - 1, -1, -1) if rev else range(n_chunks)
    last = 0 if rev else c - 1

    heads = range(n_heads)
    steps = c.bit_length() - 1
    qk, kk, inv, pw, qkd, res, uw = {}, {}, {}, {}, {}, {}, {}

    def rows_of(ci):
        return slice(ci * c, (ci + 1) * c)

    assert rep == 2
    row2 = lax.broadcasted_iota(jnp.int32, (c, 2 * c), 0)
    lane2 = lax.broadcasted_iota(jnp.int32, (c, 2 * c), 1)
    col2, second = lane2 % c, lane2 >= c
    incl2 = (col2 >= row2) if rev else (col2 <= row2)
    strict2 = (col2 > row2) if rev else (col2 < row2)
    eye2 = jnp.where(col2 == row2, 1.0, 0.0)
    same_head = (lax.broadcasted_iota(jnp.int32, (2 * c, 2 * c), 0) // c
                 == lax.broadcasted_iota(jnp.int32, (2 * c, 2 * c), 1) // c)

    def block_diag(x):
        return jnp.where(same_head, jnp.concatenate([x, x], axis=0), jnp.zeros((), x.dtype))

    def gram(ci):
        for kh in range(n_heads // rep):
            k_in = k_ref[rows_of(ci), kh * hw:(kh + 1) * hw]
            both = lax.dot_general(jnp.concatenate([q_ref[rows_of(ci), kh * hw:(kh + 1) * hw], k_in], axis=0),
                                   jnp.concatenate([k_in, k_in], axis=0),
                                   (((1,), (1,)), ((), ())), preferred_element_type=F32)
            qk[ci, kh], kk[ci, kh] = both[:c], both[c:]

    sizes = [2 << k for k in range(steps)]
    joins = {b: ((row2 // b) == (col2 // b)) & ((row2 // (b // 2)) != (col2 // (b // 2))) for b in sizes}

    def construct(ci, kh):
        rows = rows_of(ci)
        h0, h1 = kh * rep, kh * rep + 1
        gc2 = jnp.where(second, gc_ref[rows, h1:h1 + 1], gc_ref[rows, h0:h0 + 1])
        bc2 = jnp.where(second, bc_ref[rows, h1:h1 + 1], bc_ref[rows, h0:h0 + 1])
        gr2 = jnp.concatenate([gr_ref[h0:h0 + 1, rows], gr_ref[h1:h1 + 1, rows]], axis=1)
        dec = jnp.where(incl2, jnp.exp(jnp.where(incl2, gc2 - gr2, 0.0)), 0.0)
        a = jnp.where(strict2, bc2 * kk[ci, kh] * dec, 0.0)
        pw[ci, kh] = a
        inv[ci, kh] = eye2 - jnp.where(joins[sizes[0]], a, 0.0)
        qkd[ci, kh] = (qk[ci, kh] * dec).astype(BF16)
        res[ci, kh] = None

    def level(s, it):
        b = sizes[(s + 1) // 2]
        if s % 2 == 1:
            if res[it] is not None:
                inv[it] = inv[it] - res[it]
            res[it] = _dot(inv[it].astype(BF16), block_diag(jnp.where(joins[b], pw[it], 0.0).astype(BF16)))
        else:
            res[it] = _dot(res[it].astype(BF16), block_diag(inv[it].astype(BF16)))

    def solve(ci, kh):
        rows = rows_of(ci)
        t_pair = (inv[ci, kh] - res[ci, kh]).astype(BF16)
        for half, h in enumerate((kh * rep, kh * rep + 1)):
            bc1, gc1 = bc_ref[rows, h:h + 1], gc_ref[rows, h:h + 1]
            rhs = jnp.concatenate(
                [(v_ref[rows, h * hw:(h + 1) * hw].astype(F32) * bc1).astype(BF16),
                 (k_ref[rows, kh * hw:(kh + 1) * hw].astype(F32) * (bc1 * jnp.exp(gc1))).astype(BF16)], axis=1)
            uw[ci, h] = _dot(t_pair, pad_half(rhs, half))

    def pad_half(x, half):
        zeros = jnp.zeros_like(x)
        return jnp.concatenate([zeros, x] if half else [x, zeros], axis=0)

    st, ws_qs, kv, intra, gtot = {}, {}, {}, {}, {}

    def state_read(ci):
        for h in heads:
            st[h] = s_ref[h]
            ws_qs[h] = _dot(jnp.concatenate([uw[ci, h][:, hw:].astype(BF16),
                                             q_ref[rows_of(ci), (h // rep) * hw:(h // rep + 1) * hw]], axis=0),
                            st[h].astype(BF16))

    def state_update(ci):
        rows = rows_of(ci)
        for h in heads:
            v_new = (uw[ci, h][:, :hw] - ws_qs[h][:c]).astype(BF16)
            gc1 = gc_ref[rows, h:h + 1]
            gtot[h] = gc1[last:last + 1, :]
            kh = h // rep
            k_dec = (k_ref[rows, kh * hw:(kh + 1) * hw].astype(F32) * jnp.exp(gtot[h] - gc1)).astype(BF16)
            kv[h] = lax.dot_general(k_dec, v_new, (((0,), (0,)), ((), ())), preferred_element_type=F32)
            intra[h] = _dot(qkd[ci, kh], pad_half(v_new, h % rep))

    def state_write(ci):
        rows = rows_of(ci)
        for h in heads:
            s_ref[h] = st[h] * jnp.exp(gtot[h]) + kv[h]
        for h in heads:
            out = jnp.exp(gc_ref[rows, h:h + 1]) * ws_qs[h][c:] + intra[h]
            cols = slice(h * hw, (h + 1) * hw)
            if finish:
                o = out + prev_ref[rows, cols]
                o = o * lax.rsqrt(jnp.mean(o * o, axis=-1, keepdims=True) + EPS) * nw_ref[...]
                o_ref[rows, cols] = (o * _silu(z_ref[rows, cols].astype(F32))).astype(o_ref.dtype)
            else:
                o_ref[rows, cols] = out.astype(o_ref.dtype)

    pairs = [(ci, kh) for ci in order for kh in range(n_heads // rep)]
    for ci in order:
        gram(ci)
    for pair in pairs:
        construct(*pair)
    for s in range(1, 2 * (steps - 1) + 1):
        for pair in pairs:
            level(s, pair)
    for pair in pairs:
        solve(*pair)
    for ci in order:
        state_read(ci)
        state_update(ci)
        state_write(ci)


def _gdn_scan(cfg, qk, v, bc, gc, gr, *, rev, prev=None, qkvz=None, norm_w=None):
    t, tr, batch = cfg["t"], ROW_TILE, cfg["batch"]
    hw = GDN_HEAD
    hv = cfg["d"] // hw
    hk = hv // 2
    hps = min(GDN_HEADS_PER_STEP, hv)
    kps = hps // 2
    nhb = hv // hps
    nb = (cfg["ctx_len"] + cfg["seq"]) // tr
    finish = prev is not None
    d = 1 if rev else 0
    order = _bwd_block if rev else _fwd_block

    def blk(b, i):
        return order(cfg, b, i)

    k_off = hk // kps
    z_off = (2 * hk + hv) // hps
    col = pl.BlockSpec((None, tr, hps), lambda b, hb, i: (d * nhb + hb, blk(b, i), 0))
    row = pl.BlockSpec((None, hps, tr), lambda b, hb, i: (d * nhb + hb, 0, blk(b, i)))
    wide = lambda off: pl.BlockSpec((tr, hps * hw), lambda b, hb, i: (blk(b, i), off + hb))
    in_specs = [
        pl.BlockSpec((tr, kps * hw), lambda b, hb, i: (blk(b, i), hb)),
        pl.BlockSpec((tr, kps * hw), lambda b, hb, i: (blk(b, i), k_off + hb)),
        wide(0), col, col, row,
    ]
    ins = [qk, qk, v, bc, gc, gr]
    if finish:
        in_specs += [wide(0), wide(z_off), pl.BlockSpec((1, hw), lambda b, hb, i: (0, 0))]
        ins += [prev, qkvz, norm_w]
    return pl.pallas_call(
        functools.partial(_gdn_kernel, rev=rev, finish=finish),
        grid=(batch, nhb, nb),
        in_specs=in_specs,
        out_specs=wide(0),
        out_shape=jax.ShapeDtypeStruct((t, hv * hw), BF16),
        scratch_shapes=[pltpu.VMEM((hps, hw, hw), F32)],
        compiler_params=_cparams(("arbitrary", "arbitrary", "arbitrary")),
        name="gdn_scan_bwd" if rev else "gdn_scan_fwd",
    )(*ins)


def _pad_lanes(v, width=128):
    v = v.reshape(1, -1).astype(F32)
    return jnp.pad(v, ((0, 0), (0, width - v.shape[1])))


def _pad_cols(w, width=128):
    return jnp.pad(w, ((0, 0), (0, width - w.shape[1])))


def _tail_weights(w, layer, main):
    if w.shape[-1] - main == LANES:
        return w, layer, main
    return _pad_cols(w[layer][:, main:])[None], 0, 0


def _ssd_mixer(cfg, h, layer, w_in, conv_w, conv_b, a_log, dt_bias, d_skip, norm_w, w_out):
    d = cfg["d"]
    heads = d // SSD_HEAD_DIM
    bc_dim = SSD_GROUPS * SSD_STATE
    main = 2 * d + 2 * bc_dim
    tm = cfg["tm"]
    zx = _matmul(h, w_in, layer, 0, main, BF16, tm, _pick(main, (512, 256, 128)), "ssd_in_proj")
    w_tail, tail_layer, tail_lo = _tail_weights(w_in, layer, main)
    tail = _matmul(h, w_tail, tail_layer, tail_lo, tail_lo + LANES, F32, tm, LANES, "ssd_in_proj_dt")
    xbc = _conv_silu(cfg, zx, d, d + 2 * bc_dim, conv_w, conv_b)
    a_neg = -jnp.exp(a_log.astype(F32))
    dtc, cuc, dtr, cur = _prep(cfg, tail, _pad_lanes(dt_bias), _pad_lanes(a_neg),
                               kind="ssd", chunk=SSD_CHUNK, n_heads=heads, hps=heads // SSD_GROUPS)
    y = _ssd_scan(cfg, xbc, zx, dtc, cuc, dtr, cur,
                  jnp.repeat(d_skip.astype(F32), SSD_HEAD_DIM).reshape(1, d), norm_w.reshape(1, d))
    return _matmul(y, w_out, layer, 0, d, BF16, tm, _pick(d, (512, 256, 128)), "ssd_out_proj")


def _gdn_mixer(cfg, h, layer, w_in, conv_w, a_log, dt_bias, norm_w, w_out):
    d = cfg["d"]
    hv = d // GDN_HEAD
    key_dim = (hv // 2) * GDN_HEAD
    conv_dim = 2 * key_dim + d
    main = conv_dim + d
    tm = cfg["tm"]
    qkvz = _matmul(h, w_in, layer, 0, main, BF16, tm, _pick(main, (512, 256, 128)), "gdn_in_proj")
    w_tail, tail_layer, tail_lo = _tail_weights(w_in, layer, main)
    tail = _matmul(h, w_tail, tail_layer, tail_lo, tail_lo + LANES, F32, tm, LANES, "gdn_in_proj_gates")
    qk = _conv_silu(cfg, qkvz, 0, 2 * key_dim, conv_w[:, :2 * key_dim], jnp.zeros((2 * key_dim,), F32),
                    norm_ch=2 * key_dim, q_ch=key_dim, q_scale=GDN_HEAD ** -0.5)
    v = _conv_silu(cfg, qkvz, 2 * key_dim, d, conv_w[:, 2 * key_dim:], jnp.zeros((d,), F32))
    a_pos = jnp.exp(a_log.astype(F32)).reshape(-1)
    zeros = jnp.zeros((2 * hv,), F32)
    bias_full = _pad_lanes(jnp.concatenate([zeros, dt_bias.astype(F32).reshape(-1)]))
    amul_full = _pad_lanes(jnp.concatenate([zeros, -a_pos]))
    hps = min(GDN_HEADS_PER_STEP, hv)
    bc, gc, _, gr = _prep(cfg, tail, bias_full, amul_full, kind="gdn", chunk=GDN_CHUNK,
                           n_heads=hv, hps=hps)
    o_f = _gdn_scan(cfg, qk, v, bc, gc, gr, rev=False)
    y = _gdn_scan(cfg, qk, v, bc, gc, gr, rev=True, prev=o_f, qkvz=qkvz,
                  norm_w=norm_w.reshape(1, GDN_HEAD).astype(F32))
    return _matmul(y, w_out, layer, 0, d, BF16, tm, _pick(d, (512, 256, 128)), "gdn_out_proj")


def _ffn(cfg, h, layer, w_in, w_out_bf):
    d = cfg["d"]
    d_ff = w_in.shape[-1] // 2
    hid = _matmul_swiglu(h, w_in, layer, cfg["tm"], _pick(d_ff, (256, 128)))
    return _matmul(hid, w_out_bf, layer, 0, d, BF16, cfg["tm_out"], _pick(d, (512, 256, 128)), "ffn_out")


def kernel(x, c, ctx, c_ctx, ada_down, ada_up, ada_bias, norm_mix, norm_ffn, ffn_in, ffn_out,
           ssd_in, ssd_conv_w, ssd_conv_b, ssd_a_log, ssd_dt_bias, ssd_d, ssd_norm, ssd_out,
           gdn_in, gdn_conv_w, gdn_a_log, gdn_dt_bias, gdn_norm, gdn_out, final_norm):
    batch, seq, d = x.shape
    ctx_len = ctx.shape[1]
    depth = ada_down.shape[0]
    rows = seq // GRID_W
    assert rows == ROW_TILE and ctx_len % CONV_ROWS == 0 and seq % CONV_ROWS == 0
    t_lat, t_ctx = batch * seq, batch * ctx_len
    t = t_lat + t_ctx
    cfg = dict(d=d, batch=batch, seq=seq, ctx_len=ctx_len, cols=GRID_W, t_lat=t_lat, t_ctx=t_ctx, t=t,
               tm=_pick(t, (1536, 1024, 512, 256, 128)), tm_out=_pick(t, (512, 256, 128)))
    assert t_ctx % GRID_W == 0

    n_streams = batch + 1
    cv = jnp.concatenate([c.astype(F32), c_ctx.astype(F32)[None, :],
                          jnp.zeros((8 - n_streams % 8 if n_streams % 8 else 0, d), F32)], axis=0)
    mods = _ada_modulation(cv, ada_down, ada_up, ada_bias)

    def mod(layer, which):
        return mods[layer, :n_streams, which * d:(which + 1) * d].reshape(n_streams, 1, d)

    ffn_out_bf = ffn_out.astype(BF16)

    x_lat = x.reshape(t_lat, d)
    x_ctx = ctx.reshape(t_ctx, d)
    pending, pending_gate = None, None
    col_major = False
    for i in range(depth):
        want_col_major = (i // N_MIXERS) % 2 == 1
        assert want_col_major or not col_major
        j = i // N_MIXERS
        x_lat, x_ctx, h = _norm_modulate(cfg, x_lat, x_ctx, pending, pending_gate,
                                         norm_mix[i].reshape(1, d), mod(i, 0), mod(i, 1),
                                         to_col_major=want_col_major and not col_major)
        col_major = want_col_major
        if i % N_MIXERS == 0:
            y = _ssd_mixer(cfg, h, j, ssd_in, ssd_conv_w[j], ssd_conv_b[j], ssd_a_log[j],
                           ssd_dt_bias[j], ssd_d[j], ssd_norm[j], ssd_out)
        else:
            y = _gdn_mixer(cfg, h, j, gdn_in, gdn_conv_w[j], gdn_a_log[j], gdn_dt_bias[j],
                           gdn_norm[j], gdn_out)
        x_lat, x_ctx, h = _norm_modulate(cfg, x_lat, x_ctx, y, mod(i, 2),
                                         norm_ffn[i].reshape(1, d), mod(i, 3), mod(i, 4),
                                         to_col_major=False)
        pending, pending_gate = _ffn(cfg, h, i, ffn_in, ffn_out_bf), mod(i, 5)
    out = _final_norm(cfg, x_lat, pending, pending_gate, final_norm.reshape(1, d),
                      from_col_major=col_major)
    return out.reshape(batch, seq, d)
```

```python
import functools

import jax
import jax.numpy as jnp
from jax import lax
from jax.experimental import pallas as pl
from jax.experimental.pallas import tpu as pltpu

F32 = jnp.float32
BF16 = jnp.bfloat16
EPS = 1e-6

GRID_W = 64
N_MIXERS = 2
N_MOD = 6
SSD_HEAD_DIM = 64
SSD_GROUPS = 8
SSD_STATE = 128
SSD_CHUNK = 128
SSD_GROUPS_PER_STEP = 4
GDN_HEAD = 128
GDN_CHUNK = 64
GDN_HEADS_PER_STEP = 16

ROW_TILE = 128
NORM_ROWS = 256
CONV_ROWS = 256
HALO = 8
LANES = 128
CONV_PITCH = 36
CONV_SLAB_UNROLL = 2
V7X_VMEM_LIMIT = 56 * 1024 * 1024


def _cparams(sem, vmem=None):
    return pltpu.CompilerParams(dimension_semantics=sem, vmem_limit_bytes=vmem or V7X_VMEM_LIMIT)


def _sigmoid(x):
    return 1.0 / (1.0 + jnp.exp(-x))


def _silu(x):
    h = 0.5 * x
    return h + h * jnp.tanh(h)


def _softplus(x):
    return jnp.maximum(x, 0.0) + jnp.log1p(jnp.exp(-jnp.abs(x)))


def _pick(n, candidates):
    for c in candidates:
        if n % c == 0:
            return c
    raise ValueError(f"no tile for {n} among {candidates}")


def _split3(a):
    hi = a.astype(BF16).astype(F32)
    r1 = a - hi
    mid = r1.astype(BF16).astype(F32)
    lo = (r1 - mid).astype(BF16).astype(F32)
    return hi, mid, lo


def _dot(a, b):
    return jnp.dot(a, b, preferred_element_type=F32)


def _dot_exact_rhs(m, a):
    hi, mid, lo = _split3(a)
    mb = m.astype(BF16)
    return _dot(mb, hi.astype(BF16)) + _dot(mb, mid.astype(BF16)) + _dot(mb, lo.astype(BF16))


def _ada_kernel(cv_ref, down_ref, up_ref, bias_ref, o_ref):
    t = jnp.dot(_silu(cv_ref[...]), down_ref[...], preferred_element_type=F32,
                precision=lax.Precision.HIGHEST)
    o_ref[...] = jnp.dot(t, up_ref[...], preferred_element_type=F32,
                         precision=lax.Precision.HIGHEST) + bias_ref[...]


def _ada_modulation(cv, down, up, bias):
    depth, d, r = down.shape
    n = up.shape[-1]
    tn = _pick(n, (6144, 3072, 1536, 768, 512, 256, 128))
    rows = cv.shape[0]
    return pl.pallas_call(
        _ada_kernel,
        grid=(depth, n // tn),
        in_specs=[
            pl.BlockSpec((rows, d), lambda l, j: (0, 0)),
            pl.BlockSpec((None, d, r), lambda l, j: (l, 0, 0)),
            pl.BlockSpec((None, r, tn), lambda l, j: (l, 0, j)),
            pl.BlockSpec((None, 1, tn), lambda l, j: (l, 0, j)),
        ],
        out_specs=pl.BlockSpec((None, rows, tn), lambda l, j: (l, 0, j)),
        out_shape=jax.ShapeDtypeStruct((depth, rows, n), F32),
        compiler_params=_cparams(("arbitrary", "arbitrary")),
        name="ada_modulation",
    )(cv, down, up, bias.reshape(depth, 1, n))


def _norm_kernel(*refs, has_y, n_lat_tiles):
    if has_y:
        (xl_ref, xc_ref, yl_ref, yc_ref, g_ref, w_ref, sh_ref, sc_ref,
         xlo_ref, xco_ref, h_ref) = refs
    else:
        xl_ref, xc_ref, w_ref, sh_ref, sc_ref, h_ref = refs

    def run(x_ref, y_ref, xo_ref):
        x = x_ref[...]
        if has_y:
            x = x + g_ref[...] * y_ref[...]
            xo_ref[...] = x
        xn = x * lax.rsqrt(jnp.mean(x * x, axis=-1, keepdims=True) + EPS) * w_ref[...]
        h_ref[...] = (xn * (1.0 + sc_ref[...]) + sh_ref[...]).astype(h_ref.dtype)

    i = pl.program_id(0)

    @pl.when(i < n_lat_tiles)
    def _():
        run(xl_ref, yl_ref if has_y else None, xlo_ref if has_y else None)

    @pl.when(i >= n_lat_tiles)
    def _():
        run(xc_ref, yc_ref if has_y else None, xco_ref if has_y else None)


def _norm_modulate(cfg, x_lat, x_ctx, y, gate, w, shift, scale, *, to_col_major):
    d = cfg["d"]
    tr = ROW_TILE if to_col_major else NORM_ROWS
    assert cfg["seq"] % tr == 0 and cfg["t_ctx"] % tr == 0
    n_lat, n_ctx = cfg["t_lat"] // tr, cfg["t_ctx"] // tr
    per_batch = cfg["seq"] // tr
    cols = cfg["cols"]
    has_y = y is not None

    def lat_tile(i):
        return jnp.minimum(i, n_lat - 1)

    def ctx_tile(i):
        return jnp.maximum(i - n_lat, 0)

    def stream(i):
        return jnp.where(i < n_lat, i // per_batch, cfg["batch"])

    if to_col_major:
        def strided(i):
            il = lat_tile(i)
            return (il // cols, il % cols)
        xl_in = x_lat.reshape(cfg["t_lat"] // cols, cols * d)
        xl_spec = pl.BlockSpec((tr, d), strided)
    else:
        xl_in = x_lat
        xl_spec = pl.BlockSpec((tr, d), lambda i: (lat_tile(i), 0))
    xc_spec = pl.BlockSpec((tr, d), lambda i: (ctx_tile(i), 0))
    mod_spec = pl.BlockSpec((None, 1, d), lambda i: (stream(i), 0, 0))
    w_spec = pl.BlockSpec((1, d), lambda i: (0, 0))
    h_spec = pl.BlockSpec((tr, d), lambda i: (i, 0))
    h_shape = jax.ShapeDtypeStruct((cfg["t"], d), BF16)

    if has_y:
        if to_col_major:
            yl_in = y.reshape(cfg["t"] // cols, cols * d)
            yl_spec = pl.BlockSpec((tr, d), strided)
        else:
            yl_in = y
            yl_spec = pl.BlockSpec((tr, d), lambda i: (lat_tile(i), 0))
        yc_spec = pl.BlockSpec((tr, d), lambda i: (n_lat + ctx_tile(i), 0))
        ins = (xl_in, x_ctx, yl_in, y, gate, w, shift, scale)
        in_specs = [xl_spec, xc_spec, yl_spec, yc_spec, mod_spec, w_spec, mod_spec, mod_spec]
        out_specs = [pl.BlockSpec((tr, d), lambda i: (lat_tile(i), 0)), xc_spec, h_spec]
        out_shape = [jax.ShapeDtypeStruct((cfg["t_lat"], d), F32),
                     jax.ShapeDtypeStruct((cfg["t_ctx"], d), F32), h_shape]
    else:
        ins = (xl_in, x_ctx, w, shift, scale)
        in_specs = [xl_spec, xc_spec, w_spec, mod_spec, mod_spec]
        out_specs = h_spec
        out_shape = h_shape
    out = pl.pallas_call(
        functools.partial(_norm_kernel, has_y=has_y, n_lat_tiles=n_lat),
        grid=(n_lat + n_ctx,),
        in_specs=in_specs, out_specs=out_specs, out_shape=out_shape,
        compiler_params=_cparams(("arbitrary",)),
        name="norm_modulate",
    )(*ins)
    if has_y:
        return out
    return x_lat, x_ctx, out


def _final_kernel(x_ref, y_ref, g_ref, w_ref, o_ref):
    x = x_ref[...] + g_ref[...] * y_ref[...]
    o_ref[...] = x * lax.rsqrt(jnp.mean(x * x, axis=-1, keepdims=True) + EPS) * w_ref[...]


def _final_norm(cfg, x_lat, y, gate, w, *, from_col_major):
    d, tr = cfg["d"], ROW_TILE
    n_lat = cfg["t_lat"] // tr
    per_batch = cfg["seq"] // tr
    cols = cfg["cols"]
    plain = pl.BlockSpec((tr, d), lambda i: (i, 0))
    if from_col_major:
        out_spec = pl.BlockSpec((tr, d), lambda i: (i // cols, i % cols))
        out_shape = jax.ShapeDtypeStruct((cfg["t_lat"] // cols, cols * d), F32)
    else:
        out_spec, out_shape = plain, jax.ShapeDtypeStruct((cfg["t_lat"], d), F32)
    out = pl.pallas_call(
        _final_kernel,
        grid=(n_lat,),
        in_specs=[plain, plain,
                  pl.BlockSpec((None, 1, d), lambda i: (i // per_batch, 0, 0)),
                  pl.BlockSpec((1, d), lambda i: (0, 0))],
        out_specs=out_spec, out_shape=out_shape,
        compiler_params=_cparams(("arbitrary",)),
        name="final_norm",
    )(x_lat, y, gate, w)
    return out.reshape(cfg["t_lat"], d)


def _mm_kernel(a_ref, w_ref, o_ref):
    o_ref[...] = _dot(a_ref[...], w_ref[...].astype(BF16)).astype(o_ref.dtype)


def _matmul(a, w, layer, col_lo, col_hi, out_dtype, tm, tn, name):
    m, k = a.shape
    n = col_hi - col_lo
    assert m % tm == 0 and n % tn == 0 and col_lo % tn == 0
    off = col_lo // tn
    return pl.pallas_call(
        _mm_kernel,
        grid=(m // tm, n // tn),
        in_specs=[pl.BlockSpec((tm, k), lambda i, j: (i, 0)),
                  pl.BlockSpec((None, k, tn), lambda i, j: (layer, 0, off + j))],
        out_specs=pl.BlockSpec((tm, tn), lambda i, j: (i, j)),
        out_shape=jax.ShapeDtypeStruct((m, n), out_dtype),
        compiler_params=_cparams(("arbitrary", "arbitrary")),
        name=name,
    )(a, w)


def _swiglu_kernel(a_ref, wg_ref, wu_ref, o_ref):
    a = a_ref[...]
    g = _dot(a, wg_ref[...].astype(BF16))
    u = _dot(a, wu_ref[...].astype(BF16))
    o_ref[...] = (_silu(g) * u).astype(o_ref.dtype)


def _matmul_swiglu(a, w, layer, tm, tn):
    m, k = a.shape
    d_ff = w.shape[-1] // 2
    assert m % tm == 0 and d_ff % tn == 0
    nj = d_ff // tn
    return pl.pallas_call(
        _swiglu_kernel,
        grid=(m // tm, nj),
        in_specs=[pl.BlockSpec((tm, k), lambda i, j: (i, 0)),
                  pl.BlockSpec((None, k, tn), lambda i, j: (layer, 0, j)),
                  pl.BlockSpec((None, k, tn), lambda i, j: (layer, 0, nj + j))],
        out_specs=pl.BlockSpec((tm, tn), lambda i, j: (i, j)),
        out_shape=jax.ShapeDtypeStruct((m, d_ff), BF16),
        compiler_params=_cparams(("arbitrary", "arbitrary")),
        name="ffn_in_swiglu",
    )(a, w, w)


def _conv_kernel(prev_ref, cur_ref, next_ref, w_ref, b_ref, o_ref, scr_ref, out_ref, *,
                 tiles_per_lat_seq, n_lat_tiles, tiles_per_ctx_seq, n_norm_tiles, n_q_tiles, q_scale):
    i, j = pl.program_id(0), pl.program_id(1)
    ts = cur_ref.shape[0]
    ksz = w_ref.shape[0]
    is_lat = i < n_lat_tiles
    pos = jnp.where(is_lat, i % tiles_per_lat_seq, (i - n_lat_tiles) % tiles_per_ctx_seq)
    per_seq = jnp.where(is_lat, tiles_per_lat_seq, tiles_per_ctx_seq)
    keep_prev = jnp.where(pos == 0, 0.0, 1.0)
    keep_next = jnp.where(pos == per_seq - 1, 0.0, 1.0)
    if n_norm_tiles:
        normed = j < n_norm_tiles
        scale = jnp.where(j < n_q_tiles, q_scale, 1.0)
    first = HALO - ksz // 2
    n_slabs = cur_ref.shape[1] // LANES

    def one_slab(sl):
        lanes = pl.ds(pl.multiple_of(sl * LANES, LANES), LANES)
        scr_ref[sl, 0:HALO, :] = prev_ref[:, lanes].astype(F32) * keep_prev
        scr_ref[sl, HALO:HALO + ts, :] = cur_ref[:, lanes].astype(F32)
        scr_ref[sl, HALO + ts:2 * HALO + ts, :] = next_ref[:, lanes].astype(F32) * keep_next
        scr_ref[sl, 2 * HALO + ts:, :] = jnp.zeros((scr_ref.shape[1] - 2 * HALO - ts, LANES), F32)
        wk = [jnp.broadcast_to(w_ref[k:k + 1, lanes], (8, LANES)) for k in range(ksz)]
        bias = jnp.broadcast_to(b_ref[:, lanes], (8, LANES))
        taps = [scr_ref[sl, pl.ds(first + g, 8, stride=CONV_PITCH), :]
                for g in range(CONV_PITCH + ksz - 1)]
        for g in range(CONV_PITCH):
            acc = bias + wk[0] * taps[g]
            for k in range(1, ksz):
                acc = acc + wk[k] * taps[g + k]
            y = _silu(acc)
            if n_norm_tiles:
                r = lax.rsqrt(jnp.sum(y * y, axis=-1, keepdims=True) + EPS) * scale
                y = y * jnp.where(normed, r, 1.0)
            out_ref[sl, pl.ds(g, 8, stride=CONV_PITCH), :] = y
        o_ref[:, lanes] = out_ref[sl, 0:ts, :].astype(o_ref.dtype)

    @pl.loop(0, n_slabs // CONV_SLAB_UNROLL)
    def _(it):
        for u in range(CONV_SLAB_UNROLL):
            one_slab(it * CONV_SLAB_UNROLL + u)


def _conv_silu(cfg, src, col_lo, n_ch, w, b, *, norm_ch=0, q_ch=0, q_scale=1.0):
    t, ts = cfg["t"], CONV_ROWS
    tc = next(c for c in (2048, 1024, 512, 256)
              if all(v % c == 0 for v in (n_ch, col_lo, norm_ch, q_ch)))
    assert (tc // LANES) % CONV_SLAB_UNROLL == 0
    assert cfg["seq"] % ts == 0 and cfg["ctx_len"] % ts == 0 and 8 * CONV_PITCH >= ts
    ksz = w.shape[0]
    last_row = (HALO - ksz // 2) + (CONV_PITCH + ksz - 2) + 7 * CONV_PITCH
    in_rows = -(-(last_row + 1) // 8) * 8
    assert in_rows >= ts + 2 * HALO
    off = col_lo // tc
    hb = ts // HALO
    last_hb = t // HALO - 1
    kern = functools.partial(
        _conv_kernel, tiles_per_lat_seq=cfg["seq"] // ts, n_lat_tiles=cfg["t_lat"] // ts,
        tiles_per_ctx_seq=cfg["ctx_len"] // ts, n_norm_tiles=norm_ch // tc, n_q_tiles=q_ch // tc,
        q_scale=q_scale)
    return pl.pallas_call(
        kern,
        grid=(t // ts, n_ch // tc),
        in_specs=[
            pl.BlockSpec((HALO, tc), lambda i, j: (jnp.maximum(i * hb - 1, 0), off + j)),
            pl.BlockSpec((ts, tc), lambda i, j: (i, off + j)),
            pl.BlockSpec((HALO, tc), lambda i, j: (jnp.minimum((i + 1) * hb, last_hb), off + j)),
            pl.BlockSpec((w.shape[0], tc), lambda i, j: (0, j)),
            pl.BlockSpec((1, tc), lambda i, j: (0, j)),
        ],
        out_specs=pl.BlockSpec((ts, tc), lambda i, j: (i, j)),
        out_shape=jax.ShapeDtypeStruct((t, n_ch), BF16),
        scratch_shapes=[pltpu.VMEM((tc // LANES, in_rows, LANES), F32),
                        pltpu.VMEM((tc // LANES, 8 * CONV_PITCH, LANES), F32)],
        compiler_params=_cparams(("arbitrary", "arbitrary")),
        name="conv_silu",
    )(src, src, src, w, b.reshape(1, n_ch))


def _prep_kernel(tail_ref, bias_ref, amul_ref, p1c_ref, p2c_ref, p1r_ref, p2r_ref, *,
                 kind, chunk, p2_off, split_lane, hps):
    tr = ROW_TILE
    row = lax.broadcasted_iota(jnp.int32, (tr, tr), 0)
    col = lax.broadcasted_iota(jnp.int32, (tr, tr), 1)
    same = (row // chunk) == (col // chunk)
    lower = jnp.where(same & (col <= row), 1.0, 0.0)
    upper = jnp.where(same & (col >= row), 1.0, 0.0)
    lane = lax.broadcasted_iota(jnp.int32, (tr, LANES), 1)
    for sub in range(tail_ref.shape[0] // tr):
        rows = slice(sub * tr, (sub + 1) * tr)
        raw = tail_ref[rows, :]
        sp = _softplus(raw + bias_ref[...])
        p1 = sp if kind == "ssd" else _sigmoid(raw)
        a = sp * amul_ref[...]
        cum = jnp.where(lane < split_lane, _dot_exact_rhs(lower, a), _dot_exact_rhs(upper, a))
        p1t = p1.T
        cumt = cum.T
        for n in range(p1c_ref.shape[0]):
            p1c_ref[n, rows, :] = p1[:, n * hps:(n + 1) * hps]
            p2c_ref[n, rows, :] = cum[:, p2_off + n * hps:p2_off + (n + 1) * hps]
            p1r_ref[n, :, rows] = p1t[n * hps:(n + 1) * hps, :]
            p2r_ref[n, :, rows] = cumt[p2_off + n * hps:p2_off + (n + 1) * hps, :]


def _prep(cfg, tail, bias_full, amul_full, *, kind, chunk, n_heads, hps):
    t = cfg["t"]
    tr = _pick(t, (4 * ROW_TILE, 2 * ROW_TILE, ROW_TILE))
    nblk = 2 * n_heads // hps
    p2_off = 0 if kind == "ssd" else 2 * n_heads
    split_lane = p2_off + n_heads
    kern = functools.partial(_prep_kernel, kind=kind, chunk=chunk, p2_off=p2_off,
                             split_lane=split_lane, hps=hps)
    col_spec = pl.BlockSpec((nblk, tr, hps), lambda i: (0, i, 0))
    row_spec = pl.BlockSpec((nblk, hps, tr), lambda i: (0, 0, i))
    col_shape = jax.ShapeDtypeStruct((nblk, t, hps), F32)
    row_shape = jax.ShapeDtypeStruct((nblk, hps, t), F32)
    return pl.pallas_call(
        kern,
        grid=(t // tr,),
        in_specs=[pl.BlockSpec((tr, 128), lambda i: (i, 0)),
                  pl.BlockSpec((1, 128), lambda i: (0, 0)),
                  pl.BlockSpec((1, 128), lambda i: (0, 0))],
        out_specs=[col_spec, col_spec, row_spec, row_spec],
        out_shape=[col_shape, col_shape, row_shape, row_shape],
        compiler_params=_cparams(("arbitrary",)),
        name=f"prep_{kind}",
    )(tail, bias_full, amul_full)


def _fwd_block(cfg, b, i):
    ncc, ncl = cfg["ctx_len"] // ROW_TILE, cfg["seq"] // ROW_TILE
    ctx = cfg["batch"] * ncl + b * ncc + i
    lat = b * ncl + (i - ncc)
    return jnp.where(i < ncc, ctx, lat)


def _bwd_block(cfg, b, i):
    ncc, ncl = cfg["ctx_len"] // ROW_TILE, cfg["seq"] // ROW_TILE
    ctx = cfg["batch"] * ncl + b * ncc + (ncc - 1 - i)
    lat = b * ncl + (ncl - 1 - (i - ncc))
    return jnp.where(i < ncc, ctx, lat)


def _ssd_kernel(x_ref, b_ref, c_ref, z_ref, dtc_f_ref, dtc_b_ref, cuc_f_ref, cuc_b_ref,
                dtr_f_ref, dtr_b_ref, cur_f_ref, cur_b_ref, dskip_ref, nw_ref,
                o_ref, sf_ref, sb_ref, store_ref, *, nb, ncc):
    i = pl.program_id(2)
    q = x_ref.shape[0]
    gps = dtc_f_ref.shape[0]
    hpg = dtc_f_ref.shape[-1]
    width = x_ref.shape[1] // gps
    n = b_ref.shape[1] // gps
    hd = width // hpg
    groups = range(gps)
    expand = jnp.where(lax.broadcasted_iota(jnp.int32, (hpg, width), 1) // hd
                       == lax.broadcasted_iota(jnp.int32, (hpg, width), 0), 1.0, 0.0).astype(BF16)

    def widen(a):
        hi = a.astype(BF16)
        lo = (a - hi.astype(F32)).astype(BF16)
        return _dot(hi, expand) + _dot(lo, expand)

    def cols(gg, w):
        return slice(gg * w, (gg + 1) * w)

    @pl.when(i == 0)
    def _():
        sb_ref[...] = jnp.zeros_like(sb_ref)

    @pl.when(i == nb)
    def _():
        sf_ref[...] = jnp.zeros_like(sf_ref)

    @pl.when(i < nb)
    def _():
        for gg in groups:
            cub = cuc_b_ref[gg]
            tot = cub[0:1, :]
            wide = widen(jnp.concatenate([dtc_b_ref[gg] * jnp.exp(tot - cub), jnp.exp(cub[0:8, :])], axis=0))
            wgt, dec = wide[:q], wide[q:q + 1]
            xs = (x_ref[:, cols(gg, width)].astype(F32) * wgt).astype(BF16)
            cs = lax.dot_general(b_ref[:, cols(gg, n)], xs, (((0,), (0,)), ((), ())),
                                 preferred_element_type=F32)
            store_ref[i, gg] = sb_ref[gg].astype(BF16)
            sb_ref[gg] = sb_ref[gg] * dec + cs

    @pl.when(i >= nb)
    def _():
        i2 = i - nb
        slot = jnp.where(i2 < ncc, ncc - 1 - i2, nb - 1 - (i2 - ncc))
        li = lax.broadcasted_iota(jnp.int32, (q, q), 0)
        si = lax.broadcasted_iota(jnp.int32, (q, q), 1)
        causal = si <= li
        lane = lax.broadcasted_iota(jnp.int32, (q, 2 * hd), 1)
        cb = [lax.dot_general(c_ref[:, cols(gg, n)], b_ref[:, cols(gg, n)], (((1,), (1,)), ((), ())),
                              preferred_element_type=F32) for gg in groups]
        wide = []
        for gg in groups:
            cuf = cuc_f_ref[gg]
            wide.append(widen(jnp.concatenate(
                [jnp.exp(cuf), jnp.exp(cuc_b_ref[gg]), dtc_f_ref[gg] * jnp.exp(cuf[q - 1:q, :] - cuf)], axis=0)))
        y_off = []
        for gg in groups:
            ef, eb, wf = wide[gg][:q], wide[gg][q:2 * q], wide[gg][2 * q:]
            cm = c_ref[:, cols(gg, n)]
            y_off.append(_dot(cm, sf_ref[gg].astype(BF16)) * ef + _dot(cm, store_ref[slot, gg]) * eb)
            xs = (x_ref[:, cols(gg, width)].astype(F32) * wf).astype(BF16)
            cs = lax.dot_general(b_ref[:, cols(gg, n)], xs, (((0,), (0,)), ((), ())),
                                 preferred_element_type=F32)
            sf_ref[gg] = sf_ref[gg] * ef[q - 1:q, :] + cs

        y_diag = [[] for _ in groups]
        for p in range(hpg // 2):
            for gg in groups:
                xp = x_ref[:, gg * width + p * 2 * hd:gg * width + (p + 1) * 2 * hd]
                acc = None
                for half in range(2):
                    r = 2 * p + half
                    log_dec = jnp.where(causal, cuc_f_ref[gg, :, r:r + 1] - cur_f_ref[gg, r:r + 1, :],
                                        cuc_b_ref[gg, :, r:r + 1] - cur_b_ref[gg, r:r + 1, :])
                    dt_row = jnp.where(causal, dtr_f_ref[gg, r:r + 1, :], dtr_b_ref[gg, r:r + 1, :])
                    mix = (cb[gg] * (jnp.exp(log_dec) * dt_row)).astype(BF16)
                    keep = (lane < hd) if half == 0 else (lane >= hd)
                    part = _dot(mix, jnp.where(keep, xp, jnp.zeros_like(xp)))
                    acc = part if acc is None else acc + part
                y_diag[gg].append(acc)

        for gg in groups:
            cb_diag = jnp.sum(jnp.where(si == li, cb[gg], 0.0), axis=-1, keepdims=True)
            own_b = widen(dtc_b_ref[gg] * cb_diag)
            xf = x_ref[:, cols(gg, width)].astype(F32)
            y = (dskip_ref[:, cols(gg, width)] + own_b) * xf + y_off[gg] + jnp.concatenate(y_diag[gg], axis=1)
            gated = y * _silu(z_ref[:, cols(gg, width)].astype(F32))
            o_ref[:, cols(gg, width)] = (
                gated * lax.rsqrt(jnp.mean(gated * gated, axis=-1, keepdims=True) + EPS)
                * nw_ref[:, cols(gg, width)]).astype(o_ref.dtype)


def _ssd_scan(cfg, xbc, zx, dtc, cuc, dtr, cur, d_skip, norm_w):
    t, tr, batch = cfg["t"], ROW_TILE, cfg["batch"]
    g, n = SSD_GROUPS, SSD_STATE
    gps = SSD_GROUPS_PER_STEP if g % SSD_GROUPS_PER_STEP == 0 else 1
    inner = cfg["d"]
    width = inner // g
    hpg = width // SSD_HEAD_DIM
    assert hpg % 2 == 0 and tr == SSD_CHUNK and inner % (gps * n) == 0
    ncc, ncl = cfg["ctx_len"] // tr, cfg["seq"] // tr
    nb = ncc + ncl
    b_off, c_off = inner // (gps * n), (inner + g * n) // (gps * n)

    def blk(b, i):
        return jnp.where(i < nb, _bwd_block(cfg, b, i), _fwd_block(cfg, b, i - nb))

    def oblk(b, i):
        return _fwd_block(cfg, b, jnp.maximum(i - nb, 0))

    col = lambda d: pl.BlockSpec((gps, tr, hpg), lambda b, gi, i: (d * (g // gps) + gi, blk(b, i), 0))
    row = lambda d: pl.BlockSpec((gps, hpg, tr), lambda b, gi, i: (d * (g // gps) + gi, 0, blk(b, i)))
    kern = functools.partial(_ssd_kernel, nb=nb, ncc=ncc)
    return pl.pallas_call(
        kern,
        grid=(batch, g // gps, 2 * nb),
        in_specs=[
            pl.BlockSpec((tr, gps * width), lambda b, gi, i: (blk(b, i), gi)),
            pl.BlockSpec((tr, gps * n), lambda b, gi, i: (blk(b, i), b_off + gi)),
            pl.BlockSpec((tr, gps * n), lambda b, gi, i: (blk(b, i), c_off + gi)),
            pl.BlockSpec((tr, gps * width), lambda b, gi, i: (oblk(b, i), gi)),
            col(0), col(1), col(0), col(1), row(0), row(1), row(0), row(1),
            pl.BlockSpec((1, gps * width), lambda b, gi, i: (0, gi)),
            pl.BlockSpec((1, gps * width), lambda b, gi, i: (0, gi)),
        ],
        out_specs=pl.BlockSpec((tr, gps * width), lambda b, gi, i: (oblk(b, i), gi)),
        out_shape=jax.ShapeDtypeStruct((t, inner), BF16),
        scratch_shapes=[pltpu.VMEM((gps, n, width), F32), pltpu.VMEM((gps, n, width), F32),
                        pltpu.VMEM((nb, gps, n, width), BF16)],
        compiler_params=_cparams(("arbitrary", "arbitrary", "arbitrary")),
        name="ssd_scan",
    )(xbc, xbc, xbc, zx, dtc, dtc, cuc, cuc, dtr, dtr, cur, cur, d_skip, norm_w)


def _gdn_kernel(*refs, rev, finish):
    if finish:
        (q_ref, k_ref, v_ref, bc_ref, gc_ref, gr_ref, prev_ref, z_ref, nw_ref,
         o_ref, s_ref) = refs
    else:
        q_ref, k_ref, v_ref, bc_ref, gc_ref, gr_ref, o_ref, s_ref = refs
    c = GDN_CHUNK
    hw = GDN_HEAD
    n_heads = s_ref.shape[0]
    rep = n_heads // (k_ref.shape[1] // hw)

    @pl.when(pl.program_id(2) == 0)
    def _():
        s_ref[...] = jnp.zeros_like(s_ref)

    ii = lax.broadcasted_iota(jnp.int32, (c, c), 0)
    jj = lax.broadcasted_iota(jnp.int32, (c, c), 1)
    incl = (jj >= ii) if rev else (jj <= ii)
    strict = (jj > ii) if rev else (jj < ii)
    eye = jnp.where(ii == jj, 1.0, 0.0)
    n_chunks = q_ref.shape[0] // c
    order = range(n_chunks - 1, -1, -1) if rev else range(n_chunks)
    last = 0 if rev else c - 1

    heads = range(n_heads)
    steps = c.bit_length() - 1
    qk, kk, inv, pw, qkd, res, uw = {}, {}, {}, {}, {}, {}, {}

    def rows_of(ci):
        return slice(ci * c, (ci + 1) * c)

    assert rep == 2
    row2 = lax.broadcasted_iota(jnp.int32, (c, 2 * c), 0)
    lane2 = lax.broadcasted_iota(jnp.int32, (c, 2 * c), 1)
    col2, second = lane2 % c, lane2 >= c
    incl2 = (col2 >= row2) if rev else (col2 <= row2)
    strict2 = (col2 > row2) if rev else (col2 < row2)
    eye2 = jnp.where(col2 == row2, 1.0, 0.0)
    same_head = (lax.broadcasted_iota(jnp.int32, (2 * c, 2 * c), 0) // c
                 == lax.broadcasted_iota(jnp.int32, (2 * c, 2 * c), 1) // c)

    def block_diag(x):
        return jnp.where(same_head, jnp.concatenate([x, x], axis=0), jnp.zeros((), x.dtype))

    def gram(ci):
        for kh in range(n_heads // rep):
            k_in = k_ref[rows_of(ci), kh * hw:(kh + 1) * hw]
            both = lax.dot_general(jnp.concatenate([q_ref[rows_of(ci), kh * hw:(kh + 1) * hw], k_in], axis=0),
                                   jnp.concatenate([k_in, k_in], axis=0),
                                   (((1,), (1,)), ((), ())), preferred_element_type=F32)
            qk[ci, kh], kk[ci, kh] = both[:c], both[c:]

    sizes = [2 << k for k in range(steps)]
    joins = {b: ((row2 // b) == (col2 // b)) & ((row2 // (b // 2)) != (col2 // (b // 2))) for b in sizes}

    def construct(ci, kh):
        rows = rows_of(ci)
        h0, h1 = kh * rep, kh * rep + 1
        gc2 = jnp.where(second, gc_ref[rows, h1:h1 + 1], gc_ref[rows, h0:h0 + 1])
        bc2 = jnp.where(second, bc_ref[rows, h1:h1 + 1], bc_ref[rows, h0:h0 + 1])
        gr2 = jnp.concatenate([gr_ref[h0:h0 + 1, rows], gr_ref[h1:h1 + 1, rows]], axis=1)
        dec = jnp.where(incl2, jnp.exp(jnp.where(incl2, gc2 - gr2, 0.0)), 0.0)
        a = jnp.where(strict2, bc2 * kk[ci, kh] * dec, 0.0)
        pw[ci, kh] = a
        inv[ci, kh] = eye2 - jnp.where(joins[sizes[0]], a, 0.0)
        qkd[ci, kh] = (qk[ci, kh] * dec).astype(BF16)
        res[ci, kh] = None

    def level(s, it):
        b = sizes[(s + 1) // 2]
        if s % 2 == 1:
            if res[it] is not None:
                inv[it] = inv[it] - res[it]
            res[it] = _dot(inv[it].astype(BF16), block_diag(jnp.where(joins[b], pw[it], 0.0).astype(BF16)))
        else:
            res[it] = _dot(res[it].astype(BF16), block_diag(inv[it].astype(BF16)))

    def solve(ci, kh):
        rows = rows_of(ci)
        t_pair = (inv[ci, kh] - res[ci, kh]).astype(BF16)
        for half, h in enumerate((kh * rep, kh * rep + 1)):
            bc1, gc1 = bc_ref[rows, h:h + 1], gc_ref[rows, h:h + 1]
            rhs = jnp.concatenate(
                [(v_ref[rows, h * hw:(h + 1) * hw].astype(F32) * bc1).astype(BF16),
                 (k_ref[rows, kh * hw:(kh + 1) * hw].astype(F32) * (bc1 * jnp.exp(gc1))).astype(BF16)], axis=1)
            uw[ci, h] = _dot(t_pair, pad_half(rhs, half))

    def pad_half(x, half):
        zeros = jnp.zeros_like(x)
        return jnp.concatenate([zeros, x] if half else [x, zeros], axis=0)

    st, ws_qs, kv, intra, gtot = {}, {}, {}, {}, {}

    def state_read(ci):
        for h in heads:
            st[h] = s_ref[h]
            ws_qs[h] = _dot(jnp.concatenate([uw[ci, h][:, hw:].astype(BF16),
                                             q_ref[rows_of(ci), (h // rep) * hw:(h // rep + 1) * hw]], axis=0),
                            st[h].astype(BF16))

    def state_update(ci):
        rows = rows_of(ci)
        for h in heads:
            v_new = (uw[ci, h][:, :hw] - ws_qs[h][:c]).astype(BF16)
            gc1 = gc_ref[rows, h:h + 1]
            gtot[h] = gc1[last:last + 1, :]
            kh = h // rep
            k_dec = (k_ref[rows, kh * hw:(kh + 1) * hw].astype(F32) * jnp.exp(gtot[h] - gc1)).astype(BF16)
            kv[h] = lax.dot_general(k_dec, v_new, (((0,), (0,)), ((), ())), preferred_element_type=F32)
            intra[h] = _dot(qkd[ci, kh], pad_half(v_new, h % rep))

    def state_write(ci):
        rows = rows_of(ci)
        for h in heads:
            s_ref[h] = st[h] * jnp.exp(gtot[h]) + kv[h]
        for h in heads:
            out = jnp.exp(gc_ref[rows, h:h + 1]) * ws_qs[h][c:] + intra[h]
            cols = slice(h * hw, (h + 1) * hw)
            if finish:
                o = out + prev_ref[rows, cols]
                o = o * lax.rsqrt(jnp.mean(o * o, axis=-1, keepdims=True) + EPS) * nw_ref[...]
                o_ref[rows, cols] = (o * _silu(z_ref[rows, cols].astype(F32))).astype(o_ref.dtype)
            else:
                o_ref[rows, cols] = out.astype(o_ref.dtype)

    pairs = [(ci, kh) for ci in order for kh in range(n_heads // rep)]
    for ci in order:
        gram(ci)
    for pair in pairs:
        construct(*pair)
    for s in range(1, 2 * (steps - 1) + 1):
        for pair in pairs:
            level(s, pair)
    for pair in pairs:
        solve(*pair)
    for ci in order:
        state_read(ci)
        state_update(ci)
        state_write(ci)


def _gdn_scan(cfg, qk, v, bc, gc, gr, *, rev, prev=None, qkvz=None, norm_w=None):
    t, tr, batch = cfg["t"], ROW_TILE, cfg["batch"]
    hw = GDN_HEAD
    hv = cfg["d"] // hw
    hk = hv // 2
    hps = min(GDN_HEADS_PER_STEP, hv)
    kps = hps // 2
    nhb = hv // hps
    nb = (cfg["ctx_len"] + cfg["seq"]) // tr
    finish = prev is not None
    d = 1 if rev else 0
    order = _bwd_block if rev else _fwd_block

    def blk(b, i):
        return order(cfg, b, i)

    k_off = hk // kps
    z_off = (2 * hk + hv) // hps
    col = pl.BlockSpec((None, tr, hps), lambda b, hb, i: (d * nhb + hb, blk(b, i), 0))
    row = pl.BlockSpec((None, hps, tr), lambda b, hb, i: (d * nhb + hb, 0, blk(b, i)))
    wide = lambda off: pl.BlockSpec((tr, hps * hw), lambda b, hb, i: (blk(b, i), off + hb))
    in_specs = [
        pl.BlockSpec((tr, kps * hw), lambda b, hb, i: (blk(b, i), hb)),
        pl.BlockSpec((tr, kps * hw), lambda b, hb, i: (blk(b, i), k_off + hb)),
        wide(0), col, col, row,
    ]
    ins = [qk, qk, v, bc, gc, gr]
    if finish:
        in_specs += [wide(0), wide(z_off), pl.BlockSpec((1, hw), lambda b, hb, i: (0, 0))]
        ins += [prev, qkvz, norm_w]
    return pl.pallas_call(
        functools.partial(_gdn_kernel, rev=rev, finish=finish),
        grid=(batch, nhb, nb),
        in_specs=in_specs,
        out_specs=wide(0),
        out_shape=jax.ShapeDtypeStruct((t, hv * hw), BF16),
        scratch_shapes=[pltpu.VMEM((hps, hw, hw), F32)],
        compiler_params=_cparams(("arbitrary", "arbitrary", "arbitrary")),
        name="gdn_scan_bwd" if rev else "gdn_scan_fwd",
    )(*ins)


def _gdn_finish_kernel(of_ref, ob_ref, z_ref, nw_ref, o_ref):
    for h in range(o_ref.shape[1] // GDN_HEAD):
        cols = slice(h * GDN_HEAD, (h + 1) * GDN_HEAD)
        o = of_ref[:, cols].astype(F32) + ob_ref[:, cols].astype(F32)
        o = o * lax.rsqrt(jnp.mean(o * o, axis=-1, keepdims=True) + EPS) * nw_ref[...]
        o_ref[:, cols] = (o * _silu(z_ref[:, cols].astype(F32))).astype(o_ref.dtype)


def _gdn_finish(cfg, o_f, o_b, qkvz, z_lo, norm_w):
    t, d = cfg["t"], cfg["d"]
    tr = NORM_ROWS
    width = _pick(d, (1024, 512, 256, 128))
    assert t % tr == 0 and z_lo % width == 0
    z_off = z_lo // width
    spec = pl.BlockSpec((tr, width), lambda i, j: (i, j))
    return pl.pallas_call(
        _gdn_finish_kernel,
        grid=(t // tr, d // width),
        in_specs=[spec, spec, pl.BlockSpec((tr, width), lambda i, j: (i, z_off + j)),
                  pl.BlockSpec((1, GDN_HEAD), lambda i, j: (0, 0))],
        out_specs=spec,
        out_shape=jax.ShapeDtypeStruct((t, d), BF16),
        compiler_params=_cparams(("arbitrary", "arbitrary")),
        name="gdn_finish",
    )(o_f, o_b, qkvz, norm_w)


def _pad_lanes(v, width=128):
    v = v.reshape(1, -1).astype(F32)
    return jnp.pad(v, ((0, 0), (0, width - v.shape[1])))


def _pad_cols(w, width=128):
    return jnp.pad(w, ((0, 0), (0, width - w.shape[1])))


def _tail_weights(w, layer, main):
    if w.shape[-1] - main == LANES:
        return w, layer, main
    return _pad_cols(w[layer][:, main:])[None], 0, 0


def _ssd_mixer(cfg, h, layer, w_in, conv_w, conv_b, a_log, dt_bias, d_skip, norm_w, w_out):
    d = cfg["d"]
    heads = d // SSD_HEAD_DIM
    bc_dim = SSD_GROUPS * SSD_STATE
    main = 2 * d + 2 * bc_dim
    tm = cfg["tm"]
    zx = _matmul(h, w_in, layer, 0, main, BF16, tm, _pick(main, (512, 256, 128)), "ssd_in_proj")
    w_tail, tail_layer, tail_lo = _tail_weights(w_in, layer, main)
    tail = _matmul(h, w_tail, tail_layer, tail_lo, tail_lo + LANES, F32, tm, LANES, "ssd_in_proj_dt")
    xbc = _conv_silu(cfg, zx, d, d + 2 * bc_dim, conv_w, conv_b)
    a_neg = -jnp.exp(a_log.astype(F32))
    dtc, cuc, dtr, cur = _prep(cfg, tail, _pad_lanes(dt_bias), _pad_lanes(a_neg),
                               kind="ssd", chunk=SSD_CHUNK, n_heads=heads, hps=heads // SSD_GROUPS)
    y = _ssd_scan(cfg, xbc, zx, dtc, cuc, dtr, cur,
                  jnp.repeat(d_skip.astype(F32), SSD_HEAD_DIM).reshape(1, d), norm_w.reshape(1, d))
    return _matmul(y, w_out, layer, 0, d, BF16, tm, _pick(d, (512, 256, 128)), "ssd_out_proj")


def _gdn_mixer(cfg, h, layer, w_in, conv_w, a_log, dt_bias, norm_w, w_out):
    d = cfg["d"]
    hv = d // GDN_HEAD
    key_dim = (hv // 2) * GDN_HEAD
    conv_dim = 2 * key_dim + d
    main = conv_dim + d
    tm = cfg["tm"]
    qkvz = _matmul(h, w_in, layer, 0, main, BF16, tm, _pick(main, (512, 256, 128)), "gdn_in_proj")
    w_tail, tail_layer, tail_lo = _tail_weights(w_in, layer, main)
    tail = _matmul(h, w_tail, tail_layer, tail_lo, tail_lo + LANES, F32, tm, LANES, "gdn_in_proj_gates")
    qk = _conv_silu(cfg, qkvz, 0, 2 * key_dim, conv_w[:, :2 * key_dim], jnp.zeros((2 * key_dim,), F32),
                    norm_ch=2 * key_dim, q_ch=key_dim, q_scale=GDN_HEAD ** -0.5)
    v = _conv_silu(cfg, qkvz, 2 * key_dim, d, conv_w[:, 2 * key_dim:], jnp.zeros((d,), F32))
    a_pos = jnp.exp(a_log.astype(F32)).reshape(-1)
    zeros = jnp.zeros((2 * hv,), F32)
    bias_full = _pad_lanes(jnp.concatenate([zeros, dt_bias.astype(F32).reshape(-1)]))
    amul_full = _pad_lanes(jnp.concatenate([zeros, -a_pos]))
    hps = min(GDN_HEADS_PER_STEP, hv)
    bc, gc, _, gr = _prep(cfg, tail, bias_full, amul_full, kind="gdn", chunk=GDN_CHUNK,
                           n_heads=hv, hps=hps)
    o_f = _gdn_scan(cfg, qk, v, bc, gc, gr, rev=False)
    o_b = _gdn_scan(cfg, qk, v, bc, gc, gr, rev=True)
    y = _gdn_finish(cfg, o_f, o_b, qkvz, conv_dim, norm_w.reshape(1, GDN_HEAD).astype(F32))
    return _matmul(y, w_out, layer, 0, d, BF16, tm, _pick(d, (512, 256, 128)), "gdn_out_proj")


def _ffn(cfg, h, layer, w_in, w_out_bf):
    d = cfg["d"]
    d_ff = w_in.shape[-1] // 2
    hid = _matmul_swiglu(h, w_in, layer, cfg["tm"], _pick(d_ff, (256, 128)))
    return _matmul(hid, w_out_bf, layer, 0, d, BF16, cfg["tm_out"], _pick(d, (512, 256, 128)), "ffn_out")


def kernel(x, c, ctx, c_ctx, ada_down, ada_up, ada_bias, norm_mix, norm_ffn, ffn_in, ffn_out,
           ssd_in, ssd_conv_w, ssd_conv_b, ssd_a_log, ssd_dt_bias, ssd_d, ssd_norm, ssd_out,
           gdn_in, gdn_conv_w, gdn_a_log, gdn_dt_bias, gdn_norm, gdn_out, final_norm):
    batch, seq, d = x.shape
    ctx_len = ctx.shape[1]
    depth = ada_down.shape[0]
    rows = seq // GRID_W
    assert rows == ROW_TILE and ctx_len % CONV_ROWS == 0 and seq % CONV_ROWS == 0
    t_lat, t_ctx = batch * seq, batch * ctx_len
    t = t_lat + t_ctx
    cfg = dict(d=d, batch=batch, seq=seq, ctx_len=ctx_len, cols=GRID_W, t_lat=t_lat, t_ctx=t_ctx, t=t,
               tm=_pick(t, (1536, 1024, 512, 256, 128)), tm_out=_pick(t, (512, 256, 128)))
    assert t_ctx % GRID_W == 0

    n_streams = batch + 1
    cv = jnp.concatenate([c.astype(F32), c_ctx.astype(F32)[None, :],
                          jnp.zeros((8 - n_streams % 8 if n_streams % 8 else 0, d), F32)], axis=0)
    mods = _ada_modulation(cv, ada_down, ada_up, ada_bias)

    def mod(layer, which):
        return mods[layer, :n_streams, which * d:(which + 1) * d].reshape(n_streams, 1, d)

    ffn_out_bf = ffn_out.astype(BF16)

    x_lat = x.reshape(t_lat, d)
    x_ctx = ctx.reshape(t_ctx, d)
    pending, pending_gate = None, None
    col_major = False
    for i in range(depth):
        want_col_major = (i // N_MIXERS) % 2 == 1
        assert want_col_major or not col_major
        j = i // N_MIXERS
        x_lat, x_ctx, h = _norm_modulate(cfg, x_lat, x_ctx, pending, pending_gate,
                                         norm_mix[i].reshape(1, d), mod(i, 0), mod(i, 1),
                                         to_col_major=want_col_major and not col_major)
        col_major = want_col_major
        if i % N_MIXERS == 0:
            y = _ssd_mixer(cfg, h, j, ssd_in, ssd_conv_w[j], ssd_conv_b[j], ssd_a_log[j],
                           ssd_dt_bias[j], ssd_d[j], ssd_norm[j], ssd_out)
        else:
            y = _gdn_mixer(cfg, h, j, gdn_in, gdn_conv_w[j], gdn_a_log[j], gdn_dt_bias[j],
                           gdn_norm[j], gdn_out)
        x_lat, x_ctx, h = _norm_modulate(cfg, x_lat, x_ctx, y, mod(i, 2),
                                         norm_ffn[i].reshape(1, d), mod(i, 3), mod(i, 4),
                                         to_col_major=False)
        pending, pending_gate = _ffn(cfg, h, i, ffn_in, ffn_out_bf), mod(i, 5)
    out = _final_norm(cfg, x_lat, pending, pending_gate, final_norm.reshape(1, d),
                      from_col_major=col_major)
    return out.reshape(batch, seq, d)
```
